```python
import math
import jax, jax.numpy as jnp
from jax import lax
import numpy as np

D_MODEL = 1024
BATCH = 4
SEQ = 8192
DEPTH = 2

CHUNK = 64
HEAD_DIM = 64
A_HEADS = (D_MODEL // 2) // HEAD_DIM
A_PREV_CHUNKS = 8
A_MAX_REL = 128
B_HEADS = (D_MODEL // 2) // HEAD_DIM
B_KV_HEADS = max(1, B_HEADS // 4)
B_GROUP = B_HEADS // B_KV_HEADS
B_WINDOW = 128
B_PREV_CHUNKS = B_WINDOW // CHUNK
T5_BUCKETS = 32
T5_MAX_DIST = 128
POOL_WINDOWS = (2, 4, 8, 16)
POOL_GROUPS = len(POOL_WINDOWS)
POOL_CH = D_MODEL // POOL_GROUPS
D_FF = 4 * D_MODEL
RMS_EPS = 1e-6
NEG_INF = -1e30
N_EVEN = (DEPTH + 1) // 2
N_ODD = DEPTH // 2
A_WIDTH = A_HEADS * HEAD_DIM
B_WIDTH = B_HEADS * HEAD_DIM
B_KV_WIDTH = B_KV_HEADS * HEAD_DIM
IN_PROJ_WIDTH = 3 * A_WIDTH + B_WIDTH + 2 * B_KV_WIDTH
MIX_OUT_WIDTH = A_WIDTH + B_WIDTH

kernel_name = "chunk_causal_hybrid_bandattn_pool_block"


def rms_norm(x, g):
    xf = x.astype(jnp.float32)
    ms = jnp.mean(xf * xf, axis=-1, keepdims=True)
    return (xf * lax.rsqrt(ms + RMS_EPS)).astype(x.dtype) * g


def chunk_band(t, n_prev):
    nC = t.shape[0]
    tp = jnp.pad(t, ((n_prev, 0), (0, 0), (0, 0), (0, 0)))
    band = jnp.stack([tp[j:j + nC] for j in range(n_prev + 1)], axis=1)
    return band.reshape(nC, (n_prev + 1) * CHUNK, *t.shape[2:])


def band_valid(nC, n_prev):
    c = np.arange(nC)[:, None]
    kc = np.arange((n_prev + 1) * CHUNK)[None, :] // CHUNK
    return jnp.asarray(c - n_prev + kc >= 0)


def band_rel(n_prev):
    i = np.arange(CHUNK)[:, None]
    k = np.arange((n_prev + 1) * CHUNK)[None, :]
    return i - (k - n_prev * CHUNK)


def t5_bucket(rel_kq):
    nb = T5_BUCKETS // 2
    ret = (rel_kq > 0).astype(np.int32) * nb
    n = np.abs(rel_kq)
    max_exact = nb // 2
    large = max_exact + (np.log(np.maximum(n, 1) / max_exact)
                         / math.log(T5_MAX_DIST / max_exact) * (nb - max_exact)).astype(np.int32)
    large = np.minimum(large, nb - 1)
    return ret + np.where(n < max_exact, n, large)


def band_attention(q, kb, vb, bias, valid, sink):
    s = jnp.einsum('cqngd,cknd->ngcqk', q, kb).astype(jnp.float32) * (HEAD_DIM ** -0.5)
    s = s + bias[:, :, None].astype(jnp.float32)
    s = jnp.where(valid[:, None, :], s, NEG_INF)
    m = jnp.max(s, axis=-1, keepdims=True)
    if sink is not None:
        sk = sink.astype(jnp.float32)[:, :, None, None, None]
        m = jnp.maximum(m, sk)
    p = jnp.exp(s - m)
    denom = jnp.sum(p, axis=-1, keepdims=True)
    if sink is not None:
        denom = denom + jnp.exp(sk - m)
    p = (p / denom).astype(vb.dtype)
    return jnp.einsum('ngcqk,cknd->cqngd', p, vb)


def even_mixer(h, w_in, w_out, relpos_a, sink_b, t5_table):
    bsz, s_len, _ = h.shape
    nC = s_len // CHUNK
    proj = h @ w_in
    cuts = [A_WIDTH, 2 * A_WIDTH, 3 * A_WIDTH, 3 * A_WIDTH + B_WIDTH,
            3 * A_WIDTH + B_WIDTH + B_KV_WIDTH]
    qa, ka, va, qb, kb, vb = jnp.split(proj, cuts, axis=-1)
    qa = qa.reshape(bsz, nC, CHUNK, A_HEADS, 1, HEAD_DIM)
    ka = ka.reshape(bsz, nC, CHUNK, A_HEADS, HEAD_DIM)
    va = va.reshape(bsz, nC, CHUNK, A_HEADS, HEAD_DIM)
    qb = qb.reshape(bsz, nC, CHUNK, B_KV_HEADS, B_GROUP, HEAD_DIM)
    kb = kb.reshape(bsz, nC, CHUNK, B_KV_HEADS, HEAD_DIM)
    vb = vb.reshape(bsz, nC, CHUNK, B_KV_HEADS, HEAD_DIM)

    idx_a = np.clip(band_rel(A_PREV_CHUNKS), -A_MAX_REL, A_MAX_REL) + A_MAX_REL
    bias_a = jnp.transpose(relpos_a[idx_a], (2, 0, 1))[:, None]
    valid_a = band_valid(nC, A_PREV_CHUNKS)
    idx_b = t5_bucket(-band_rel(B_PREV_CHUNKS))
    bias_b = jnp.transpose(t5_table[idx_b], (2, 0, 1)).reshape(
        B_KV_HEADS, B_GROUP, CHUNK, (B_PREV_CHUNKS + 1) * CHUNK)
    valid_b = band_valid(nC, B_PREV_CHUNKS)

    def per_sample(args):
        qa_s, ka_s, va_s, qb_s, kb_s, vb_s = args
        oa = band_attention(qa_s, chunk_band(ka_s, A_PREV_CHUNKS), chunk_band(va_s, A_PREV_CHUNKS),
                            bias_a, valid_a, None)
        ob = band_attention(qb_s, chunk_band(kb_s, B_PREV_CHUNKS), chunk_band(vb_s, B_PREV_CHUNKS),
                            bias_b, valid_b, sink_b.reshape(B_KV_HEADS, B_GROUP))
        return (oa.reshape(s_len, A_WIDTH), ob.reshape(s_len, B_WIDTH))

    oa, ob = lax.map(per_sample, (qa, ka, va, qb, kb, vb))
    return jnp.concatenate([oa, ob], axis=-1) @ w_out


def pool_mixer(h, pool_w, pool_scale):
    bsz, s_len, _ = h.shape
    hf = h.astype(jnp.float32)
    cs = jnp.cumsum(hf, axis=1)
    t = jnp.arange(s_len)
    outs = []
    for g, w in enumerate(POOL_WINDOWS):
        sl = slice(g * POOL_CH, (g + 1) * POOL_CH)
        c = cs[..., sl]
        c_prev = jnp.pad(c, ((0, 0), (w, 0), (0, 0)))[:, :s_len]
        cnt = jnp.minimum(t + 1, w).astype(jnp.float32)[:, None]
        outs.append((c - c_prev) / cnt - hf[..., sl])
    d = jnp.stack(outs, axis=2).astype(h.dtype)
    y = jnp.einsum('bsgc,gce->bsge', d, pool_w).reshape(bsz, s_len, D_MODEL)
    return y * pool_scale


def sq_relu_mlp(h, w_up, w_down):
    u = jax.nn.relu(h @ w_up)
    return (u * u) @ w_down


def setup_inputs(seed: int = 0) -> dict:
    key = jax.random.key(seed)
    ks = jax.random.split(key, 20)
    f32 = jnp.float32

    def nrm(k, shape, scale):
        return jax.random.normal(k, shape, f32) * scale

    def gain(k, shape):
        return 1.0 + 0.05 * jax.random.normal(k, shape, f32)

    return {
        "x": nrm(ks[0], (BATCH, SEQ, D_MODEL), 1.0),
        "t5_table": nrm(ks[1], (T5_BUCKETS, B_HEADS), 0.5),
        "e_norm_pre": gain(ks[2], (N_EVEN, D_MODEL)),
        "e_norm_post": gain(ks[3], (N_EVEN, D_MODEL)),
        "e_w_in": nrm(ks[4], (N_EVEN, D_MODEL, IN_PROJ_WIDTH), D_MODEL ** -0.5),
        "e_w_out": nrm(ks[5], (N_EVEN, MIX_OUT_WIDTH, D_MODEL), MIX_OUT_WIDTH ** -0.5),
        "e_relpos_a": nrm(ks[6], (N_EVEN, 2 * A_MAX_REL + 1, A_HEADS), 0.5),
        "e_sink_b": nrm(ks[7], (N_EVEN, B_HEADS), 0.5),
        "o_norm_pre": gain(ks[8], (N_ODD, D_MODEL)),
        "o_norm_post": gain(ks[9], (N_ODD, D_MODEL)),
        "o_pool_w": nrm(ks[10], (N_ODD, POOL_GROUPS, POOL_CH, POOL_CH), POOL_CH ** -0.5),
        "o_pool_scale": gain(ks[11], (N_ODD, D_MODEL)),
        "mlp_norm_pre": gain(ks[12], (DEPTH, D_MODEL)),
        "mlp_norm_post": gain(ks[13], (DEPTH, D_MODEL)),
        "mlp_w_up": nrm(ks[14], (DEPTH, D_MODEL, D_FF), D_MODEL ** -0.5),
        "mlp_w_down": nrm(ks[15], (DEPTH, D_FF, D_MODEL), D_FF ** -0.5),
    }


def reference(x, t5_table, e_norm_pre, e_norm_post, e_w_in, e_w_out, e_relpos_a, e_sink_b,
              o_norm_pre, o_norm_post, o_pool_w, o_pool_scale,
              mlp_norm_pre, mlp_norm_post, mlp_w_up, mlp_w_down):
    h = x
    for layer in range(DEPTH):
        i = layer // 2
        if layer % 2 == 0:
            y = even_mixer(rms_norm(h, e_norm_pre[i]), e_w_in[i], e_w_out[i],
                           e_relpos_a[i], e_sink_b[i], t5_table)
            h = h + rms_norm(y, e_norm_post[i])
        else:
            y = pool_mixer(rms_norm(h, o_norm_pre[i]), o_pool_w[i], o_pool_scale[i])
            h = h + rms_norm(y, o_norm_post[i])
        y = sq_relu_mlp(rms_norm(h, mlp_norm_pre[layer]), mlp_w_up[layer], mlp_w_down[layer])
        h = h + rms_norm(y, mlp_norm_post[layer])
    return h
```

```python
import functools
import math

import jax
import jax.numpy as jnp
import numpy as np
from jax import lax
from jax.experimental import pallas as pl
from jax.experimental.pallas import tpu as pltpu

CHUNK = 64
HEAD_DIM = 64
A_PREV_CHUNKS = 8
A_MAX_REL = 128
B_GROUP = 4
B_PREV_CHUNKS = 2
T5_BUCKETS = 32
T5_MAX_DIST = 128
POOL_WINDOWS = (2, 4, 8, 16)
RMS_EPS = 1e-6
NEG_INF = -1e30

LANES = 128
V7X_VMEM_BYTES = 64 * 1024 * 1024

TOKEN_TILE = 512
ATTN_TILE = 512
PAIR = 2 * CHUNK
PAIRS_PER_TILE = ATTN_TILE // PAIR
A_BAND = (A_PREV_CHUNKS + 2) * CHUNK
B_BAND = (B_PREV_CHUNKS + 2) * CHUNK
POOL_HALO = 16
FF_CHUNK = 1024
VMEM_LIMIT = 56 * 1024 * 1024

BF16 = jnp.bfloat16
F32 = jnp.float32


def _rms(x, g):
    ms = jnp.mean(x * x, axis=-1, keepdims=True)
    return (x * lax.rsqrt(ms + RMS_EPS)) * g


def _dot(a, b):
    return jnp.dot(a, b, preferred_element_type=F32)


def _dot_nt(a, b):
    return lax.dot_general(a, b, (((1,), (1,)), ((), ())), preferred_element_type=F32)


def _resident(shape):
    nd = len(shape)
    return pl.BlockSpec(shape, lambda *_: (0,) * nd, pipeline_mode=pl.Buffered(1))


def _inproj_kernel(x_ref, g_ref, w_ref, o_ref, *, seg, q_segs, q_scale):
    xb = _rms(x_ref[...], g_ref[...]).astype(BF16)
    for s in range(w_ref.shape[1] // seg):
        y = _dot(xb, w_ref[:, s * seg:(s + 1) * seg])
        if s in q_segs:
            y = y * q_scale
        o_ref[:, s * seg:(s + 1) * seg] = y.astype(BF16)


def _inproj(x2, g, w_ext, q_segs, seg):
    n, d = x2.shape
    wcols = w_ext.shape[1]
    kern = functools.partial(_inproj_kernel, seg=seg, q_segs=q_segs,
                             q_scale=HEAD_DIM ** -0.5)
    return pl.pallas_call(
        kern,
        grid=(n // TOKEN_TILE,),
        in_specs=[
            pl.BlockSpec((TOKEN_TILE, d), lambda i: (i, 0)),
            _resident((1, d)),
            _resident((d, wcols)),
        ],
        out_specs=pl.BlockSpec((TOKEN_TILE, wcols), lambda i: (i, 0)),
        out_shape=jax.ShapeDtypeStruct((n, wcols), BF16),
        compiler_params=pltpu.CompilerParams(
            dimension_semantics=("arbitrary",), vmem_limit_bytes=VMEM_LIMIT),
        name="inproj",
    )(x2, g, w_ext)


def _softmax_pv(s, v, sink):
    m = jnp.max(s, axis=-1, keepdims=True)
    if sink is not None:
        m = jnp.maximum(m, sink)
    p = jnp.exp(s - m)
    l = jnp.sum(p, axis=-1, keepdims=True)
    if sink is not None:
        l = l + jnp.exp(sink - m)
    pb = p.astype(BF16)
    if isinstance(v, tuple):
        w1 = v[0].shape[0]
        o = _dot(pb[:, :w1], v[0]) + _dot(pb[:, w1:], v[1])
    else:
        o = _dot(pb, v)
    return o, 1.0 / l


def _attn_kernel(sink_ref, qa_ref, kap_ref, kac_ref, vap_ref, vac_ref,
                 qb_ref, kbp_ref, kbc_ref, vbp_ref, vbc_ref,
                 biasa_ref, biasb_ref, maska_ref, maskb_ref,
                 oa_ref, ob_ref, *, tiles_per_seq):
    i = pl.program_id(0)
    hp = pl.program_id(1)
    first = (i % tiles_per_seq) == 0
    lane = lax.broadcasted_iota(jnp.int32, (1, LANES), 1)
    lo_half = lane < HEAD_DIM
    zero = jnp.zeros((), BF16)

    for jj in range(PAIRS_PER_TILE):
        r0 = jj * PAIR
        q2 = qa_ref[r0:r0 + PAIR, :]
        kp = kap_ref[r0:, :]
        kc = kac_ref[:r0 + PAIR, :]
        vp = vap_ref[r0:, :]
        vc = vac_ref[:r0 + PAIR, :]
        mrow = maska_ref[jnp.where(first, jj, PAIRS_PER_TILE)]
        outs = []
        for e in range(2):
            qm = jnp.where(lo_half if e == 0 else ~lo_half, q2, zero)
            s = jnp.concatenate([_dot_nt(qm, kp), _dot_nt(qm, kc)], axis=1)
            s = s + biasa_ref[2 * hp + e] + mrow
            o, rl = _softmax_pv(s, (vp, vc), None)
            outs.append(o * rl)
        oa_ref[r0:r0 + PAIR, :] = jnp.where(lo_half, outs[0], outs[1]).astype(BF16)

        q2 = qb_ref[r0:r0 + PAIR, :]
        if jj == 0:
            k = jnp.concatenate([kbp_ref[...], kbc_ref[:PAIR, :]], axis=0)
            v = jnp.concatenate([vbp_ref[...], vbc_ref[:PAIR, :]], axis=0)
            mrow = maskb_ref[jnp.where(first, 0, 1)]
        else:
            k = kbc_ref[r0 - PAIR:r0 + PAIR, :]
            v = vbc_ref[r0 - PAIR:r0 + PAIR, :]
            mrow = None
        outs = []
        for e in range(2):
            h = 2 * hp + e
            qm = jnp.where(lo_half if e == 0 else ~lo_half, q2, zero)
            s = _dot_nt(qm, k) + biasb_ref[h]
            if mrow is not None:
                s = s + mrow
            o, rl = _softmax_pv(s, v, sink_ref[h])
            outs.append(o * rl)
        ob_ref[r0:r0 + PAIR, :] = jnp.where(lo_half, outs[0], outs[1]).astype(BF16)


def _attention(proj, sink, bias_a, bias_b, mask_a, mask_b, seq_len, n_pairs_a, n_pairs_b):
    n = proj.shape[0]
    tiles_per_seq = seq_len // ATTN_TILE
    qa0 = 0
    ka0 = qa0 + n_pairs_a
    va0 = ka0 + n_pairs_a
    qb0 = va0 + n_pairs_a
    kb0 = qb0 + n_pairs_b
    n_kv_b = n_pairs_b * 2 // B_GROUP
    vb0 = kb0 + n_kv_b
    pairs_per_kv = n_pairs_b // n_kv_b
    sub = ATTN_TILE // PAIR

    def cur(c0):
        return pl.BlockSpec((ATTN_TILE, LANES), lambda i, hp: (i, c0 + hp))

    def prev(c0):
        return pl.BlockSpec((ATTN_TILE, LANES),
                            lambda i, hp: (jnp.maximum(i - 1, 0), c0 + hp))

    def cur_kv(c0):
        return pl.BlockSpec((ATTN_TILE, LANES), lambda i, hp: (i, c0 + hp // pairs_per_kv))

    def prev_kv(c0):
        return pl.BlockSpec((PAIR, LANES),
                            lambda i, hp: (jnp.maximum(i * sub - 1, 0), c0 + hp // pairs_per_kv))

    kern = functools.partial(_attn_kernel, tiles_per_seq=tiles_per_seq)
    out_spec = pl.BlockSpec((ATTN_TILE, LANES), lambda i, hp: (i, hp))
    return pl.pallas_call(
        kern,
        grid=(n // ATTN_TILE, n_pairs_a),
        in_specs=[
            pl.BlockSpec(memory_space=pltpu.SMEM),
            cur(qa0), prev(ka0), cur(ka0), prev(va0), cur(va0),
            cur(qb0), prev_kv(kb0), cur_kv(kb0), prev_kv(vb0), cur_kv(vb0),
            _resident(bias_a.shape), _resident(bias_b.shape),
            _resident(mask_a.shape), _resident(mask_b.shape),
        ],
        out_specs=[out_spec, out_spec],
        out_shape=[jax.ShapeDtypeStruct((n, n_pairs_a * LANES), BF16),
                   jax.ShapeDtypeStruct((n, n_pairs_b * LANES), BF16)],
        compiler_params=pltpu.CompilerParams(
            dimension_semantics=("arbitrary", "arbitrary"), vmem_limit_bytes=VMEM_LIMIT),
        name="band_attention",
    )(sink, proj, proj, proj, proj, proj, proj, proj, proj, proj, proj,
      bias_a, bias_b, mask_a, mask_b)


def _mlp_tail(h1, gpre_ref, wup_ref, wdown_ref, gpost_ref):
    hn = _rms(h1, gpre_ref[...]).astype(BF16)
    acc = None
    for c in range(wup_ref.shape[1] // FF_CHUNK):
        u = _dot(hn, wup_ref[:, c * FF_CHUNK:(c + 1) * FF_CHUNK])
        u = jnp.maximum(u, 0.0)
        part = _dot((u * u).astype(BF16), wdown_ref[c * FF_CHUNK:(c + 1) * FF_CHUNK, :])
        acc = part if acc is None else acc + part
    return h1 + _rms(acc, gpost_ref[...])


def _outproj_mlp_kernel(oa_ref, ob_ref, x_ref, woa_ref, wob_ref, gpost_ref,
                        gpre_ref, wup_ref, wdown_ref, gmpost_ref, o_ref):
    y = _dot(oa_ref[...], woa_ref[...]) + _dot(ob_ref[...], wob_ref[...])
    h1 = x_ref[...] + _rms(y, gpost_ref[...])
    o_ref[...] = _mlp_tail(h1, gpre_ref, wup_ref, wdown_ref, gmpost_ref)


def _outproj_mlp(oa, ob, x2, w_oa, w_ob, g_post, g_mpre, w_up, w_down, g_mpost):
    n, d = x2.shape
    row = lambda w: pl.BlockSpec((TOKEN_TILE, w), lambda i: (i, 0))
    return pl.pallas_call(
        _outproj_mlp_kernel,
        grid=(n // TOKEN_TILE,),
        in_specs=[row(oa.shape[1]), row(ob.shape[1]), row(d),
                  _resident(w_oa.shape), _resident(w_ob.shape), _resident((1, d)),
                  _resident((1, d)), _resident(w_up.shape), _resident(w_down.shape),
                  _resident((1, d))],
        out_specs=row(d),
        out_shape=jax.ShapeDtypeStruct((n, d), F32),
        compiler_params=pltpu.CompilerParams(
            dimension_semantics=("arbitrary",), vmem_limit_bytes=VMEM_LIMIT),
        name="outproj_mlp",
    )(oa, ob, x2, w_oa, w_ob, g_post, g_mpre, w_up, w_down, g_mpost)


def _pool_mlp_kernel(h_ref, halo_ref, gpre_ref, pw_ref, pscale_ref, gpost_ref,
                     gmpre_ref, wup_ref, wdown_ref, gmpost_ref, o_ref, ext_ref,
                     *, tiles_per_seq):
    i = pl.program_id(0)
    t_in_seq = i % tiles_per_seq
    h = h_ref[...]
    g = gpre_ref[...]
    hn = _rms(h, g)
    keep = jnp.where(t_in_seq == 0, 0.0, 1.0)
    ext_ref[:POOL_HALO, :] = _rms(halo_ref[...], g) * keep
    ext_ref[POOL_HALO:, :] = hn
    tm = h.shape[0]
    pos = t_in_seq * tm + lax.broadcasted_iota(jnp.int32, (tm, 1), 0)
    pc = pw_ref.shape[1]
    ys = []
    for gi, w in enumerate(POOL_WINDOWS):
        cols = slice(gi * pc, (gi + 1) * pc)
        wsum = ext_ref[POOL_HALO:POOL_HALO + tm, cols]
        for j in range(1, w):
            wsum = wsum + ext_ref[POOL_HALO - j:POOL_HALO - j + tm, cols]
        cnt = jnp.minimum(pos + 1, w).astype(F32)
        d = wsum / cnt - hn[:, cols]
        ys.append(_dot(d.astype(BF16), pw_ref[gi]))
    y = jnp.concatenate(ys, axis=1) * pscale_ref[...]
    h1 = h + _rms(y, gpost_ref[...])
    o_ref[...] = _mlp_tail(h1, gmpre_ref, wup_ref, wdown_ref, gmpost_ref)


def _pool_mlp(h2, seq_len, g_pre, pool_w, pool_scale, g_post, g_mpre, w_up, w_down, g_mpost):
    n, d = h2.shape
    tiles_per_seq = seq_len // TOKEN_TILE
    halo_blocks = TOKEN_TILE // POOL_HALO
    row = pl.BlockSpec((TOKEN_TILE, d), lambda i: (i, 0))
    halo = pl.BlockSpec((POOL_HALO, d), lambda i: (jnp.maximum(i * halo_blocks - 1, 0), 0))
    kern = functools.partial(_pool_mlp_kernel, tiles_per_seq=tiles_per_seq)
    return pl.pallas_call(
        kern,
        grid=(n // TOKEN_TILE,),
        in_specs=[row, halo, _resident((1, d)), _resident(pool_w.shape), _resident((1, d)),
                  _resident((1, d)), _resident((1, d)), _resident(w_up.shape),
                  _resident(w_down.shape), _resident((1, d))],
        out_specs=row,
        out_shape=jax.ShapeDtypeStruct((n, d), F32),
        scratch_shapes=[pltpu.VMEM((POOL_HALO + TOKEN_TILE, d), F32)],
        compiler_params=pltpu.CompilerParams(
            dimension_semantics=("arbitrary",), vmem_limit_bytes=VMEM_LIMIT),
        name="pool_mlp",
    )(h2, h2, g_pre, pool_w, pool_scale, g_post, g_mpre, w_up, w_down, g_mpost)


def _pair_band_index(n_prev):
    band = (n_prev + 2) * CHUNK
    r = np.arange(PAIR)[:, None]
    k = np.arange(band)[None, :]
    e, qi = r // CHUNK, r % CHUNK
    kc = k // CHUNK - e
    ok = (kc >= 0) & (kc <= n_prev)
    key_pos = kc * CHUNK + k % CHUNK - n_prev * CHUNK
    return qi - key_pos, ok


def _t5_bucket(rel_kq):
    nb = T5_BUCKETS // 2
    ret = (rel_kq > 0).astype(np.int32) * nb
    n = np.abs(rel_kq)
    max_exact = nb // 2
    large = max_exact + (np.log(np.maximum(n, 1) / max_exact)
                         / math.log(T5_MAX_DIST / max_exact) * (nb - max_exact)).astype(np.int32)
    large = np.minimum(large, nb - 1)
    return ret + np.where(n < max_exact, n, large)


def _bias_tables(relpos_a, t5_table):
    rel_a, ok_a = _pair_band_index(A_PREV_CHUNKS)
    idx_a = np.clip(rel_a, -A_MAX_REL, A_MAX_REL) + A_MAX_REL
    bias_a = jnp.transpose(relpos_a[np.where(ok_a, idx_a, 0)], (2, 0, 1))
    bias_a = jnp.where(ok_a[None], bias_a, NEG_INF)
    rel_b, ok_b = _pair_band_index(B_PREV_CHUNKS)
    idx_b = _t5_bucket(-rel_b)
    bias_b = jnp.transpose(t5_table[np.where(ok_b, idx_b, 0)], (2, 0, 1))
    bias_b = jnp.where(ok_b[None], bias_b, NEG_INF)
    return bias_a.astype(F32), bias_b.astype(F32)


def _start_masks():
    kc_a = np.arange(A_BAND) // CHUNK
    mask_a = np.zeros((PAIRS_PER_TILE + 1, 1, A_BAND), np.float32)
    for jj in range(PAIRS_PER_TILE):
        mask_a[jj, 0, 2 * jj - A_PREV_CHUNKS + kc_a < 0] = NEG_INF
    kc_b = np.arange(B_BAND) // CHUNK
    mask_b = np.zeros((2, 1, B_BAND), np.float32)
    mask_b[0, 0, kc_b - B_PREV_CHUNKS < 0] = NEG_INF
    return jnp.asarray(mask_a), jnp.asarray(mask_b)


def _inproj_weight(w_in, a_width, b_width, b_kv_width):
    cuts = np.cumsum([a_width, a_width, a_width, b_width, b_kv_width])
    qa, ka, va, qb, kb, vb = jnp.split(w_in, cuts, axis=-1)

    def dup(w):
        heads = [w[:, j * HEAD_DIM:(j + 1) * HEAD_DIM] for j in range(w.shape[1] // HEAD_DIM)]
        return jnp.concatenate([x for hd in heads for x in (hd, hd)], axis=1)

    return jnp.concatenate([qa, ka, va, qb, dup(kb), dup(vb)], axis=1).astype(BF16)


def kernel(x, t5_table, e_norm_pre, e_norm_post, e_w_in, e_w_out, e_relpos_a, e_sink_b,
           o_norm_pre, o_norm_post, o_pool_w, o_pool_scale,
           mlp_norm_pre, mlp_norm_post, mlp_w_up, mlp_w_down):
    bsz, seq_len, d = x.shape
    depth = mlp_w_up.shape[0]
    a_heads = e_relpos_a.shape[2]
    b_heads = e_sink_b.shape[1]
    a_width, b_width = a_heads * HEAD_DIM, b_heads * HEAD_DIM
    b_kv_width = (b_heads // B_GROUP) * HEAD_DIM
    assert seq_len % ATTN_TILE == 0 and seq_len % TOKEN_TILE == 0
    assert a_heads == b_heads and a_heads % 2 == 0
    row = lambda v: v.reshape(1, -1).astype(F32)

    h = x.reshape(bsz * seq_len, d)
    mask_a, mask_b = _start_masks()
    for layer in range(depth):
        i = layer // 2
        w_up = mlp_w_up[layer].astype(BF16)
        w_down = mlp_w_down[layer].astype(BF16)
        g_mpre, g_mpost = row(mlp_norm_pre[layer]), row(mlp_norm_post[layer])
        if layer % 2 == 0:
            w_ext = _inproj_weight(e_w_in[i], a_width, b_width, b_kv_width)
            seg = 512
            q_segs = (0, (3 * a_width) // seg)
            assert a_width == seg and b_width == seg
            proj = _inproj(h, row(e_norm_pre[i]), w_ext, q_segs, seg)
            bias_a, bias_b = _bias_tables(e_relpos_a[i], t5_table)
            oa, ob = _attention(proj, e_sink_b[i].astype(F32), bias_a, bias_b, mask_a, mask_b,
                                seq_len, a_heads // 2, b_heads // 2)
            w_out = e_w_out[i].astype(BF16)
            h = _outproj_mlp(oa, ob, h, w_out[:a_width], w_out[a_width:], row(e_norm_post[i]),
                             g_mpre, w_up, w_down, g_mpost)
        else:
            h = _pool_mlp(h, seq_len, row(o_norm_pre[i]), o_pool_w[i].astype(BF16),
                          row(o_pool_scale[i]), row(o_norm_post[i]),
                          g_mpre, w_up, w_down, g_mpost)
    return h.reshape(bsz, seq_len, d)
```

```python
import functools
import math

import jax
import jax.numpy as jnp
import numpy as np
from jax import lax
from jax.experimental import pallas as pl
from jax.experimental.pallas import tpu as pltpu

CHUNK = 64
HEAD_DIM = 64
A_PREV_CHUNKS = 8
A_MAX_REL = 128
B_GROUP = 4
B_PREV_CHUNKS = 2
T5_BUCKETS = 32
T5_MAX_DIST = 128
POOL_WINDOWS = (2, 4, 8, 16)
RMS_EPS = 1e-6
NEG_INF = -1e30

LANES = 128
V7X_VMEM_BYTES = 64 * 1024 * 1024

TOKEN_TILE = 512
ATTN_TILE = 512
PAIR = 2 * CHUNK
PAIRS_PER_TILE = ATTN_TILE // PAIR
A_BAND = (A_PREV_CHUNKS + 2) * CHUNK
B_BAND = (B_PREV_CHUNKS + 2) * CHUNK
ATTN_LOOKAHEAD = 3
POOL_HALO = 16
FF_CHUNK = 1024
VMEM_LIMIT = 56 * 1024 * 1024

BF16 = jnp.bfloat16
F32 = jnp.float32


def _rms(x, g):
    ms = jnp.mean(x * x, axis=-1, keepdims=True)
    return (x * lax.rsqrt(ms + RMS_EPS)) * g


def _dot(a, b):
    return jnp.dot(a, b, preferred_element_type=F32)


def _dot_nt(a, b):
    return lax.dot_general(a, b, (((1,), (1,)), ((), ())), preferred_element_type=F32)


def _resident(shape):
    nd = len(shape)
    return pl.BlockSpec(shape, lambda *_: (0,) * nd, pipeline_mode=pl.Buffered(1))


def _inproj_kernel(x_ref, g_ref, w_ref, o_ref, *, seg, q_segs, q_scale):
    xb = _rms(x_ref[...], g_ref[...]).astype(BF16)
    for s in range(w_ref.shape[1] // seg):
        y = _dot(xb, w_ref[:, s * seg:(s + 1) * seg])
        if s in q_segs:
            y = y * q_scale
        o_ref[:, s * seg:(s + 1) * seg] = y.astype(BF16)


def _inproj(x2, g, w_ext, q_segs, seg):
    n, d = x2.shape
    wcols = w_ext.shape[1]
    kern = functools.partial(_inproj_kernel, seg=seg, q_segs=q_segs,
                             q_scale=HEAD_DIM ** -0.5)
    return pl.pallas_call(
        kern,
        grid=(n // TOKEN_TILE,),
        in_specs=[
            pl.BlockSpec((TOKEN_TILE, d), lambda i: (i, 0)),
            _resident((1, d)),
            _resident((d, wcols)),
        ],
        out_specs=pl.BlockSpec((TOKEN_TILE, wcols), lambda i: (i, 0)),
        out_shape=jax.ShapeDtypeStruct((n, wcols), BF16),
        compiler_params=pltpu.CompilerParams(
            dimension_semantics=("arbitrary",), vmem_limit_bytes=VMEM_LIMIT),
        name="inproj",
    )(x2, g, w_ext)


def _softmax_pv(s, v, sink):
    m = jnp.max(s, axis=-1, keepdims=True)
    if sink is not None:
        m = jnp.maximum(m, sink)
    p = jnp.exp(s - m)
    l = jnp.sum(p, axis=-1, keepdims=True)
    if sink is not None:
        l = l + jnp.exp(sink - m)
    pb = p.astype(BF16)
    if isinstance(v, tuple):
        w1 = v[0].shape[0]
        o = _dot(pb[:, :w1], v[0]) + _dot(pb[:, w1:], v[1])
    else:
        o = _dot(pb, v)
    return o, 1.0 / l


def _attn_kernel(sink_ref, qa_ref, kap_ref, kac_ref, vap_ref, vac_ref,
                 qb_ref, kbp_ref, kbc_ref, vbp_ref, vbc_ref,
                 biasa_ref, biasb_ref, maska_ref, maskb_ref,
                 oa_ref, ob_ref, *, tiles_per_seq):
    i = pl.program_id(0)
    hp = pl.program_id(1)
    first = (i % tiles_per_seq) == 0
    lane = lax.broadcasted_iota(jnp.int32, (1, LANES), 1)
    lo_half = lane < HEAD_DIM
    zero = jnp.zeros((), BF16)

    def head_q(q_ref, r0, e):
        return jnp.where(lo_half if e == 0 else ~lo_half, q_ref[r0:r0 + PAIR, :], zero)

    def logits_a(jj, e):
        r0 = jj * PAIR
        qm = head_q(qa_ref, r0, e)
        s = jnp.concatenate([_dot_nt(qm, kap_ref[r0:, :]),
                             _dot_nt(qm, kac_ref[:r0 + PAIR, :])], axis=1)
        return s + biasa_ref[2 * hp + e] + maska_ref[jnp.where(first, jj, PAIRS_PER_TILE)]

    def finish_a(jj, e, s):
        r0 = jj * PAIR
        o, rl = _softmax_pv(s, (vap_ref[r0:, :], vac_ref[:r0 + PAIR, :]), None)
        return o * rl

    def band_b(p_ref, c_ref, jj):
        r0 = jj * PAIR
        if jj == 0:
            return jnp.concatenate([p_ref[...], c_ref[:PAIR, :]], axis=0)
        return c_ref[r0 - PAIR:r0 + PAIR, :]

    def logits_b(jj, e):
        s = _dot_nt(head_q(qb_ref, jj * PAIR, e), band_b(kbp_ref, kbc_ref, jj))
        s = s + biasb_ref[2 * hp + e]
        if jj == 0:
            s = s + maskb_ref[jnp.where(first, 0, 1)]
        return s

    def finish_b(jj, e, s):
        o, rl = _softmax_pv(s, band_b(vbp_ref, vbc_ref, jj), sink_ref[2 * hp + e])
        return o * rl

    mixers = {"a": (logits_a, finish_a, oa_ref), "b": (logits_b, finish_b, ob_ref)}
    units = [(mx, jj, e) for jj in range(PAIRS_PER_TILE) for mx in ("a", "b") for e in range(2)]
    pending, done = {}, {}
    for t in range(len(units) + ATTN_LOOKAHEAD):
        if t < len(units):
            mx, jj, e = units[t]
            pending[t] = mixers[mx][0](jj, e)
        if t >= ATTN_LOOKAHEAD:
            mx, jj, e = units[t - ATTN_LOOKAHEAD]
            done[e] = mixers[mx][1](jj, e, pending.pop(t - ATTN_LOOKAHEAD))
            if e == 1:
                mixers[mx][2][jj * PAIR:(jj + 1) * PAIR, :] = (
                    jnp.where(lo_half, done[0], done[1]).astype(BF16))


def _attention(proj, sink, bias_a, bias_b, mask_a, mask_b, seq_len, n_pairs_a, n_pairs_b):
    n = proj.shape[0]
    tiles_per_seq = seq_len // ATTN_TILE
    qa0 = 0
    ka0 = qa0 + n_pairs_a
    va0 = ka0 + n_pairs_a
    qb0 = va0 + n_pairs_a
    kb0 = qb0 + n_pairs_b
    n_kv_b = n_pairs_b * 2 // B_GROUP
    vb0 = kb0 + n_kv_b
    pairs_per_kv = n_pairs_b // n_kv_b
    sub = ATTN_TILE // PAIR

    def cur(c0):
        return pl.BlockSpec((ATTN_TILE, LANES), lambda i, hp: (i, c0 + hp))

    def prev(c0):
        return pl.BlockSpec((ATTN_TILE, LANES),
                            lambda i, hp: (jnp.maximum(i - 1, 0), c0 + hp))

    def cur_kv(c0):
        return pl.BlockSpec((ATTN_TILE, LANES), lambda i, hp: (i, c0 + hp // pairs_per_kv))

    def prev_kv(c0):
        return pl.BlockSpec((PAIR, LANES),
                            lambda i, hp: (jnp.maximum(i * sub - 1, 0), c0 + hp // pairs_per_kv))

    kern = functools.partial(_attn_kernel, tiles_per_seq=tiles_per_seq)
    out_spec = pl.BlockSpec((ATTN_TILE, LANES), lambda i, hp: (i, hp))
    return pl.pallas_call(
        kern,
        grid=(n // ATTN_TILE, n_pairs_a),
        in_specs=[
            pl.BlockSpec(memory_space=pltpu.SMEM),
            cur(qa0), prev(ka0), cur(ka0), prev(va0), cur(va0),
            cur(qb0), prev_kv(kb0), cur_kv(kb0), prev_kv(vb0), cur_kv(vb0),
            _resident(bias_a.shape), _resident(bias_b.shape),
            _resident(mask_a.shape), _resident(mask_b.shape),
        ],
        out_specs=[out_spec, out_spec],
        out_shape=[jax.ShapeDtypeStruct((n, n_pairs_a * LANES), BF16),
                   jax.ShapeDtypeStruct((n, n_pairs_b * LANES), BF16)],
        compiler_params=pltpu.CompilerParams(
            dimension_semantics=("arbitrary", "arbitrary"), vmem_limit_bytes=VMEM_LIMIT),
        name="band_attention",
    )(sink, proj, proj, proj, proj, proj, proj, proj, proj, proj, proj,
      bias_a, bias_b, mask_a, mask_b)


def _mlp_tail(h1, gpre_ref, wup_ref, wdown_ref, gpost_ref):
    hn = _rms(h1, gpre_ref[...]).astype(BF16)
    acc = None
    for c in range(wup_ref.shape[1] // FF_CHUNK):
        u = _dot(hn, wup_ref[:, c * FF_CHUNK:(c + 1) * FF_CHUNK])
        u = jnp.maximum(u, 0.0)
        part = _dot((u * u).astype(BF16), wdown_ref[c * FF_CHUNK:(c + 1) * FF_CHUNK, :])
        acc = part if acc is None else acc + part
    return h1 + _rms(acc, gpost_ref[...])


def _outproj_mlp_kernel(oa_ref, ob_ref, x_ref, woa_ref, wob_ref, gpost_ref,
                        gpre_ref, wup_ref, wdown_ref, gmpost_ref, o_ref):
    y = _dot(oa_ref[...], woa_ref[...]) + _dot(ob_ref[...], wob_ref[...])
    h1 = x_ref[...] + _rms(y, gpost_ref[...])
    o_ref[...] = _mlp_tail(h1, gpre_ref, wup_ref, wdown_ref, gmpost_ref)


def _outproj_mlp(oa, ob, x2, w_oa, w_ob, g_post, g_mpre, w_up, w_down, g_mpost):
    n, d = x2.shape
    row = lambda w: pl.BlockSpec((TOKEN_TILE, w), lambda i: (i, 0))
    return pl.pallas_call(
        _outproj_mlp_kernel,
        grid=(n // TOKEN_TILE,),
        in_specs=[row(oa.shape[1]), row(ob.shape[1]), row(d),
                  _resident(w_oa.shape), _resident(w_ob.shape), _resident((1, d)),
                  _resident((1, d)), _resident(w_up.shape), _resident(w_down.shape),
                  _resident((1, d))],
        out_specs=row(d),
        out_shape=jax.ShapeDtypeStruct((n, d), F32),
        compiler_params=pltpu.CompilerParams(
            dimension_semantics=("arbitrary",), vmem_limit_bytes=VMEM_LIMIT),
        name="outproj_mlp",
    )(oa, ob, x2, w_oa, w_ob, g_post, g_mpre, w_up, w_down, g_mpost)


def _pool_mlp_kernel(h_ref, halo_ref, gpre_ref, pw_ref, pscale_ref, gpost_ref,
                     gmpre_ref, wup_ref, wdown_ref, gmpost_ref, o_ref, ext_ref,
                     *, tiles_per_seq):
    i = pl.program_id(0)
    t_in_seq = i % tiles_per_seq
    h = h_ref[...]
    g = gpre_ref[...]
    hn = _rms(h, g)
    keep = jnp.where(t_in_seq == 0, 0.0, 1.0)
    ext_ref[:POOL_HALO, :] = _rms(halo_ref[...], g) * keep
    ext_ref[POOL_HALO:, :] = hn
    tm = h.shape[0]
    pos = t_in_seq * tm + lax.broadcasted_iota(jnp.int32, (tm, 1), 0)
    pc = pw_ref.shape[1]
    ys = []
    for gi, w in enumerate(POOL_WINDOWS):
        cols = slice(gi * pc, (gi + 1) * pc)
        wsum = ext_ref[POOL_HALO:POOL_HALO + tm, cols]
        for j in range(1, w):
            wsum = wsum + ext_ref[POOL_HALO - j:POOL_HALO - j + tm, cols]
        cnt = jnp.minimum(pos + 1, w).astype(F32)
        d = wsum / cnt - hn[:, cols]
        ys.append(_dot(d.astype(BF16), pw_ref[gi]))
    y = jnp.concatenate(ys, axis=1) * pscale_ref[...]
    h1 = h + _rms(y, gpost_ref[...])
    o_ref[...] = _mlp_tail(h1, gmpre_ref, wup_ref, wdown_ref, gmpost_ref)


def _pool_mlp(h2, seq_len, g_pre, pool_w, pool_scale, g_post, g_mpre, w_up, w_down, g_mpost):
    n, d = h2.shape
    tiles_per_seq = seq_len // TOKEN_TILE
    halo_blocks = TOKEN_TILE // POOL_HALO
    row = pl.BlockSpec((TOKEN_TILE, d), lambda i: (i, 0))
    halo = pl.BlockSpec((POOL_HALO, d), lambda i: (jnp.maximum(i * halo_blocks - 1, 0), 0))
    kern = functools.partial(_pool_mlp_kernel, tiles_per_seq=tiles_per_seq)
    return pl.pallas_call(
        kern,
        grid=(n // TOKEN_TILE,),
        in_specs=[row, halo, _resident((1, d)), _resident(pool_w.shape), _resident((1, d)),
                  _resident((1, d)), _resident((1, d)), _resident(w_up.shape),
                  _resident(w_down.shape), _resident((1, d))],
        out_specs=row,
        out_shape=jax.ShapeDtypeStruct((n, d), F32),
        scratch_shapes=[pltpu.VMEM((POOL_HALO + TOKEN_TILE, d), F32)],
        compiler_params=pltpu.CompilerParams(
            dimension_semantics=("arbitrary",), vmem_limit_bytes=VMEM_LIMIT),
        name="pool_mlp",
    )(h2, h2, g_pre, pool_w, pool_scale, g_post, g_mpre, w_up, w_down, g_mpost)


def _pair_band_index(n_prev):
    band = (n_prev + 2) * CHUNK
    r = np.arange(PAIR)[:, None]
    k = np.arange(band)[None, :]
    e, qi = r // CHUNK, r % CHUNK
    kc = k // CHUNK - e
    ok = (kc >= 0) & (kc <= n_prev)
    key_pos = kc * CHUNK + k % CHUNK - n_prev * CHUNK
    return qi - key_pos, ok


def _t5_bucket(rel_kq):
    nb = T5_BUCKETS // 2
    ret = (rel_kq > 0).astype(np.int32) * nb
    n = np.abs(rel_kq)
    max_exact = nb // 2
    large = max_exact + (np.log(np.maximum(n, 1) / max_exact)
                         / math.log(T5_MAX_DIST / max_exact) * (nb - max_exact)).astype(np.int32)
    large = np.minimum(large, nb - 1)
    return ret + np.where(n < max_exact, n, large)


def _toeplitz_bias(table, index_of_rel, n_prev):
    band = (n_prev + 2) * CHUNK
    period = band + PAIR
    delta = np.arange(period)
    delta = np.where(delta < band, delta, delta - period)
    diag = jnp.take(table, index_of_rel(n_prev * CHUNK - delta), axis=0).T
    skew = jnp.tile(diag, (1, PAIR))[:, :PAIR * (period - 1)].reshape(-1, PAIR, period - 1)
    bias = skew[:, :, :band]
    _, ok = _pair_band_index(n_prev)
    return jnp.where(ok[None], bias, NEG_INF).astype(F32)


def _bias_tables(relpos_a, t5_table):
    bias_a = _toeplitz_bias(
        relpos_a, lambda rel: np.clip(rel, -A_MAX_REL, A_MAX_REL) + A_MAX_REL, A_PREV_CHUNKS)
    bias_b = _toeplitz_bias(t5_table, lambda rel: _t5_bucket(-rel), B_PREV_CHUNKS)
    return bias_a, bias_b


def _start_masks():
    kc_a = np.arange(A_BAND) // CHUNK
    mask_a = np.zeros((PAIRS_PER_TILE + 1, 1, A_BAND), np.float32)
    for jj in range(PAIRS_PER_TILE):
        mask_a[jj, 0, 2 * jj - A_PREV_CHUNKS + kc_a < 0] = NEG_INF
    kc_b = np.arange(B_BAND) // CHUNK
    mask_b = np.zeros((2, 1, B_BAND), np.float32)
    mask_b[0, 0, kc_b - B_PREV_CHUNKS < 0] = NEG_INF
    return jnp.asarray(mask_a), jnp.asarray(mask_b)


def _inproj_weight(w_in, a_width, b_width, b_kv_width):
    cuts = np.cumsum([a_width, a_width, a_width, b_width, b_kv_width])
    qa, ka, va, qb, kb, vb = jnp.split(w_in, cuts, axis=-1)

    def dup(w):
        heads = [w[:, j * HEAD_DIM:(j + 1) * HEAD_DIM] for j in range(w.shape[1] // HEAD_DIM)]
        return jnp.concatenate([x for hd in heads for x in (hd, hd)], axis=1)

    return jnp.concatenate([qa, ka, va, qb, dup(kb), dup(vb)], axis=1).astype(BF16)


def kernel(x, t5_table, e_norm_pre, e_norm_post, e_w_in, e_w_out, e_relpos_a, e_sink_b,
           o_norm_pre, o_norm_post, o_pool_w, o_pool_scale,
           mlp_norm_pre, mlp_norm_post, mlp_w_up, mlp_w_down):
    bsz, seq_len, d = x.shape
    depth = mlp_w_up.shape[0]
    a_heads = e_relpos_a.shape[2]
    b_heads = e_sink_b.shape[1]
    a_width, b_width = a_heads * HEAD_DIM, b_heads * HEAD_DIM
    b_kv_width = (b_heads // B_GROUP) * HEAD_DIM
    assert seq_len % ATTN_TILE == 0 and seq_len % TOKEN_TILE == 0
    assert a_heads == b_heads and a_heads % 2 == 0
    row = lambda v: v.reshape(1, -1).astype(F32)

    h = x.reshape(bsz * seq_len, d)
    mask_a, mask_b = _start_masks()
    for layer in range(depth):
        i = layer // 2
        w_up = mlp_w_up[layer].astype(BF16)
        w_down = mlp_w_down[layer].astype(BF16)
        g_mpre, g_mpost = row(mlp_norm_pre[layer]), row(mlp_norm_post[layer])
        if layer % 2 == 0:
            w_ext = _inproj_weight(e_w_in[i], a_width, b_width, b_kv_width)
            seg = 512
            q_segs = (0, (3 * a_width) // seg)
            assert a_width == seg and b_width == seg
            proj = _inproj(h, row(e_norm_pre[i]), w_ext, q_segs, seg)
            bias_a, bias_b = _bias_tables(e_relpos_a[i], t5_table)
            oa, ob = _attention(proj, e_sink_b[i].astype(F32), bias_a, bias_b, mask_a, mask_b,
                                seq_len, a_heads // 2, b_heads // 2)
            w_out = e_w_out[i].astype(BF16)
            h = _outproj_mlp(oa, ob, h, w_out[:a_width], w_out[a_width:], row(e_norm_post[i]),
                             g_mpre, w_up, w_down, g_mpost)
        else:
            h = _pool_mlp(h, seq_len, row(o_norm_pre[i]), o_pool_w[i].astype(BF16),
                          row(o_pool_scale[i]), row(o_norm_post[i]),
                          g_mpre, w_up, w_down, g_mpost)
    return h.reshape(bsz, seq_len, d)
```

```python
import functools
import math

import jax
import jax.numpy as jnp
import numpy as np
from jax import lax
from jax.experimental import pallas as pl
from jax.experimental.pallas import tpu as pltpu

CHUNK = 64
HEAD_DIM = 64
A_PREV_CHUNKS = 8
A_MAX_REL = 128
B_GROUP = 4
B_PREV_CHUNKS = 2
T5_BUCKETS = 32
T5_MAX_DIST = 128
POOL_WINDOWS = (2, 4, 8, 16)
RMS_EPS = 1e-6
NEG_INF = -1e30

LANES = 128
V7X_VMEM_BYTES = 64 * 1024 * 1024

TOKEN_TILE = 512
ATTN_TILE = 512
PAIR = 2 * CHUNK
PAIRS_PER_TILE = ATTN_TILE // PAIR
A_BAND = (A_PREV_CHUNKS + 2) * CHUNK
B_BAND = (B_PREV_CHUNKS + 2) * CHUNK
ATTN_LOOKAHEAD = 3
POOL_HALO = 16
FF_CHUNK = 1024
VMEM_LIMIT = 56 * 1024 * 1024

BF16 = jnp.bfloat16
F32 = jnp.float32


def _rms(x, g):
    ms = jnp.mean(x * x, axis=-1, keepdims=True)
    return (x * lax.rsqrt(ms + RMS_EPS)) * g


def _dot(a, b):
    return jnp.dot(a, b, preferred_element_type=F32)


def _dot_nt(a, b):
    return lax.dot_general(a, b, (((1,), (1,)), ((), ())), preferred_element_type=F32)


def _resident(shape):
    nd = len(shape)
    return pl.BlockSpec(shape, lambda *_: (0,) * nd, pipeline_mode=pl.Buffered(1))


def _inproj_kernel(x_ref, g_ref, w_ref, o_ref, *, seg, q_segs, q_scale):
    xb = _rms(x_ref[...], g_ref[...]).astype(BF16)
    for s in range(w_ref.shape[1] // seg):
        y = _dot(xb, w_ref[:, s * seg:(s + 1) * seg])
        if s in q_segs:
            y = y * q_scale
        o_ref[:, s * seg:(s + 1) * seg] = y.astype(BF16)


def _inproj(x2, g, w_ext, q_segs, seg):
    n, d = x2.shape
    wcols = w_ext.shape[1]
    kern = functools.partial(_inproj_kernel, seg=seg, q_segs=q_segs,
                             q_scale=HEAD_DIM ** -0.5)
    return pl.pallas_call(
        kern,
        grid=(n // TOKEN_TILE,),
        in_specs=[
            pl.BlockSpec((TOKEN_TILE, d), lambda i: (i, 0)),
            _resident((1, d)),
            _resident((d, wcols)),
        ],
        out_specs=pl.BlockSpec((TOKEN_TILE, wcols), lambda i: (i, 0)),
        out_shape=jax.ShapeDtypeStruct((n, wcols), BF16),
        compiler_params=pltpu.CompilerParams(
            dimension_semantics=("arbitrary",), vmem_limit_bytes=VMEM_LIMIT),
        name="inproj",
    )(x2, g, w_ext)


def _softmax_pv(s, v, sink):
    m = jnp.max(s, axis=-1, keepdims=True)
    if sink is not None:
        m = jnp.maximum(m, sink)
    p = jnp.exp(s - m)
    l = jnp.sum(p, axis=-1, keepdims=True)
    if sink is not None:
        l = l + jnp.exp(sink - m)
    pb = p.astype(BF16)
    if isinstance(v, tuple):
        w1 = v[0].shape[0]
        o = _dot(pb[:, :w1], v[0]) + _dot(pb[:, w1:], v[1])
    else:
        o = _dot(pb, v)
    return o, 1.0 / l


def _attn_kernel(sink_ref, qa_ref, kap_ref, kac_ref, vap_ref, vac_ref,
                 qb_ref, kbp_ref, kbc_ref, vbp_ref, vbc_ref,
                 biasa_ref, biasb_ref, maska_ref, maskb_ref,
                 oa_ref, ob_ref, *, tiles_per_seq):
    i = pl.program_id(0)
    hp = pl.program_id(1)
    first = (i % tiles_per_seq) == 0
    lane = lax.broadcasted_iota(jnp.int32, (1, LANES), 1)
    lo_half = lane < HEAD_DIM
    zero = jnp.zeros((), BF16)

    def head_q(q_ref, r0, e):
        return jnp.where(lo_half if e == 0 else ~lo_half, q_ref[r0:r0 + PAIR, :], zero)

    def logits_a(jj, e):
        r0 = jj * PAIR
        qm = head_q(qa_ref, r0, e)
        s = jnp.concatenate([_dot_nt(qm, kap_ref[r0:, :]),
                             _dot_nt(qm, kac_ref[:r0 + PAIR, :])], axis=1)
        return s + biasa_ref[2 * hp + e] + maska_ref[jnp.where(first, jj, PAIRS_PER_TILE)]

    def finish_a(jj, e, s):
        r0 = jj * PAIR
        o, rl = _softmax_pv(s, (vap_ref[r0:, :], vac_ref[:r0 + PAIR, :]), None)
        return o * rl

    def band_b(p_ref, c_ref, jj):
        r0 = jj * PAIR
        if jj == 0:
            return jnp.concatenate([p_ref[...], c_ref[:PAIR, :]], axis=0)
        return c_ref[r0 - PAIR:r0 + PAIR, :]

    def logits_b(jj, e):
        s = _dot_nt(head_q(qb_ref, jj * PAIR, e), band_b(kbp_ref, kbc_ref, jj))
        s = s + biasb_ref[2 * hp + e]
        if jj == 0:
            s = s + maskb_ref[jnp.where(first, 0, 1)]
        return s

    def finish_b(jj, e, s):
        o, rl = _softmax_pv(s, band_b(vbp_ref, vbc_ref, jj), sink_ref[2 * hp + e])
        return o * rl

    mixers = {"a": (logits_a, finish_a, oa_ref), "b": (logits_b, finish_b, ob_ref)}
    units = [(mx, jj, e) for jj in range(PAIRS_PER_TILE) for mx in ("a", "b") for e in range(2)]
    pending, done = {}, {}
    for t in range(len(units) + ATTN_LOOKAHEAD):
        if t < len(units):
            mx, jj, e = units[t]
            pending[t] = mixers[mx][0](jj, e)
        if t >= ATTN_LOOKAHEAD:
            mx, jj, e = units[t - ATTN_LOOKAHEAD]
            done[e] = mixers[mx][1](jj, e, pending.pop(t - ATTN_LOOKAHEAD))
            if e == 1:
                mixers[mx][2][jj * PAIR:(jj + 1) * PAIR, :] = (
                    jnp.where(lo_half, done[0], done[1]).astype(BF16))


def _attention(proj, sink, bias_a, bias_b, mask_a, mask_b, seq_len, n_pairs_a, n_pairs_b):
    n = proj.shape[0]
    tiles_per_seq = seq_len // ATTN_TILE
    qa0 = 0
    ka0 = qa0 + n_pairs_a
    va0 = ka0 + n_pairs_a
    qb0 = va0 + n_pairs_a
    kb0 = qb0 + n_pairs_b
    n_kv_b = n_pairs_b * 2 // B_GROUP
    vb0 = kb0 + n_kv_b
    pairs_per_kv = n_pairs_b // n_kv_b
    sub = ATTN_TILE // PAIR

    def cur(c0):
        return pl.BlockSpec((ATTN_TILE, LANES), lambda i, hp: (i, c0 + hp))

    def prev(c0):
        return pl.BlockSpec((ATTN_TILE, LANES),
                            lambda i, hp: (jnp.maximum(i - 1, 0), c0 + hp))

    def cur_kv(c0):
        return pl.BlockSpec((ATTN_TILE, LANES), lambda i, hp: (i, c0 + hp // pairs_per_kv))

    def prev_kv(c0):
        return pl.BlockSpec((PAIR, LANES),
                            lambda i, hp: (jnp.maximum(i * sub - 1, 0), c0 + hp // pairs_per_kv))

    kern = functools.partial(_attn_kernel, tiles_per_seq=tiles_per_seq)
    out_spec = pl.BlockSpec((ATTN_TILE, LANES), lambda i, hp: (i, hp))
    return pl.pallas_call(
        kern,
        grid=(n // ATTN_TILE, n_pairs_a),
        in_specs=[
            pl.BlockSpec(memory_space=pltpu.SMEM),
            cur(qa0), prev(ka0), cur(ka0), prev(va0), cur(va0),
            cur(qb0), prev_kv(kb0), cur_kv(kb0), prev_kv(vb0), cur_kv(vb0),
            _resident(bias_a.shape), _resident(bias_b.shape),
            _resident(mask_a.shape), _resident(mask_b.shape),
        ],
        out_specs=[out_spec, out_spec],
        out_shape=[jax.ShapeDtypeStruct((n, n_pairs_a * LANES), BF16),
                   jax.ShapeDtypeStruct((n, n_pairs_b * LANES), BF16)],
        compiler_params=pltpu.CompilerParams(
            dimension_semantics=("arbitrary", "arbitrary"), vmem_limit_bytes=VMEM_LIMIT),
        name="band_attention",
    )(sink, proj, proj, proj, proj, proj, proj, proj, proj, proj, proj,
      bias_a, bias_b, mask_a, mask_b)


def _mlp_tail(h1, gpre_ref, wup_ref, wdown_ref, gpost_ref, side_jobs=()):
    jobs = list(side_jobs)
    hn = _rms(h1, gpre_ref[...]).astype(BF16)
    acc = None
    for c in range(wup_ref.shape[1] // FF_CHUNK):
        u = _dot(hn, wup_ref[:, c * FF_CHUNK:(c + 1) * FF_CHUNK])
        if jobs:
            jobs.pop(0)()
        u = jnp.maximum(u, 0.0)
        part = _dot((u * u).astype(BF16), wdown_ref[c * FF_CHUNK:(c + 1) * FF_CHUNK, :])
        if jobs:
            jobs.pop(0)()
        acc = part if acc is None else acc + part
    for job in jobs:
        job()
    return h1 + _rms(acc, gpost_ref[...])


def _outproj_mlp_kernel(oa_ref, ob_ref, x_ref, woa_ref, wob_ref, gpost_ref,
                        gpre_ref, wup_ref, wdown_ref, gmpost_ref, o_ref):
    y = _dot(oa_ref[...], woa_ref[...]) + _dot(ob_ref[...], wob_ref[...])
    h1 = x_ref[...] + _rms(y, gpost_ref[...])
    o_ref[...] = _mlp_tail(h1, gpre_ref, wup_ref, wdown_ref, gmpost_ref)


def _outproj_mlp(oa, ob, x2, w_oa, w_ob, g_post, g_mpre, w_up, w_down, g_mpost):
    n, d = x2.shape
    row = lambda w: pl.BlockSpec((TOKEN_TILE, w), lambda i: (i, 0))
    return pl.pallas_call(
        _outproj_mlp_kernel,
        grid=(n // TOKEN_TILE,),
        in_specs=[row(oa.shape[1]), row(ob.shape[1]), row(d),
                  _resident(w_oa.shape), _resident(w_ob.shape), _resident((1, d)),
                  _resident((1, d)), _resident(w_up.shape), _resident(w_down.shape),
                  _resident((1, d))],
        out_specs=row(d),
        out_shape=jax.ShapeDtypeStruct((n, d), F32),
        compiler_params=pltpu.CompilerParams(
            dimension_semantics=("arbitrary",), vmem_limit_bytes=VMEM_LIMIT),
        name="outproj_mlp",
    )(oa, ob, x2, w_oa, w_ob, g_post, g_mpre, w_up, w_down, g_mpost)


def _pool_mixer_jobs(h_ref, halo_fn, g_ref, pw_ref, pscale_ref, t_in_seq, y_ref):
    tm = h_ref.shape[0]
    pc = pw_ref.shape[1]
    st = {}

    def prepare():
        st["hn"] = _rms(h_ref[...], g_ref[...])
        st["halo"] = halo_fn()
        pos1 = t_in_seq * tm + 1 + lax.broadcasted_iota(jnp.int32, (tm, 1), 0)
        st["pos1"] = pos1
        st["inv_pos1"] = 1.0 / pos1.astype(F32)

    def group(gi, w):
        cols = slice(gi * pc, (gi + 1) * pc)
        hn = st["hn"][:, cols]
        wsum = jnp.concatenate([st["halo"][:, cols], hn], axis=0)
        k = 1
        while k < w:
            wsum = wsum + pltpu.roll(wsum, k, axis=0)
            k *= 2
        inv_cnt = jnp.where(st["pos1"] < w, st["inv_pos1"], 1.0 / w)
        d = (wsum[POOL_HALO:, :] * inv_cnt - hn).astype(BF16)
        y_ref[:, cols] = _dot(d, pw_ref[gi]) * pscale_ref[:, cols]

    return [prepare] + [functools.partial(group, gi, w) for gi, w in enumerate(POOL_WINDOWS)]


def _pool_mlp_kernel(h_ref, hnext_ref, gpre_ref, pw_ref, pscale_ref, gpost_ref,
                     gmpre_ref, wup_ref, wdown_ref, gmpost_ref, o_ref, y_ref, *, tiles_per_seq):
    i = pl.program_id(0)
    tm, dm = h_ref.shape

    @pl.when(i == 0)
    def _():
        for job in _pool_mixer_jobs(h_ref, lambda: jnp.zeros((POOL_HALO, dm), F32),
                                    gpre_ref, pw_ref, pscale_ref, 0, y_ref):
            job()

    h1 = h_ref[...] + _rms(y_ref[...], gpost_ref[...])

    t_next = (i + 1) % tiles_per_seq

    def next_halo():
        keep = jnp.where(t_next == 0, 0.0, 1.0)
        return _rms(h_ref[tm - POOL_HALO:, :], gpre_ref[...]) * keep

    jobs = _pool_mixer_jobs(hnext_ref, next_halo, gpre_ref, pw_ref, pscale_ref, t_next, y_ref)
    o_ref[...] = _mlp_tail(h1, gmpre_ref, wup_ref, wdown_ref, gmpost_ref, jobs)


def _pool_mlp(h2, seq_len, g_pre, pool_w, pool_scale, g_post, g_mpre, w_up, w_down, g_mpost):
    n, d = h2.shape
    tiles_per_seq = seq_len // TOKEN_TILE
    n_tiles = n // TOKEN_TILE
    row = pl.BlockSpec((TOKEN_TILE, d), lambda i: (i, 0))
    nxt = pl.BlockSpec((TOKEN_TILE, d), lambda i: (jnp.minimum(i + 1, n_tiles - 1), 0))
    kern = functools.partial(_pool_mlp_kernel, tiles_per_seq=tiles_per_seq)
    return pl.pallas_call(
        kern,
        grid=(n_tiles,),
        in_specs=[row, nxt, _resident((1, d)), _resident(pool_w.shape), _resident((1, d)),
                  _resident((1, d)), _resident((1, d)), _resident(w_up.shape),
                  _resident(w_down.shape), _resident((1, d))],
        out_specs=row,
        out_shape=jax.ShapeDtypeStruct((n, d), F32),
        scratch_shapes=[pltpu.VMEM((TOKEN_TILE, d), F32)],
        compiler_params=pltpu.CompilerParams(
            dimension_semantics=("arbitrary",), vmem_limit_bytes=VMEM_LIMIT),
        name="pool_mlp",
    )(h2, h2, g_pre, pool_w, pool_scale, g_post, g_mpre, w_up, w_down, g_mpost)


def _pair_band_index(n_prev):
    band = (n_prev + 2) * CHUNK
    r = np.arange(PAIR)[:, None]
    k = np.arange(band)[None, :]
    e, qi = r // CHUNK, r % CHUNK
    kc = k // CHUNK - e
    ok = (kc >= 0) & (kc <= n_prev)
    key_pos = kc * CHUNK + k % CHUNK - n_prev * CHUNK
    return qi - key_pos, ok


def _t5_bucket(rel_kq):
    nb = T5_BUCKETS // 2
    ret = (rel_kq > 0).astype(np.int32) * nb
    n = np.abs(rel_kq)
    max_exact = nb // 2
    large = max_exact + (np.log(np.maximum(n, 1) / max_exact)
                         / math.log(T5_MAX_DIST / max_exact) * (nb - max_exact)).astype(np.int32)
    large = np.minimum(large, nb - 1)
    return ret + np.where(n < max_exact, n, large)


def _toeplitz_bias(table, index_of_rel, n_prev):
    band = (n_prev + 2) * CHUNK
    period = band + PAIR
    delta = np.arange(period)
    delta = np.where(delta < band, delta, delta - period)
    diag = jnp.take(table, index_of_rel(n_prev * CHUNK - delta), axis=0).T
    skew = jnp.tile(diag, (1, PAIR))[:, :PAIR * (period - 1)].reshape(-1, PAIR, period - 1)
    bias = skew[:, :, :band]
    _, ok = _pair_band_index(n_prev)
    return jnp.where(ok[None], bias, NEG_INF).astype(F32)


def _bias_tables(relpos_a, t5_table):
    bias_a = _toeplitz_bias(
        relpos_a, lambda rel: np.clip(rel, -A_MAX_REL, A_MAX_REL) + A_MAX_REL, A_PREV_CHUNKS)
    bias_b = _toeplitz_bias(t5_table, lambda rel: _t5_bucket(-rel), B_PREV_CHUNKS)
    return bias_a, bias_b


def _start_masks():
    kc_a = np.arange(A_BAND) // CHUNK
    mask_a = np.zeros((PAIRS_PER_TILE + 1, 1, A_BAND), np.float32)
    for jj in range(PAIRS_PER_TILE):
        mask_a[jj, 0, 2 * jj - A_PREV_CHUNKS + kc_a < 0] = NEG_INF
    kc_b = np.arange(B_BAND) // CHUNK
    mask_b = np.zeros((2, 1, B_BAND), np.float32)
    mask_b[0, 0, kc_b - B_PREV_CHUNKS < 0] = NEG_INF
    return jnp.asarray(mask_a), jnp.asarray(mask_b)


def _inproj_weight(w_in, a_width, b_width, b_kv_width):
    cuts = np.cumsum([a_width, a_width, a_width, b_width, b_kv_width])
    qa, ka, va, qb, kb, vb = jnp.split(w_in, cuts, axis=-1)

    def dup(w):
        heads = [w[:, j * HEAD_DIM:(j + 1) * HEAD_DIM] for j in range(w.shape[1] // HEAD_DIM)]
        return jnp.concatenate([x for hd in heads for x in (hd, hd)], axis=1)

    return jnp.concatenate([qa, ka, va, qb, dup(kb), dup(vb)], axis=1).astype(BF16)


def kernel(x, t5_table, e_norm_pre, e_norm_post, e_w_in, e_w_out, e_relpos_a, e_sink_b,
           o_norm_pre, o_norm_post, o_pool_w, o_pool_scale,
           mlp_norm_pre, mlp_norm_post, mlp_w_up, mlp_w_down):
    bsz, seq_len, d = x.shape
    depth = mlp_w_up.shape[0]
    a_heads = e_relpos_a.shape[2]
    b_heads = e_sink_b.shape[1]
    a_width, b_width = a_heads * HEAD_DIM, b_heads * HEAD_DIM
    b_kv_width = (b_heads // B_GROUP) * HEAD_DIM
    assert seq_len % ATTN_TILE == 0 and seq_len % TOKEN_TILE == 0
    assert a_heads == b_heads and a_heads % 2 == 0
    row = lambda v: v.reshape(1, -1).astype(F32)

    h = x.reshape(bsz * seq_len, d)
    mask_a, mask_b = _start_masks()
    for layer in range(depth):
        i = layer // 2
        w_up = mlp_w_up[layer].astype(BF16)
        w_down = mlp_w_down[layer].astype(BF16)
        g_mpre, g_mpost = row(mlp_norm_pre[layer]), row(mlp_norm_post[layer])
        if layer % 2 == 0:
            w_ext = _inproj_weight(e_w_in[i], a_width, b_width, b_kv_width)
            seg = 512
            q_segs = (0, (3 * a_width) // seg)
            assert a_width == seg and b_width == seg
            proj = _inproj(h, row(e_norm_pre[i]), w_ext, q_segs, seg)
            bias_a, bias_b = _bias_tables(e_relpos_a[i], t5_table)
            oa, ob = _attention(proj, e_sink_b[i].astype(F32), bias_a, bias_b, mask_a, mask_b,
                                seq_len, a_heads // 2, b_heads // 2)
            w_out = e_w_out[i].astype(BF16)
            h = _outproj_mlp(oa, ob, h, w_out[:a_width], w_out[a_width:], row(e_norm_post[i]),
                             g_mpre, w_up, w_down, g_mpost)
        else:
            h = _pool_mlp(h, seq_len, row(o_norm_pre[i]), o_pool_w[i].astype(BF16),
                          row(o_pool_scale[i]), row(o_norm_post[i]),
                          g_mpre, w_up, w_down, g_mpost)
    return h.reshape(bsz, seq_len, d)
```

```python
import functools
import math

import jax
import jax.numpy as jnp
import numpy as np
from jax import lax
from jax.experimental import pallas as pl
from jax.experimental.pallas import tpu as pltpu

CHUNK = 64
HEAD_DIM = 64
A_PREV_CHUNKS = 8
A_MAX_REL = 128
B_GROUP = 4
B_PREV_CHUNKS = 2
T5_BUCKETS = 32
T5_MAX_DIST = 128
POOL_WINDOWS = (2, 4, 8, 16)
RMS_EPS = 1e-6
NEG_INF = -1e30
LOG2E = math.log2(math.e)

LANES = 128
V7X_VMEM_BYTES = 64 * 1024 * 1024

TOKEN_TILE = 512
ATTN_TILE = 512
PAIR = 2 * CHUNK
PAIRS_PER_TILE = ATTN_TILE // PAIR
A_BAND = (A_PREV_CHUNKS + 2) * CHUNK
B_BAND = (B_PREV_CHUNKS + 2) * CHUNK
ATTN_LOOKAHEAD = 3
POOL_HALO = 16
FF_CHUNK = 1024
VMEM_LIMIT = 56 * 1024 * 1024

BF16 = jnp.bfloat16
F32 = jnp.float32


def _rms(x, g):
    ms = jnp.mean(x * x, axis=-1, keepdims=True)
    return (x * lax.rsqrt(ms + RMS_EPS)) * g


def _dot(a, b):
    return jnp.dot(a, b, preferred_element_type=F32)


def _dot_nt(a, b):
    return lax.dot_general(a, b, (((1,), (1,)), ((), ())), preferred_element_type=F32)


def _resident(shape):
    nd = len(shape)
    return pl.BlockSpec(shape, lambda *_: (0,) * nd, pipeline_mode=pl.Buffered(1))


def _inproj_kernel(x_ref, g_ref, w_ref, o_ref, *, seg, q_segs, q_scale):
    xb = _rms(x_ref[...], g_ref[...]).astype(BF16)
    for s in range(w_ref.shape[1] // seg):
        y = _dot(xb, w_ref[:, s * seg:(s + 1) * seg])
        if s in q_segs:
            y = y * q_scale
        o_ref[:, s * seg:(s + 1) * seg] = y.astype(BF16)


def _inproj(x2, g, w_ext, q_segs, seg):
    n, d = x2.shape
    wcols = w_ext.shape[1]
    kern = functools.partial(_inproj_kernel, seg=seg, q_segs=q_segs,
                             q_scale=HEAD_DIM ** -0.5 * LOG2E)
    return pl.pallas_call(
        kern,
        grid=(n // TOKEN_TILE,),
        in_specs=[
            pl.BlockSpec((TOKEN_TILE, d), lambda i: (i, 0)),
            _resident((1, d)),
            _resident((d, wcols)),
        ],
        out_specs=pl.BlockSpec((TOKEN_TILE, wcols), lambda i: (i, 0)),
        out_shape=jax.ShapeDtypeStruct((n, wcols), BF16),
        compiler_params=pltpu.CompilerParams(
            dimension_semantics=("arbitrary",), vmem_limit_bytes=VMEM_LIMIT),
        name="inproj",
    )(x2, g, w_ext)


def _softmax_pv(s, v, sink):
    m = jnp.max(s, axis=-1, keepdims=True)
    if sink is not None:
        m = jnp.maximum(m, sink)
    p = jnp.exp2(s - m)
    l = jnp.sum(p, axis=-1, keepdims=True)
    if sink is not None:
        l = l + jnp.exp2(sink - m)
    pb = p.astype(BF16)
    if isinstance(v, tuple):
        w1 = v[0].shape[0]
        o = _dot(pb[:, :w1], v[0]) + _dot(pb[:, w1:], v[1])
    else:
        o = _dot(pb, v)
    return o, 1.0 / l


def _attn_kernel(sink_ref, qa_ref, kap_ref, kac_ref, vap_ref, vac_ref,
                 qb_ref, kbp_ref, kbc_ref, vbp_ref, vbc_ref, biasa_ref, biasb_ref,
                 oa_ref, ob_ref, *, tiles_per_seq, pairs_per_kv):
    i = pl.program_id(0)
    first = (i % tiles_per_seq) == 0
    lane = lax.broadcasted_iota(jnp.int32, (1, LANES), 1)
    lo_half = lane < HEAD_DIM
    zero = jnp.zeros((), BF16)
    n_head_pairs = qa_ref.shape[1] // LANES

    def lanes(hp):
        return slice(hp * LANES, (hp + 1) * LANES)

    def head_q(q_ref, r0, hp, e):
        q2 = q_ref[r0:r0 + PAIR, lanes(hp)]
        return jnp.where(lo_half if e == 0 else ~lo_half, q2, zero)

    def logits_a(jj, hp, e):
        r0 = jj * PAIR
        qm = head_q(qa_ref, r0, hp, e)
        s = jnp.concatenate([_dot_nt(qm, kap_ref[r0:, lanes(hp)]),
                             _dot_nt(qm, kac_ref[:r0 + PAIR, lanes(hp)])], axis=1)
        return s + biasa_ref[jnp.where(first, jj, PAIRS_PER_TILE), 2 * hp + e]

    def finish_a(jj, hp, e, s):
        r0 = jj * PAIR
        o, rl = _softmax_pv(s, (vap_ref[r0:, lanes(hp)], vac_ref[:r0 + PAIR, lanes(hp)]), None)
        return o * rl

    def band_b(p_ref, c_ref, jj, hp):
        r0 = jj * PAIR
        kv = lanes(hp // pairs_per_kv)
        if jj == 0:
            return jnp.concatenate([p_ref[:, kv], c_ref[:PAIR, kv]], axis=0)
        return c_ref[r0 - PAIR:r0 + PAIR, kv]

    def logits_b(jj, hp, e):
        s = _dot_nt(head_q(qb_ref, jj * PAIR, hp, e), band_b(kbp_ref, kbc_ref, jj, hp))
        variant = jnp.where(first, 0, 1) if jj == 0 else 1
        return s + biasb_ref[variant, 2 * hp + e]

    def finish_b(jj, hp, e, s):
        o, rl = _softmax_pv(s, band_b(vbp_ref, vbc_ref, jj, hp), sink_ref[2 * hp + e])
        return o * rl

    mixers = {"a": (logits_a, finish_a, oa_ref), "b": (logits_b, finish_b, ob_ref)}
    units = [(mx, jj, hp, e) for hp in range(n_head_pairs) for jj in range(PAIRS_PER_TILE)
             for mx in ("a", "b") for e in range(2)]
    pending, done = {}, {}
    for t in range(len(units) + ATTN_LOOKAHEAD):
        if t < len(units):
            mx, jj, hp, e = units[t]
            pending[t] = mixers[mx][0](jj, hp, e)
        if t >= ATTN_LOOKAHEAD:
            mx, jj, hp, e = units[t - ATTN_LOOKAHEAD]
            done[e] = mixers[mx][1](jj, hp, e, pending.pop(t - ATTN_LOOKAHEAD))
            if e == 1:
                mixers[mx][2][jj * PAIR:(jj + 1) * PAIR, lanes(hp)] = (
                    jnp.where(lo_half, done[0], done[1]).astype(BF16))


def _attention(proj, sink, bias_a, bias_b, seq_len, a_width, b_width, b_kv_width):
    n = proj.shape[0]
    tiles_per_seq = seq_len // ATTN_TILE
    kv_dup = 2 * b_kv_width
    assert a_width == b_width and kv_dup % LANES == 0 and a_width % kv_dup == 0
    sub = ATTN_TILE // PAIR
    qa, ka, va, qb = 0, 1, 2, 3
    kb = 4 * a_width // kv_dup
    vb = kb + 1

    def cur(width, c):
        return pl.BlockSpec((ATTN_TILE, width), lambda i: (i, c))

    def prev(c):
        return pl.BlockSpec((ATTN_TILE, a_width), lambda i: (jnp.maximum(i - 1, 0), c))

    def prev_pair(c):
        return pl.BlockSpec((PAIR, kv_dup), lambda i: (jnp.maximum(i * sub - 1, 0), c))

    kern = functools.partial(_attn_kernel, tiles_per_seq=tiles_per_seq,
                             pairs_per_kv=(b_width // LANES) // (kv_dup // LANES))
    return pl.pallas_call(
        kern,
        grid=(n // ATTN_TILE,),
        in_specs=[
            pl.BlockSpec(memory_space=pltpu.SMEM),
            cur(a_width, qa), prev(ka), cur(a_width, ka), prev(va), cur(a_width, va),
            cur(b_width, qb), prev_pair(kb), cur(kv_dup, kb), prev_pair(vb), cur(kv_dup, vb),
            _resident(bias_a.shape), _resident(bias_b.shape),
        ],
        out_specs=[cur(a_width, 0), cur(b_width, 0)],
        out_shape=[jax.ShapeDtypeStruct((n, a_width), BF16),
                   jax.ShapeDtypeStruct((n, b_width), BF16)],
        compiler_params=pltpu.CompilerParams(
            dimension_semantics=("arbitrary",), vmem_limit_bytes=VMEM_LIMIT),
        name="band_attention",
    )(sink, proj, proj, proj, proj, proj, proj, proj, proj, proj, proj, bias_a, bias_b)


def _mlp_tail(h1, gpre_ref, wup_ref, wdown_ref, gpost_ref, side_jobs=()):
    jobs = list(side_jobs)
    hn = _rms(h1, gpre_ref[...]).astype(BF16)
    acc = None
    for c in range(wup_ref.shape[1] // FF_CHUNK):
        u = _dot(hn, wup_ref[:, c * FF_CHUNK:(c + 1) * FF_CHUNK])
        if jobs:
            jobs.pop(0)()
        u = jnp.maximum(u, 0.0)
        part = _dot((u * u).astype(BF16), wdown_ref[c * FF_CHUNK:(c + 1) * FF_CHUNK, :])
        if jobs:
            jobs.pop(0)()
        acc = part if acc is None else acc + part
    for job in jobs:
        job()
    return h1 + _rms(acc, gpost_ref[...])


def _outproj_mlp_kernel(oa_ref, ob_ref, x_ref, woa_ref, wob_ref, gpost_ref,
                        gpre_ref, wup_ref, wdown_ref, gmpost_ref, o_ref):
    y = _dot(oa_ref[...], woa_ref[...]) + _dot(ob_ref[...], wob_ref[...])
    h1 = x_ref[...] + _rms(y, gpost_ref[...])
    o_ref[...] = _mlp_tail(h1, gpre_ref, wup_ref, wdown_ref, gmpost_ref)


def _outproj_mlp(oa, ob, x2, w_oa, w_ob, g_post, g_mpre, w_up, w_down, g_mpost):
    n, d = x2.shape
    row = lambda w: pl.BlockSpec((TOKEN_TILE, w), lambda i: (i, 0))
    return pl.pallas_call(
        _outproj_mlp_kernel,
        grid=(n // TOKEN_TILE,),
        in_specs=[row(oa.shape[1]), row(ob.shape[1]), row(d),
                  _resident(w_oa.shape), _resident(w_ob.shape), _resident((1, d)),
                  _resident((1, d)), _resident(w_up.shape), _resident(w_down.shape),
                  _resident((1, d))],
        out_specs=row(d),
        out_shape=jax.ShapeDtypeStruct((n, d), F32),
        compiler_params=pltpu.CompilerParams(
            dimension_semantics=("arbitrary",), vmem_limit_bytes=VMEM_LIMIT),
        name="outproj_mlp",
    )(oa, ob, x2, w_oa, w_ob, g_post, g_mpre, w_up, w_down, g_mpost)


def _pool_mixer_jobs(h_ref, halo_fn, g_ref, pw_ref, pscale_ref, t_in_seq, y_ref):
    tm = h_ref.shape[0]
    pc = pw_ref.shape[1]
    st = {}

    def prepare():
        st["hn"] = _rms(h_ref[...], g_ref[...])
        st["halo"] = halo_fn()
        pos1 = t_in_seq * tm + 1 + lax.broadcasted_iota(jnp.int32, (tm, 1), 0)
        st["pos1"] = pos1
        st["inv_pos1"] = 1.0 / pos1.astype(F32)

    def group(gi, w):
        cols = slice(gi * pc, (gi + 1) * pc)
        hn = st["hn"][:, cols]
        wsum = jnp.concatenate([st["halo"][:, cols], hn], axis=0)
        k = 1
        while k < w:
            wsum = wsum + pltpu.roll(wsum, k, axis=0)
            k *= 2
        inv_cnt = jnp.where(st["pos1"] < w, st["inv_pos1"], 1.0 / w)
        d = (wsum[POOL_HALO:, :] * inv_cnt - hn).astype(BF16)
        y_ref[:, cols] = _dot(d, pw_ref[gi]) * pscale_ref[:, cols]

    return [prepare] + [functools.partial(group, gi, w) for gi, w in enumerate(POOL_WINDOWS)]


def _pool_mlp_kernel(h_ref, hnext_ref, gpre_ref, pw_ref, pscale_ref, gpost_ref,
                     gmpre_ref, wup_ref, wdown_ref, gmpost_ref, o_ref, y_ref, *, tiles_per_seq):
    i = pl.program_id(0)
    tm, dm = h_ref.shape

    @pl.when(i == 0)
    def _():
        for job in _pool_mixer_jobs(h_ref, lambda: jnp.zeros((POOL_HALO, dm), F32),
                                    gpre_ref, pw_ref, pscale_ref, 0, y_ref):
            job()

    h1 = h_ref[...] + _rms(y_ref[...], gpost_ref[...])

    t_next = (i + 1) % tiles_per_seq

    def next_halo():
        keep = jnp.where(t_next == 0, 0.0, 1.0)
        return _rms(h_ref[tm - POOL_HALO:, :], gpre_ref[...]) * keep

    jobs = _pool_mixer_jobs(hnext_ref, next_halo, gpre_ref, pw_ref, pscale_ref, t_next, y_ref)
    o_ref[...] = _mlp_tail(h1, gmpre_ref, wup_ref, wdown_ref, gmpost_ref, jobs)


def _pool_mlp(h2, seq_len, g_pre, pool_w, pool_scale, g_post, g_mpre, w_up, w_down, g_mpost):
    n, d = h2.shape
    tiles_per_seq = seq_len // TOKEN_TILE
    n_tiles = n // TOKEN_TILE
    row = pl.BlockSpec((TOKEN_TILE, d), lambda i: (i, 0))
    nxt = pl.BlockSpec((TOKEN_TILE, d), lambda i: (jnp.minimum(i + 1, n_tiles - 1), 0))
    kern = functools.partial(_pool_mlp_kernel, tiles_per_seq=tiles_per_seq)
    return pl.pallas_call(
        kern,
        grid=(n_tiles,),
        in_specs=[row, nxt, _resident((1, d)), _resident(pool_w.shape), _resident((1, d)),
                  _resident((1, d)), _resident((1, d)), _resident(w_up.shape),
                  _resident(w_down.shape), _resident((1, d))],
        out_specs=row,
        out_shape=jax.ShapeDtypeStruct((n, d), F32),
        scratch_shapes=[pltpu.VMEM((TOKEN_TILE, d), F32)],
        compiler_params=pltpu.CompilerParams(
            dimension_semantics=("arbitrary",), vmem_limit_bytes=VMEM_LIMIT),
        name="pool_mlp",
    )(h2, h2, g_pre, pool_w, pool_scale, g_post, g_mpre, w_up, w_down, g_mpost)


def _pair_band_index(n_prev):
    band = (n_prev + 2) * CHUNK
    r = np.arange(PAIR)[:, None]
    k = np.arange(band)[None, :]
    e, qi = r // CHUNK, r % CHUNK
    kc = k // CHUNK - e
    ok = (kc >= 0) & (kc <= n_prev)
    key_pos = kc * CHUNK + k % CHUNK - n_prev * CHUNK
    return qi - key_pos, ok


def _t5_bucket(rel_kq):
    nb = T5_BUCKETS // 2
    ret = (rel_kq > 0).astype(np.int32) * nb
    n = np.abs(rel_kq)
    max_exact = nb // 2
    large = max_exact + (np.log(np.maximum(n, 1) / max_exact)
                         / math.log(T5_MAX_DIST / max_exact) * (nb - max_exact)).astype(np.int32)
    large = np.minimum(large, nb - 1)
    return ret + np.where(n < max_exact, n, large)


def _toeplitz_bias(table, index_of_rel, n_prev):
    band = (n_prev + 2) * CHUNK
    period = band + PAIR
    delta = np.arange(period)
    delta = np.where(delta < band, delta, delta - period)
    diag = jnp.take(table, index_of_rel(n_prev * CHUNK - delta), axis=0).T
    skew = jnp.tile(diag, (1, PAIR))[:, :PAIR * (period - 1)].reshape(-1, PAIR, period - 1)
    bias = skew[:, :, :band]
    _, ok = _pair_band_index(n_prev)
    return jnp.where(ok[None], bias, NEG_INF).astype(F32)


def _bias_tables(relpos_a, t5_table):
    bias_a = _toeplitz_bias(
        relpos_a, lambda rel: np.clip(rel, -A_MAX_REL, A_MAX_REL) + A_MAX_REL, A_PREV_CHUNKS)
    bias_b = _toeplitz_bias(t5_table, lambda rel: _t5_bucket(-rel), B_PREV_CHUNKS)
    kc_a = np.arange(A_BAND) // CHUNK
    hide_a = np.zeros((PAIRS_PER_TILE + 1, 1, 1, A_BAND), bool)
    for jj in range(PAIRS_PER_TILE):
        hide_a[jj, 0, 0] = 2 * jj - A_PREV_CHUNKS + kc_a < 0
    kc_b = np.arange(B_BAND) // CHUNK
    hide_b = np.zeros((2, 1, 1, B_BAND), bool)
    hide_b[0, 0, 0] = kc_b - B_PREV_CHUNKS < 0
    bias_a = jnp.where(hide_a, NEG_INF, bias_a[None]) * LOG2E
    bias_b = jnp.where(hide_b, NEG_INF, bias_b[None]) * LOG2E
    return bias_a, bias_b


def _inproj_weight(w_in, a_width, b_width, b_kv_width):
    cuts = np.cumsum([a_width, a_width, a_width, b_width, b_kv_width])
    qa, ka, va, qb, kb, vb = jnp.split(w_in, cuts, axis=-1)

    def dup(w):
        heads = [w[:, j * HEAD_DIM:(j + 1) * HEAD_DIM] for j in range(w.shape[1] // HEAD_DIM)]
        return jnp.concatenate([x for hd in heads for x in (hd, hd)], axis=1)

    return jnp.concatenate([qa, ka, va, qb, dup(kb), dup(vb)], axis=1).astype(BF16)


def kernel(x, t5_table, e_norm_pre, e_norm_post, e_w_in, e_w_out, e_relpos_a, e_sink_b,
           o_norm_pre, o_norm_post, o_pool_w, o_pool_scale,
           mlp_norm_pre, mlp_norm_post, mlp_w_up, mlp_w_down):
    bsz, seq_len, d = x.shape
    depth = mlp_w_up.shape[0]
    a_heads = e_relpos_a.shape[2]
    b_heads = e_sink_b.shape[1]
    a_width, b_width = a_heads * HEAD_DIM, b_heads * HEAD_DIM
    b_kv_width = (b_heads // B_GROUP) * HEAD_DIM
    assert seq_len % ATTN_TILE == 0 and seq_len % TOKEN_TILE == 0
    assert a_heads == b_heads and a_heads % 2 == 0
    row = lambda v: v.reshape(1, -1).astype(F32)

    h = x.reshape(bsz * seq_len, d)
    for layer in range(depth):
        i = layer // 2
        w_up = mlp_w_up[layer].astype(BF16)
        w_down = mlp_w_down[layer].astype(BF16)
        g_mpre, g_mpost = row(mlp_norm_pre[layer]), row(mlp_norm_post[layer])
        if layer % 2 == 0:
            w_ext = _inproj_weight(e_w_in[i], a_width, b_width, b_kv_width)
            seg = 512
            q_segs = (0, (3 * a_width) // seg)
            assert a_width == seg and b_width == seg
            proj = _inproj(h, row(e_norm_pre[i]), w_ext, q_segs, seg)
            bias_a, bias_b = _bias_tables(e_relpos_a[i], t5_table)
            oa, ob = _attention(proj, e_sink_b[i].astype(F32) * LOG2E, bias_a, bias_b,
                                seq_len, a_width, b_width, b_kv_width)
            w_out = e_w_out[i].astype(BF16)
            h = _outproj_mlp(oa, ob, h, w_out[:a_width], w_out[a_width:], row(e_norm_post[i]),
                             g_mpre, w_up, w_down, g_mpost)
        else:
            h = _pool_mlp(h, seq_len, row(o_norm_pre[i]), o_pool_w[i].astype(BF16),
                          row(o_pool_scale[i]), row(o_norm_post[i]),
                          g_mpre, w_up, w_down, g_mpost)
    return h.reshape(bsz, seq_len, d)
```

```python
import functools
import math

import jax
import jax.numpy as jnp
import numpy as np
from jax import lax
from jax.experimental import pallas as pl
from jax.experimental.pallas import tpu as pltpu

CHUNK = 64
HEAD_DIM = 64
A_PREV_CHUNKS = 8
A_MAX_REL = 128
B_GROUP = 4
B_PREV_CHUNKS = 2
T5_BUCKETS = 32
T5_MAX_DIST = 128
POOL_WINDOWS = (2, 4, 8, 16)
RMS_EPS = 1e-6
NEG_INF = -1e30
LOG2E = math.log2(math.e)

LANES = 128
V7X_VMEM_BYTES = 64 * 1024 * 1024

TOKEN_TILE = 512
SUB_ROWS = 256
ATTN_TILE = 512
PAIR = 2 * CHUNK
PAIRS_PER_TILE = ATTN_TILE // PAIR
A_BAND = (A_PREV_CHUNKS + 2) * CHUNK
B_BAND = (B_PREV_CHUNKS + 2) * CHUNK
ATTN_LOOKAHEAD = 3
POOL_HALO = 16
FF_CHUNK = 1024
VMEM_LIMIT = 56 * 1024 * 1024

BF16 = jnp.bfloat16
F32 = jnp.float32


def _rms(x, g):
    ms = jnp.mean(x * x, axis=-1, keepdims=True)
    return (x * lax.rsqrt(ms + RMS_EPS)) * g


def _dot(a, b):
    return jnp.dot(a, b, preferred_element_type=F32)


def _dot_nt(a, b):
    return lax.dot_general(a, b, (((1,), (1,)), ((), ())), preferred_element_type=F32)


def _resident(shape):
    nd = len(shape)
    return pl.BlockSpec(shape, lambda *_: (0,) * nd, pipeline_mode=pl.Buffered(1))


def _inproj_kernel(x_ref, g_ref, w_ref, o_ref, *, seg, q_segs, q_scale):
    xb = _rms(x_ref[...], g_ref[...]).astype(BF16)
    for s in range(w_ref.shape[1] // seg):
        y = _dot(xb, w_ref[:, s * seg:(s + 1) * seg])
        if s in q_segs:
            y = y * q_scale
        o_ref[:, s * seg:(s + 1) * seg] = y.astype(BF16)


def _inproj(x2, g, w_ext, q_segs, seg):
    n, d = x2.shape
    wcols = w_ext.shape[1]
    kern = functools.partial(_inproj_kernel, seg=seg, q_segs=q_segs,
                             q_scale=HEAD_DIM ** -0.5 * LOG2E)
    return pl.pallas_call(
        kern,
        grid=(n // TOKEN_TILE,),
        in_specs=[
            pl.BlockSpec((TOKEN_TILE, d), lambda i: (i, 0)),
            _resident((1, d)),
            _resident((d, wcols)),
        ],
        out_specs=pl.BlockSpec((TOKEN_TILE, wcols), lambda i: (i, 0)),
        out_shape=jax.ShapeDtypeStruct((n, wcols), BF16),
        compiler_params=pltpu.CompilerParams(
            dimension_semantics=("arbitrary",), vmem_limit_bytes=VMEM_LIMIT),
        name="inproj",
    )(x2, g, w_ext)


def _softmax_pv(s, v, sink):
    m = jnp.max(s, axis=-1, keepdims=True)
    if sink is not None:
        m = jnp.maximum(m, sink)
    p = jnp.exp2(s - m)
    l = jnp.sum(p, axis=-1, keepdims=True)
    if sink is not None:
        l = l + jnp.exp2(sink - m)
    pb = p.astype(BF16)
    if isinstance(v, tuple):
        w1 = v[0].shape[0]
        o = _dot(pb[:, :w1], v[0]) + _dot(pb[:, w1:], v[1])
    else:
        o = _dot(pb, v)
    return o, 1.0 / l


def _attn_kernel(sink_ref, qa_ref, kap_ref, kac_ref, vap_ref, vac_ref,
                 qb_ref, kbp_ref, kbc_ref, vbp_ref, vbc_ref, biasa_ref, biasb_ref,
                 oa_ref, ob_ref, *, tiles_per_seq, pairs_per_kv):
    i = pl.program_id(0)
    first = (i % tiles_per_seq) == 0
    lane = lax.broadcasted_iota(jnp.int32, (1, LANES), 1)
    lo_half = lane < HEAD_DIM
    zero = jnp.zeros((), BF16)
    n_head_pairs = qa_ref.shape[1] // LANES

    def lanes(hp):
        return slice(hp * LANES, (hp + 1) * LANES)

    def head_q(q_ref, r0, hp, e):
        q2 = q_ref[r0:r0 + PAIR, lanes(hp)]
        return jnp.where(lo_half if e == 0 else ~lo_half, q2, zero)

    def logits_a(jj, hp, e):
        r0 = jj * PAIR
        qm = head_q(qa_ref, r0, hp, e)
        s = jnp.concatenate([_dot_nt(qm, kap_ref[r0:, lanes(hp)]),
                             _dot_nt(qm, kac_ref[:r0 + PAIR, lanes(hp)])], axis=1)
        return s + biasa_ref[jnp.where(first, jj, PAIRS_PER_TILE), 2 * hp + e]

    def finish_a(jj, hp, e, s):
        r0 = jj * PAIR
        o, rl = _softmax_pv(s, (vap_ref[r0:, lanes(hp)], vac_ref[:r0 + PAIR, lanes(hp)]), None)
        return o * rl

    def band_b(p_ref, c_ref, jj, hp):
        r0 = jj * PAIR
        kv = lanes(hp // pairs_per_kv)
        if jj == 0:
            return jnp.concatenate([p_ref[:, kv], c_ref[:PAIR, kv]], axis=0)
        return c_ref[r0 - PAIR:r0 + PAIR, kv]

    def logits_b(jj, hp, e):
        s = _dot_nt(head_q(qb_ref, jj * PAIR, hp, e), band_b(kbp_ref, kbc_ref, jj, hp))
        variant = jnp.where(first, 0, 1) if jj == 0 else 1
        return s + biasb_ref[variant, 2 * hp + e]

    def finish_b(jj, hp, e, s):
        o, rl = _softmax_pv(s, band_b(vbp_ref, vbc_ref, jj, hp), sink_ref[2 * hp + e])
        return o * rl

    mixers = {"a": (logits_a, finish_a, oa_ref), "b": (logits_b, finish_b, ob_ref)}
    units = [(mx, jj, hp, e) for hp in range(n_head_pairs) for jj in range(PAIRS_PER_TILE)
             for mx in ("a", "b") for e in range(2)]
    pending, done = {}, {}
    for t in range(len(units) + ATTN_LOOKAHEAD):
        if t < len(units):
            mx, jj, hp, e = units[t]
            pending[t] = mixers[mx][0](jj, hp, e)
        if t >= ATTN_LOOKAHEAD:
            mx, jj, hp, e = units[t - ATTN_LOOKAHEAD]
            done[e] = mixers[mx][1](jj, hp, e, pending.pop(t - ATTN_LOOKAHEAD))
            if e == 1:
                mixers[mx][2][jj * PAIR:(jj + 1) * PAIR, lanes(hp)] = (
                    jnp.where(lo_half, done[0], done[1]).astype(BF16))


def _attention(proj, sink, bias_a, bias_b, seq_len, a_width, b_width, b_kv_width):
    n = proj.shape[0]
    tiles_per_seq = seq_len // ATTN_TILE
    kv_dup = 2 * b_kv_width
    assert a_width == b_width and kv_dup % LANES == 0 and a_width % kv_dup == 0
    sub = ATTN_TILE // PAIR
    qa, ka, va, qb = 0, 1, 2, 3
    kb = 4 * a_width // kv_dup
    vb = kb + 1

    def cur(width, c):
        return pl.BlockSpec((ATTN_TILE, width), lambda i: (i, c))

    def prev(c):
        return pl.BlockSpec((ATTN_TILE, a_width), lambda i: (jnp.maximum(i - 1, 0), c))

    def prev_pair(c):
        return pl.BlockSpec((PAIR, kv_dup), lambda i: (jnp.maximum(i * sub - 1, 0), c))

    kern = functools.partial(_attn_kernel, tiles_per_seq=tiles_per_seq,
                             pairs_per_kv=(b_width // LANES) // (kv_dup // LANES))
    return pl.pallas_call(
        kern,
        grid=(n // ATTN_TILE,),
        in_specs=[
            pl.BlockSpec(memory_space=pltpu.SMEM),
            cur(a_width, qa), prev(ka), cur(a_width, ka), prev(va), cur(a_width, va),
            cur(b_width, qb), prev_pair(kb), cur(kv_dup, kb), prev_pair(vb), cur(kv_dup, vb),
            _resident(bias_a.shape), _resident(bias_b.shape),
        ],
        out_specs=[cur(a_width, 0), cur(b_width, 0)],
        out_shape=[jax.ShapeDtypeStruct((n, a_width), BF16),
                   jax.ShapeDtypeStruct((n, b_width), BF16)],
        compiler_params=pltpu.CompilerParams(
            dimension_semantics=("arbitrary",), vmem_limit_bytes=VMEM_LIMIT),
        name="band_attention",
    )(sink, proj, proj, proj, proj, proj, proj, proj, proj, proj, proj, bias_a, bias_b)


def _mlp_rows(n_rows, early_fn, h1_fn, o_ref, gpre_ref, wup_ref, wdown_ref, gpost_ref,
              side_jobs=()):
    n_sub = n_rows // SUB_ROWS
    n_chunks = wup_ref.shape[1] // FF_CHUNK
    n_slots = n_sub * n_chunks * 2
    jobs = list(side_jobs)
    job_slot = {round((j + 0.5) * n_slots / len(jobs)): job for j, job in enumerate(jobs)}
    assert len(job_slot) == len(jobs) and all(0 < k <= n_slots for k in job_slot)
    rows = [slice(s * SUB_ROWS, (s + 1) * SUB_ROWS) for s in range(n_sub)]
    early = {0: early_fn(rows[0])} if early_fn else {}
    slot = 0
    for s in range(n_sub):
        if early_fn and s + 1 < n_sub:
            early[s + 1] = early_fn(rows[s + 1])
        h1 = h1_fn(rows[s], early.pop(s, None))
        hn = _rms(h1, gpre_ref[...]).astype(BF16)
        acc = None
        for c in range(n_chunks):
            ff = slice(c * FF_CHUNK, (c + 1) * FF_CHUNK)
            u = jnp.maximum(_dot(hn, wup_ref[:, ff]), 0.0)
            slot += 1
            if slot in job_slot:
                job_slot[slot]()
            part = _dot((u * u).astype(BF16), wdown_ref[ff, :])
            slot += 1
            if slot in job_slot:
                job_slot[slot]()
            acc = part if acc is None else acc + part
        o_ref[rows[s], :] = h1 + _rms(acc, gpost_ref[...])


def _outproj_mlp_kernel(oa_ref, ob_ref, x_ref, woa_ref, wob_ref, gpost_ref,
                        gpre_ref, wup_ref, wdown_ref, gmpost_ref, o_ref):
    def out_proj(rows):
        return _dot(oa_ref[rows, :], woa_ref[...]) + _dot(ob_ref[rows, :], wob_ref[...])

    def h1_fn(rows, y):
        return x_ref[rows, :] + _rms(y, gpost_ref[...])

    _mlp_rows(x_ref.shape[0], out_proj, h1_fn, o_ref, gpre_ref, wup_ref, wdown_ref, gmpost_ref)


def _outproj_mlp(oa, ob, x2, w_oa, w_ob, g_post, g_mpre, w_up, w_down, g_mpost):
    n, d = x2.shape
    row = lambda w: pl.BlockSpec((TOKEN_TILE, w), lambda i: (i, 0))
    return pl.pallas_call(
        _outproj_mlp_kernel,
        grid=(n // TOKEN_TILE,),
        in_specs=[row(oa.shape[1]), row(ob.shape[1]), row(d),
                  _resident(w_oa.shape), _resident(w_ob.shape), _resident((1, d)),
                  _resident((1, d)), _resident(w_up.shape), _resident(w_down.shape),
                  _resident((1, d))],
        out_specs=row(d),
        out_shape=jax.ShapeDtypeStruct((n, d), F32),
        compiler_params=pltpu.CompilerParams(
            dimension_semantics=("arbitrary",), vmem_limit_bytes=VMEM_LIMIT),
        name="outproj_mlp",
    )(oa, ob, x2, w_oa, w_ob, g_post, g_mpre, w_up, w_down, g_mpost)


def _pool_mixer_jobs(h_ref, halo_fn, g_ref, pw_ref, pscale_ref, t_in_seq, y_ref):
    tm = h_ref.shape[0]
    pc = pw_ref.shape[1]
    st = {}

    def prepare():
        st["hn"] = _rms(h_ref[...], g_ref[...])
        st["halo"] = halo_fn()
        pos1 = t_in_seq * tm + 1 + lax.broadcasted_iota(jnp.int32, (tm, 1), 0)
        st["pos1"] = pos1
        st["inv_pos1"] = 1.0 / pos1.astype(F32)

    def group(gi, w):
        cols = slice(gi * pc, (gi + 1) * pc)
        hn = st["hn"][:, cols]
        wsum = jnp.concatenate([st["halo"][:, cols], hn], axis=0)
        k = 1
        while k < w:
            wsum = wsum + pltpu.roll(wsum, k, axis=0)
            k *= 2
        inv_cnt = jnp.where(st["pos1"] < w, st["inv_pos1"], 1.0 / w)
        d = (wsum[POOL_HALO:, :] * inv_cnt - hn).astype(BF16)
        y_ref[:, cols] = _dot(d, pw_ref[gi]) * pscale_ref[:, cols]

    return [prepare] + [functools.partial(group, gi, w) for gi, w in enumerate(POOL_WINDOWS)]


def _pool_mlp_kernel(h_ref, hnext_ref, gpre_ref, pw_ref, pscale_ref, gpost_ref,
                     gmpre_ref, wup_ref, wdown_ref, gmpost_ref, o_ref, y_ref, *, tiles_per_seq):
    i = pl.program_id(0)
    tm, dm = h_ref.shape
    @pl.when(i == 0)
    def _():
        for job in _pool_mixer_jobs(h_ref, lambda: jnp.zeros((POOL_HALO, dm), F32),
                                    gpre_ref, pw_ref, pscale_ref, 0, y_ref):
            job()

    y = y_ref[...]

    def h1_fn(rows, _):
        return h_ref[rows, :] + _rms(y[rows, :], gpost_ref[...])

    t_next = (i + 1) % tiles_per_seq

    def next_halo():
        keep = jnp.where(t_next == 0, 0.0, 1.0)
        return _rms(h_ref[tm - POOL_HALO:, :], gpre_ref[...]) * keep

    jobs = _pool_mixer_jobs(hnext_ref, next_halo, gpre_ref, pw_ref, pscale_ref, t_next, y_ref)
    _mlp_rows(tm, None, h1_fn, o_ref, gmpre_ref, wup_ref, wdown_ref, gmpost_ref, jobs)


def _pool_mlp(h2, seq_len, g_pre, pool_w, pool_scale, g_post, g_mpre, w_up, w_down, g_mpost):
    n, d = h2.shape
    tiles_per_seq = seq_len // TOKEN_TILE
    n_tiles = n // TOKEN_TILE
    row = pl.BlockSpec((TOKEN_TILE, d), lambda i: (i, 0))
    nxt = pl.BlockSpec((TOKEN_TILE, d), lambda i: (jnp.minimum(i + 1, n_tiles - 1), 0))
    kern = functools.partial(_pool_mlp_kernel, tiles_per_seq=tiles_per_seq)
    return pl.pallas_call(
        kern,
        grid=(n_tiles,),
        in_specs=[row, nxt, _resident((1, d)), _resident(pool_w.shape), _resident((1, d)),
                  _resident((1, d)), _resident((1, d)), _resident(w_up.shape),
                  _resident(w_down.shape), _resident((1, d))],
        out_specs=row,
        out_shape=jax.ShapeDtypeStruct((n, d), F32),
        scratch_shapes=[pltpu.VMEM((TOKEN_TILE, d), F32)],
        compiler_params=pltpu.CompilerParams(
            dimension_semantics=("arbitrary",), vmem_limit_bytes=VMEM_LIMIT),
        name="pool_mlp",
    )(h2, h2, g_pre, pool_w, pool_scale, g_post, g_mpre, w_up, w_down, g_mpost)


def _pair_band_index(n_prev):
    band = (n_prev + 2) * CHUNK
    r = np.arange(PAIR)[:, None]
    k = np.arange(band)[None, :]
    e, qi = r // CHUNK, r % CHUNK
    kc = k // CHUNK - e
    ok = (kc >= 0) & (kc <= n_prev)
    key_pos = kc * CHUNK + k % CHUNK - n_prev * CHUNK
    return qi - key_pos, ok


def _t5_bucket(rel_kq):
    nb = T5_BUCKETS // 2
    ret = (rel_kq > 0).astype(np.int32) * nb
    n = np.abs(rel_kq)
    max_exact = nb // 2
    large = max_exact + (np.log(np.maximum(n, 1) / max_exact)
                         / math.log(T5_MAX_DIST / max_exact) * (nb - max_exact)).astype(np.int32)
    large = np.minimum(large, nb - 1)
    return ret + np.where(n < max_exact, n, large)


def _toeplitz_bias(table, index_of_rel, n_prev):
    band = (n_prev + 2) * CHUNK
    period = band + PAIR
    delta = np.arange(period)
    delta = np.where(delta < band, delta, delta - period)
    diag = jnp.take(table, index_of_rel(n_prev * CHUNK - delta), axis=0).T
    skew = jnp.tile(diag, (1, PAIR))[:, :PAIR * (period - 1)].reshape(-1, PAIR, period - 1)
    bias = skew[:, :, :band]
    _, ok = _pair_band_index(n_prev)
    return jnp.where(ok[None], bias, NEG_INF).astype(F32)


def _bias_tables(relpos_a, t5_table):
    bias_a = _toeplitz_bias(
        relpos_a, lambda rel: np.clip(rel, -A_MAX_REL, A_MAX_REL) + A_MAX_REL, A_PREV_CHUNKS)
    bias_b = _toeplitz_bias(t5_table, lambda rel: _t5_bucket(-rel), B_PREV_CHUNKS)
    kc_a = np.arange(A_BAND) // CHUNK
    hide_a = np.zeros((PAIRS_PER_TILE + 1, 1, 1, A_BAND), bool)
    for jj in range(PAIRS_PER_TILE):
        hide_a[jj, 0, 0] = 2 * jj - A_PREV_CHUNKS + kc_a < 0
    kc_b = np.arange(B_BAND) // CHUNK
    hide_b = np.zeros((2, 1, 1, B_BAND), bool)
    hide_b[0, 0, 0] = kc_b - B_PREV_CHUNKS < 0
    bias_a = jnp.where(hide_a, NEG_INF, bias_a[None]) * LOG2E
    bias_b = jnp.where(hide_b, NEG_INF, bias_b[None]) * LOG2E
    return bias_a, bias_b


def _inproj_weight(w_in, a_width, b_width, b_kv_width):
    cuts = np.cumsum([a_width, a_width, a_width, b_width, b_kv_width])
    qa, ka, va, qb, kb, vb = jnp.split(w_in, cuts, axis=-1)

    def dup(w):
        heads = [w[:, j * HEAD_DIM:(j + 1) * HEAD_DIM] for j in range(w.shape[1] // HEAD_DIM)]
        return jnp.concatenate([x for hd in heads for x in (hd, hd)], axis=1)

    return jnp.concatenate([qa, ka, va, qb, dup(kb), dup(vb)], axis=1).astype(BF16)


def kernel(x, t5_table, e_norm_pre, e_norm_post, e_w_in, e_w_out, e_relpos_a, e_sink_b,
           o_norm_pre, o_norm_post, o_pool_w, o_pool_scale,
           mlp_norm_pre, mlp_norm_post, mlp_w_up, mlp_w_down):
    bsz, seq_len, d = x.shape
    depth = mlp_w_up.shape[0]
    a_heads = e_relpos_a.shape[2]
    b_heads = e_sink_b.shape[1]
    a_width, b_width = a_heads * HEAD_DIM, b_heads * HEAD_DIM
    b_kv_width = (b_heads // B_GROUP) * HEAD_DIM
    assert seq_len % ATTN_TILE == 0 and seq_len % TOKEN_TILE == 0
    assert a_heads == b_heads and a_heads % 2 == 0
    row = lambda v: v.reshape(1, -1).astype(F32)

    h = x.reshape(bsz * seq_len, d)
    for layer in range(depth):
        i = layer // 2
        w_up = mlp_w_up[layer].astype(BF16)
        w_down = mlp_w_down[layer].astype(BF16)
        g_mpre, g_mpost = row(mlp_norm_pre[layer]), row(mlp_norm_post[layer])
        if layer % 2 == 0:
            w_ext = _inproj_weight(e_w_in[i], a_width, b_width, b_kv_width)
            seg = 512
            q_segs = (0, (3 * a_width) // seg)
            assert a_width == seg and b_width == seg
            proj = _inproj(h, row(e_norm_pre[i]), w_ext, q_segs, seg)
            bias_a, bias_b = _bias_tables(e_relpos_a[i], t5_table)
            oa, ob = _attention(proj, e_sink_b[i].astype(F32) * LOG2E, bias_a, bias_b,
                                seq_len, a_width, b_width, b_kv_width)
            w_out = e_w_out[i].astype(BF16)
            h = _outproj_mlp(oa, ob, h, w_out[:a_width], w_out[a_width:], row(e_norm_post[i]),
                             g_mpre, w_up, w_down, g_mpost)
        else:
            h = _pool_mlp(h, seq_len, row(o_norm_pre[i]), o_pool_w[i].astype(BF16),
                          row(o_pool_scale[i]), row(o_norm_post[i]),
                          g_mpre, w_up, w_down, g_mpost)
    return h.reshape(bsz, seq_len, d)
```

```python
import functools
import math

import jax
import jax.numpy as jnp
import numpy as np
from jax import lax
from jax.experimental import pallas as pl
from jax.experimental.pallas import tpu as pltpu

CHUNK = 64
HEAD_DIM = 64
A_PREV_CHUNKS = 8
A_MAX_REL = 128
B_GROUP = 4
B_PREV_CHUNKS = 2
T5_BUCKETS = 32
T5_MAX_DIST = 128
POOL_WINDOWS = (2, 4, 8, 16)
RMS_EPS = 1e-6
NEG_INF = -1e30
LOG2E = math.log2(math.e)

LANES = 128
V7X_VMEM_BYTES = 64 * 1024 * 1024

TOKEN_TILE = 512
INPROJ_TILE = 1024
OUTPROJ_TILE = 1024
SUB_ROWS = 256
ATTN_TILE = 512
PAIR = 2 * CHUNK
PAIRS_PER_TILE = ATTN_TILE // PAIR
A_BAND = (A_PREV_CHUNKS + 2) * CHUNK
B_BAND = (B_PREV_CHUNKS + 2) * CHUNK
ATTN_LOOKAHEAD = 3
POOL_HALO = 16
FF_CHUNK = 1024
VMEM_LIMIT = 56 * 1024 * 1024

BF16 = jnp.bfloat16
F32 = jnp.float32


def _rms(x, g):
    ms = jnp.mean(x * x, axis=-1, keepdims=True)
    return (x * lax.rsqrt(ms + RMS_EPS)) * g


def _dot(a, b):
    return jnp.dot(a, b, preferred_element_type=F32)


def _dot_nt(a, b):
    return lax.dot_general(a, b, (((1,), (1,)), ((), ())), preferred_element_type=F32)


def _resident(shape):
    nd = len(shape)
    return pl.BlockSpec(shape, lambda *_: (0,) * nd, pipeline_mode=pl.Buffered(1))


def _inproj_kernel(x_ref, g_ref, w_ref, o_ref, *, seg, q_segs, q_scale, kv_b0):
    lo_half = lax.broadcasted_iota(jnp.int32, (1, LANES), 1) < HEAD_DIM
    for r in range(x_ref.shape[0] // SUB_ROWS):
        rows = slice(r * SUB_ROWS, (r + 1) * SUB_ROWS)
        xb = _rms(x_ref[rows, :], g_ref[...]).astype(BF16)
        for s in range(kv_b0 // seg):
            y = _dot(xb, w_ref[:, s * seg:(s + 1) * seg])
            if s in q_segs:
                y = y * q_scale
            o_ref[rows, s * seg:(s + 1) * seg] = y.astype(BF16)
        y = _dot(xb, w_ref[:, kv_b0:])
        for j in range(y.shape[1] // LANES):
            two = y[:, j * LANES:(j + 1) * LANES]
            swapped = pltpu.roll(two, HEAD_DIM, axis=1)
            c0 = kv_b0 + 2 * j * LANES
            o_ref[rows, c0:c0 + LANES] = jnp.where(lo_half, two, swapped).astype(BF16)
            o_ref[rows, c0 + LANES:c0 + 2 * LANES] = jnp.where(lo_half, swapped, two).astype(BF16)


def _inproj(x2, g, w_in, q_segs, seg, kv_b0):
    n, d = x2.shape
    out_cols = kv_b0 + 2 * (w_in.shape[1] - kv_b0)
    assert n % INPROJ_TILE == 0 and kv_b0 % seg == 0
    kern = functools.partial(_inproj_kernel, seg=seg, q_segs=q_segs,
                             q_scale=HEAD_DIM ** -0.5 * LOG2E, kv_b0=kv_b0)
    return pl.pallas_call(
        kern,
        grid=(n // INPROJ_TILE,),
        in_specs=[
            pl.BlockSpec((INPROJ_TILE, d), lambda i: (i, 0)),
            _resident((1, d)),
            _resident(w_in.shape),
        ],
        out_specs=pl.BlockSpec((INPROJ_TILE, out_cols), lambda i: (i, 0)),
        out_shape=jax.ShapeDtypeStruct((n, out_cols), BF16),
        compiler_params=pltpu.CompilerParams(
            dimension_semantics=("arbitrary",), vmem_limit_bytes=VMEM_LIMIT),
        name="inproj",
    )(x2, g, w_in)


def _softmax_pv(s, v, sink):
    m = jnp.max(s, axis=-1, keepdims=True)
    if sink is not None:
        m = jnp.maximum(m, sink)
    p = jnp.exp2(s - m)
    l = jnp.sum(p, axis=-1, keepdims=True)
    if sink is not None:
        l = l + jnp.exp2(sink - m)
    pb = p.astype(BF16)
    if isinstance(v, tuple):
        w1 = v[0].shape[0]
        o = _dot(pb[:, :w1], v[0]) + _dot(pb[:, w1:], v[1])
    else:
        o = _dot(pb, v)
    return o, 1.0 / l


def _attn_kernel(sink_ref, qa_ref, kap_ref, kac_ref, vap_ref, vac_ref,
                 qb_ref, kbp_ref, kbc_ref, vbp_ref, vbc_ref, biasa_ref, biasb_ref,
                 oa_ref, ob_ref, *, tiles_per_seq, pairs_per_kv):
    i = pl.program_id(0)
    first = (i % tiles_per_seq) == 0
    lane = lax.broadcasted_iota(jnp.int32, (1, LANES), 1)
    lo_half = lane < HEAD_DIM
    zero = jnp.zeros((), BF16)
    n_head_pairs = qa_ref.shape[1] // LANES

    def lanes(hp):
        return slice(hp * LANES, (hp + 1) * LANES)

    def head_q(q_ref, r0, hp, e):
        q2 = q_ref[r0:r0 + PAIR, lanes(hp)]
        return jnp.where(lo_half if e == 0 else ~lo_half, q2, zero)

    def logits_a(jj, hp, e):
        r0 = jj * PAIR
        qm = head_q(qa_ref, r0, hp, e)
        s = jnp.concatenate([_dot_nt(qm, kap_ref[r0:, lanes(hp)]),
                             _dot_nt(qm, kac_ref[:r0 + PAIR, lanes(hp)])], axis=1)
        return s + biasa_ref[jnp.where(first, jj, PAIRS_PER_TILE), 2 * hp + e]

    def finish_a(jj, hp, e, s):
        r0 = jj * PAIR
        o, rl = _softmax_pv(s, (vap_ref[r0:, lanes(hp)], vac_ref[:r0 + PAIR, lanes(hp)]), None)
        return o * rl

    def band_b(p_ref, c_ref, jj, hp):
        r0 = jj * PAIR
        kv = lanes(hp // pairs_per_kv)
        if jj == 0:
            return jnp.concatenate([p_ref[:, kv], c_ref[:PAIR, kv]], axis=0)
        return c_ref[r0 - PAIR:r0 + PAIR, kv]

    def logits_b(jj, hp, e):
        s = _dot_nt(head_q(qb_ref, jj * PAIR, hp, e), band_b(kbp_ref, kbc_ref, jj, hp))
        variant = jnp.where(first, 0, 1) if jj == 0 else 1
        return s + biasb_ref[variant, 2 * hp + e]

    def finish_b(jj, hp, e, s):
        o, rl = _softmax_pv(s, band_b(vbp_ref, vbc_ref, jj, hp), sink_ref[2 * hp + e])
        return o * rl

    mixers = {"a": (logits_a, finish_a, oa_ref), "b": (logits_b, finish_b, ob_ref)}
    units = [(mx, jj, hp, e) for hp in range(n_head_pairs) for jj in range(PAIRS_PER_TILE)
             for mx in ("a", "b") for e in range(2)]
    pending, done = {}, {}
    for t in range(len(units) + ATTN_LOOKAHEAD):
        if t < len(units):
            mx, jj, hp, e = units[t]
            pending[t] = mixers[mx][0](jj, hp, e)
        if t >= ATTN_LOOKAHEAD:
            mx, jj, hp, e = units[t - ATTN_LOOKAHEAD]
            done[e] = mixers[mx][1](jj, hp, e, pending.pop(t - ATTN_LOOKAHEAD))
            if e == 1:
                mixers[mx][2][jj * PAIR:(jj + 1) * PAIR, lanes(hp)] = (
                    jnp.where(lo_half, done[0], done[1]).astype(BF16))


def _attention(proj, sink, bias_a, bias_b, seq_len, a_width, b_width, b_kv_width):
    n = proj.shape[0]
    tiles_per_seq = seq_len // ATTN_TILE
    kv_dup = 2 * b_kv_width
    assert a_width == b_width and kv_dup % LANES == 0 and a_width % kv_dup == 0
    sub = ATTN_TILE // PAIR
    qa, ka, va, qb = 0, 1, 2, 3
    kb = 4 * a_width // kv_dup
    vb = kb + 1

    def cur(width, c):
        return pl.BlockSpec((ATTN_TILE, width), lambda i: (i, c))

    def prev(c):
        return pl.BlockSpec((ATTN_TILE, a_width), lambda i: (jnp.maximum(i - 1, 0), c))

    def prev_pair(c):
        return pl.BlockSpec((PAIR, kv_dup), lambda i: (jnp.maximum(i * sub - 1, 0), c))

    kern = functools.partial(_attn_kernel, tiles_per_seq=tiles_per_seq,
                             pairs_per_kv=(b_width // LANES) // (kv_dup // LANES))
    return pl.pallas_call(
        kern,
        grid=(n // ATTN_TILE,),
        in_specs=[
            pl.BlockSpec(memory_space=pltpu.SMEM),
            cur(a_width, qa), prev(ka), cur(a_width, ka), prev(va), cur(a_width, va),
            cur(b_width, qb), prev_pair(kb), cur(kv_dup, kb), prev_pair(vb), cur(kv_dup, vb),
            _resident(bias_a.shape), _resident(bias_b.shape),
        ],
        out_specs=[cur(a_width, 0), cur(b_width, 0)],
        out_shape=[jax.ShapeDtypeStruct((n, a_width), BF16),
                   jax.ShapeDtypeStruct((n, b_width), BF16)],
        compiler_params=pltpu.CompilerParams(
            dimension_semantics=("arbitrary",), vmem_limit_bytes=VMEM_LIMIT),
        name="band_attention",
    )(sink, proj, proj, proj, proj, proj, proj, proj, proj, proj, proj, bias_a, bias_b)


def _mlp_rows(n_rows, early_fn, h1_fn, o_ref, gpre_ref, wup_ref, wdown_ref, gpost_ref,
              side_jobs=()):
    n_sub = n_rows // SUB_ROWS
    n_chunks = wup_ref.shape[1] // FF_CHUNK
    n_slots = n_sub * n_chunks * 2
    jobs = list(side_jobs)
    job_slot = {round((j + 0.5) * n_slots / len(jobs)): job for j, job in enumerate(jobs)}
    assert len(job_slot) == len(jobs) and all(0 < k <= n_slots for k in job_slot)
    rows = [slice(s * SUB_ROWS, (s + 1) * SUB_ROWS) for s in range(n_sub)]
    early = {0: early_fn(rows[0])} if early_fn else {}
    slot = 0
    for s in range(n_sub):
        if early_fn and s + 1 < n_sub:
            early[s + 1] = early_fn(rows[s + 1])
        h1 = h1_fn(rows[s], early.pop(s, None))
        hn = _rms(h1, gpre_ref[...]).astype(BF16)
        acc = None
        for c in range(n_chunks):
            ff = slice(c * FF_CHUNK, (c + 1) * FF_CHUNK)
            u = jnp.maximum(_dot(hn, wup_ref[:, ff]), 0.0)
            slot += 1
            if slot in job_slot:
                job_slot[slot]()
            part = _dot((u * u).astype(BF16), wdown_ref[ff, :])
            slot += 1
            if slot in job_slot:
                job_slot[slot]()
            acc = part if acc is None else acc + part
        o_ref[rows[s], :] = h1 + _rms(acc, gpost_ref[...])


def _outproj_mlp_kernel(oa_ref, ob_ref, x_ref, woa_ref, wob_ref, gpost_ref,
                        gpre_ref, wup_ref, wdown_ref, gmpost_ref, o_ref):
    def out_proj(rows):
        return _dot(oa_ref[rows, :], woa_ref[...]) + _dot(ob_ref[rows, :], wob_ref[...])

    def h1_fn(rows, y):
        return x_ref[rows, :] + _rms(y, gpost_ref[...])

    _mlp_rows(x_ref.shape[0], out_proj, h1_fn, o_ref, gpre_ref, wup_ref, wdown_ref, gmpost_ref)


def _outproj_mlp(oa, ob, x2, w_oa, w_ob, g_post, g_mpre, w_up, w_down, g_mpost):
    n, d = x2.shape
    assert n % OUTPROJ_TILE == 0
    row = lambda w: pl.BlockSpec((OUTPROJ_TILE, w), lambda i: (i, 0))
    return pl.pallas_call(
        _outproj_mlp_kernel,
        grid=(n // OUTPROJ_TILE,),
        in_specs=[row(oa.shape[1]), row(ob.shape[1]), row(d),
                  _resident(w_oa.shape), _resident(w_ob.shape), _resident((1, d)),
                  _resident((1, d)), _resident(w_up.shape), _resident(w_down.shape),
                  _resident((1, d))],
        out_specs=row(d),
        out_shape=jax.ShapeDtypeStruct((n, d), F32),
        compiler_params=pltpu.CompilerParams(
            dimension_semantics=("arbitrary",), vmem_limit_bytes=VMEM_LIMIT),
        name="outproj_mlp",
    )(oa, ob, x2, w_oa, w_ob, g_post, g_mpre, w_up, w_down, g_mpost)


def _pool_mixer_jobs(h_ref, halo_fn, g_ref, pw_ref, pscale_ref, t_in_seq, y_ref):
    tm = h_ref.shape[0]
    pc = pw_ref.shape[1]
    st = {}

    def prepare():
        st["hn"] = _rms(h_ref[...], g_ref[...])
        st["halo"] = halo_fn()
        pos1 = t_in_seq * tm + 1 + lax.broadcasted_iota(jnp.int32, (tm, 1), 0)
        st["pos1"] = pos1
        st["inv_pos1"] = 1.0 / pos1.astype(F32)

    def group(gi, w):
        cols = slice(gi * pc, (gi + 1) * pc)
        hn = st["hn"][:, cols]
        wsum = jnp.concatenate([st["halo"][:, cols], hn], axis=0)
        k = 1
        while k < w:
            wsum = wsum + pltpu.roll(wsum, k, axis=0)
            k *= 2
        inv_cnt = jnp.where(st["pos1"] < w, st["inv_pos1"], 1.0 / w)
        d = (wsum[POOL_HALO:, :] * inv_cnt - hn).astype(BF16)
        y_ref[:, cols] = _dot(d, pw_ref[gi]) * pscale_ref[:, cols]

    return [prepare] + [functools.partial(group, gi, w) for gi, w in enumerate(POOL_WINDOWS)]


def _pool_mlp_kernel(h_ref, hnext_ref, gpre_ref, pw_ref, pscale_ref, gpost_ref,
                     gmpre_ref, wup_ref, wdown_ref, gmpost_ref, o_ref, y_ref, *, tiles_per_seq):
    i = pl.program_id(0)
    tm, dm = h_ref.shape
    @pl.when(i == 0)
    def _():
        for job in _pool_mixer_jobs(h_ref, lambda: jnp.zeros((POOL_HALO, dm), F32),
                                    gpre_ref, pw_ref, pscale_ref, 0, y_ref):
            job()

    y = y_ref[...]

    def h1_fn(rows, _):
        return h_ref[rows, :] + _rms(y[rows, :], gpost_ref[...])

    t_next = (i + 1) % tiles_per_seq

    def next_halo():
        keep = jnp.where(t_next == 0, 0.0, 1.0)
        return _rms(h_ref[tm - POOL_HALO:, :], gpre_ref[...]) * keep

    jobs = _pool_mixer_jobs(hnext_ref, next_halo, gpre_ref, pw_ref, pscale_ref, t_next, y_ref)
    _mlp_rows(tm, None, h1_fn, o_ref, gmpre_ref, wup_ref, wdown_ref, gmpost_ref, jobs)


def _pool_mlp(h2, seq_len, g_pre, pool_w, pool_scale, g_post, g_mpre, w_up, w_down, g_mpost):
    n, d = h2.shape
    tiles_per_seq = seq_len // TOKEN_TILE
    n_tiles = n // TOKEN_TILE
    row = pl.BlockSpec((TOKEN_TILE, d), lambda i: (i, 0))
    nxt = pl.BlockSpec((TOKEN_TILE, d), lambda i: (jnp.minimum(i + 1, n_tiles - 1), 0))
    kern = functools.partial(_pool_mlp_kernel, tiles_per_seq=tiles_per_seq)
    return pl.pallas_call(
        kern,
        grid=(n_tiles,),
        in_specs=[row, nxt, _resident((1, d)), _resident(pool_w.shape), _resident((1, d)),
                  _resident((1, d)), _resident((1, d)), _resident(w_up.shape),
                  _resident(w_down.shape), _resident((1, d))],
        out_specs=row,
        out_shape=jax.ShapeDtypeStruct((n, d), F32),
        scratch_shapes=[pltpu.VMEM((TOKEN_TILE, d), F32)],
        compiler_params=pltpu.CompilerParams(
            dimension_semantics=("arbitrary",), vmem_limit_bytes=VMEM_LIMIT),
        name="pool_mlp",
    )(h2, h2, g_pre, pool_w, pool_scale, g_post, g_mpre, w_up, w_down, g_mpost)


def _pair_band_index(n_prev):
    band = (n_prev + 2) * CHUNK
    r = np.arange(PAIR)[:, None]
    k = np.arange(band)[None, :]
    e, qi = r // CHUNK, r % CHUNK
    kc = k // CHUNK - e
    ok = (kc >= 0) & (kc <= n_prev)
    key_pos = kc * CHUNK + k % CHUNK - n_prev * CHUNK
    return qi - key_pos, ok


def _t5_bucket(rel_kq):
    nb = T5_BUCKETS // 2
    ret = (rel_kq > 0).astype(np.int32) * nb
    n = np.abs(rel_kq)
    max_exact = nb // 2
    large = max_exact + (np.log(np.maximum(n, 1) / max_exact)
                         / math.log(T5_MAX_DIST / max_exact) * (nb - max_exact)).astype(np.int32)
    large = np.minimum(large, nb - 1)
    return ret + np.where(n < max_exact, n, large)


def _toeplitz_bias(table, index_of_rel, n_prev):
    band = (n_prev + 2) * CHUNK
    period = band + PAIR
    delta = np.arange(period)
    delta = np.where(delta < band, delta, delta - period)
    diag = jnp.take(table, index_of_rel(n_prev * CHUNK - delta), axis=0).T
    skew = jnp.tile(diag, (1, PAIR))[:, :PAIR * (period - 1)].reshape(-1, PAIR, period - 1)
    bias = skew[:, :, :band]
    _, ok = _pair_band_index(n_prev)
    return jnp.where(ok[None], bias, NEG_INF).astype(F32)


def _bias_tables(relpos_a, t5_table):
    bias_a = _toeplitz_bias(
        relpos_a, lambda rel: np.clip(rel, -A_MAX_REL, A_MAX_REL) + A_MAX_REL, A_PREV_CHUNKS)
    bias_b = _toeplitz_bias(t5_table, lambda rel: _t5_bucket(-rel), B_PREV_CHUNKS)
    kc_a = np.arange(A_BAND) // CHUNK
    hide_a = np.zeros((PAIRS_PER_TILE + 1, 1, 1, A_BAND), bool)
    for jj in range(PAIRS_PER_TILE):
        hide_a[jj, 0, 0] = 2 * jj - A_PREV_CHUNKS + kc_a < 0
    kc_b = np.arange(B_BAND) // CHUNK
    hide_b = np.zeros((2, 1, 1, B_BAND), bool)
    hide_b[0, 0, 0] = kc_b - B_PREV_CHUNKS < 0
    bias_a = jnp.where(hide_a, NEG_INF, bias_a[None]) * LOG2E
    bias_b = jnp.where(hide_b, NEG_INF, bias_b[None]) * LOG2E
    return bias_a, bias_b


def kernel(x, t5_table, e_norm_pre, e_norm_post, e_w_in, e_w_out, e_relpos_a, e_sink_b,
           o_norm_pre, o_norm_post, o_pool_w, o_pool_scale,
           mlp_norm_pre, mlp_norm_post, mlp_w_up, mlp_w_down):
    bsz, seq_len, d = x.shape
    depth = mlp_w_up.shape[0]
    a_heads = e_relpos_a.shape[2]
    b_heads = e_sink_b.shape[1]
    a_width, b_width = a_heads * HEAD_DIM, b_heads * HEAD_DIM
    b_kv_width = (b_heads // B_GROUP) * HEAD_DIM
    assert seq_len % ATTN_TILE == 0 and seq_len % TOKEN_TILE == 0
    assert a_heads == b_heads and a_heads % 2 == 0
    row = lambda v: v.reshape(1, -1).astype(F32)

    h = x.reshape(bsz * seq_len, d)
    for layer in range(depth):
        i = layer // 2
        w_up = mlp_w_up[layer].astype(BF16)
        w_down = mlp_w_down[layer].astype(BF16)
        g_mpre, g_mpost = row(mlp_norm_pre[layer]), row(mlp_norm_post[layer])
        if layer % 2 == 0:
            seg = a_width
            q_segs = (0, 3)
            assert b_width == seg and (2 * b_kv_width) % (2 * LANES) == 0
            proj = _inproj(h, row(e_norm_pre[i]), e_w_in[i].astype(BF16), q_segs, seg,
                           3 * a_width + b_width)
            bias_a, bias_b = _bias_tables(e_relpos_a[i], t5_table)
            oa, ob = _attention(proj, e_sink_b[i].astype(F32) * LOG2E, bias_a, bias_b,
                                seq_len, a_width, b_width, b_kv_width)
            w_out = e_w_out[i].astype(BF16)
            h = _outproj_mlp(oa, ob, h, w_out[:a_width], w_out[a_width:], row(e_norm_post[i]),
                             g_mpre, w_up, w_down, g_mpost)
        else:
            h = _pool_mlp(h, seq_len, row(o_norm_pre[i]), o_pool_w[i].astype(BF16),
                          row(o_pool_scale[i]), row(o_norm_post[i]),
                          g_mpre, w_up, w_down, g_mpost)
    return h.reshape(bsz, seq_len, d)
```

```python
import functools
import math

import jax
import jax.numpy as jnp
import numpy as np
from jax import lax
from jax.experimental import pallas as pl
from jax.experimental.pallas import tpu as pltpu

CHUNK = 64
HEAD_DIM = 64
A_PREV_CHUNKS = 8
A_MAX_REL = 128
B_GROUP = 4
B_PREV_CHUNKS = 2
T5_BUCKETS = 32
T5_MAX_DIST = 128
POOL_WINDOWS = (2, 4, 8, 16)
RMS_EPS = 1e-6
NEG_INF = -1e30
LOG2E = math.log2(math.e)

LANES = 128
V7X_VMEM_BYTES = 64 * 1024 * 1024

TOKEN_TILE = 512
INPROJ_TILE = 1024
OUTPROJ_TILE = 512
SUB_ROWS = 256
ATTN_TILE = 512
PAIR = 2 * CHUNK
PAIRS_PER_TILE = ATTN_TILE // PAIR
A_BAND = (A_PREV_CHUNKS + 2) * CHUNK
B_BAND = (B_PREV_CHUNKS + 2) * CHUNK
ATTN_LOOKAHEAD = 3
POOL_HALO = 16
FF_CHUNK = 1024
VMEM_LIMIT = 56 * 1024 * 1024

BF16 = jnp.bfloat16
F32 = jnp.float32


def _rms(x, g):
    ms = jnp.mean(x * x, axis=-1, keepdims=True)
    return (x * lax.rsqrt(ms + RMS_EPS)) * g


def _dot(a, b):
    return jnp.dot(a, b, preferred_element_type=F32)


def _dot_nt(a, b):
    return lax.dot_general(a, b, (((1,), (1,)), ((), ())), preferred_element_type=F32)


def _resident(shape):
    nd = len(shape)
    return pl.BlockSpec(shape, lambda *_: (0,) * nd, pipeline_mode=pl.Buffered(1))


def _inproj_kernel(x_ref, g_ref, w_ref, o_ref, *, seg, q_segs, q_scale, kv_b0):
    lo_half = lax.broadcasted_iota(jnp.int32, (1, LANES), 1) < HEAD_DIM
    for r in range(x_ref.shape[0] // SUB_ROWS):
        rows = slice(r * SUB_ROWS, (r + 1) * SUB_ROWS)
        xb = _rms(x_ref[rows, :], g_ref[...]).astype(BF16)
        for s in range(kv_b0 // seg):
            y = _dot(xb, w_ref[:, s * seg:(s + 1) * seg])
            if s in q_segs:
                y = y * q_scale
            o_ref[rows, s * seg:(s + 1) * seg] = y.astype(BF16)
        y = _dot(xb, w_ref[:, kv_b0:])
        for j in range(y.shape[1] // LANES):
            two = y[:, j * LANES:(j + 1) * LANES]
            swapped = pltpu.roll(two, HEAD_DIM, axis=1)
            c0 = kv_b0 + 2 * j * LANES
            o_ref[rows, c0:c0 + LANES] = jnp.where(lo_half, two, swapped).astype(BF16)
            o_ref[rows, c0 + LANES:c0 + 2 * LANES] = jnp.where(lo_half, swapped, two).astype(BF16)


def _inproj(x2, g, w_in, q_segs, seg, kv_b0):
    n, d = x2.shape
    out_cols = kv_b0 + 2 * (w_in.shape[1] - kv_b0)
    assert n % INPROJ_TILE == 0 and kv_b0 % seg == 0
    kern = functools.partial(_inproj_kernel, seg=seg, q_segs=q_segs,
                             q_scale=HEAD_DIM ** -0.5 * LOG2E, kv_b0=kv_b0)
    return pl.pallas_call(
        kern,
        grid=(n // INPROJ_TILE,),
        in_specs=[
            pl.BlockSpec((INPROJ_TILE, d), lambda i: (i, 0)),
            _resident((1, d)),
            _resident(w_in.shape),
        ],
        out_specs=pl.BlockSpec((INPROJ_TILE, out_cols), lambda i: (i, 0)),
        out_shape=jax.ShapeDtypeStruct((n, out_cols), BF16),
        compiler_params=pltpu.CompilerParams(
            dimension_semantics=("arbitrary",), vmem_limit_bytes=VMEM_LIMIT),
        name="inproj",
    )(x2, g, w_in)


def _softmax_pv(s, v1, sink):
    m = jnp.max(s, axis=-1, keepdims=True)
    if sink is not None:
        m = jnp.maximum(m, sink)
    pb = jnp.exp2((s - m).astype(BF16))
    if isinstance(v1, tuple):
        w1 = v1[0].shape[0]
        o = _dot(pb[:, :w1], v1[0]) + _dot(pb[:, w1:], v1[1])
    else:
        o = _dot(pb, v1)
    return o, (None if sink is None else jnp.exp2(sink - m))


def _attn_kernel(sink_ref, qa_ref, kap_ref, kac_ref, vap_ref, vac_ref,
                 qb_ref, kbp_ref, kbc_ref, vbp_ref, vbc_ref, biasa_ref, biasb_ref,
                 oa_ref, ob_ref, *, tiles_per_seq, pairs_per_kv):
    i = pl.program_id(0)
    first = (i % tiles_per_seq) == 0
    lane = lax.broadcasted_iota(jnp.int32, (1, LANES), 1)
    lo_half = lane < HEAD_DIM
    zero = jnp.zeros((), BF16)
    n_head_pairs = qa_ref.shape[1] // LANES

    def lanes(hp):
        return slice(hp * LANES, (hp + 1) * LANES)

    def head_q(q_ref, r0, hp, e):
        q2 = q_ref[r0:r0 + PAIR, lanes(hp)]
        return jnp.where(lo_half if e == 0 else ~lo_half, q2, zero)

    def logits_a(jj, hp, e):
        r0 = jj * PAIR
        qm = head_q(qa_ref, r0, hp, e)
        s = jnp.concatenate([_dot_nt(qm, kap_ref[r0:, lanes(hp)]),
                             _dot_nt(qm, kac_ref[:r0 + PAIR, lanes(hp)])], axis=1)
        return s + biasa_ref[jnp.where(first, jj, PAIRS_PER_TILE), 2 * hp + e]

    one = jnp.ones((), BF16)
    v1_cache = {}

    def v_ones(v_ref, blk, e):
        key = (id(v_ref), blk, e)
        if key not in v1_cache:
            v1_cache[key] = jnp.where(lo_half if e == 0 else ~lo_half, v_ref[:, lanes(blk)], one)
        return v1_cache[key]

    def finish_a(jj, hp, e, s):
        r0 = jj * PAIR
        return _softmax_pv(s, (v_ones(vap_ref, hp, e)[r0:], v_ones(vac_ref, hp, e)[:r0 + PAIR]),
                           None)

    def band_b(prev_blk, cur_blk, jj):
        r0 = jj * PAIR
        if jj == 0:
            return jnp.concatenate([prev_blk[...], cur_blk[:PAIR]], axis=0)
        return cur_blk[r0 - PAIR:r0 + PAIR]

    def logits_b(jj, hp, e):
        kv = lanes(hp // pairs_per_kv)
        k_band = band_b(kbp_ref.at[:, kv], kbc_ref.at[:, kv], jj)
        s = _dot_nt(head_q(qb_ref, jj * PAIR, hp, e), k_band)
        variant = jnp.where(first, 0, 1) if jj == 0 else 1
        return s + biasb_ref[variant, 2 * hp + e]

    def finish_b(jj, hp, e, s):
        kv = hp // pairs_per_kv
        return _softmax_pv(s, band_b(v_ones(vbp_ref, kv, e), v_ones(vbc_ref, kv, e), jj),
                           sink_ref[2 * hp + e])

    mixers = {"a": (logits_a, finish_a, oa_ref), "b": (logits_b, finish_b, ob_ref)}
    units = [(mx, jj, hp, e) for hp in range(n_head_pairs) for jj in range(PAIRS_PER_TILE)
             for mx in ("a", "b") for e in range(2)]
    pending, done = {}, {}
    for t in range(len(units) + ATTN_LOOKAHEAD):
        if t < len(units):
            mx, jj, hp, e = units[t]
            pending[t] = mixers[mx][0](jj, hp, e)
        if t >= ATTN_LOOKAHEAD:
            mx, jj, hp, e = units[t - ATTN_LOOKAHEAD]
            done[e] = mixers[mx][1](jj, hp, e, pending.pop(t - ATTN_LOOKAHEAD))
            if e == 1:
                (o0, x0), (o1, x1) = done[0], done[1]
                num = jnp.where(lo_half, o0, o1)
                den = pltpu.roll(jnp.where(lo_half, o1, o0), HEAD_DIM, axis=1)
                if x0 is not None:
                    den = den + jnp.where(lo_half, x0, x1)
                mixers[mx][2][jj * PAIR:(jj + 1) * PAIR, lanes(hp)] = (
                    num * (1.0 / den)).astype(BF16)


def _attention(proj, sink, bias_a, bias_b, seq_len, a_width, b_width, b_kv_width):
    n = proj.shape[0]
    tiles_per_seq = seq_len // ATTN_TILE
    kv_dup = 2 * b_kv_width
    assert a_width == b_width and kv_dup % LANES == 0 and a_width % kv_dup == 0
    sub = ATTN_TILE // PAIR
    qa, ka, va, qb = 0, 1, 2, 3
    kb = 4 * a_width // kv_dup
    vb = kb + 1

    def cur(width, c):
        return pl.BlockSpec((ATTN_TILE, width), lambda i: (i, c))

    def prev(c):
        return pl.BlockSpec((ATTN_TILE, a_width), lambda i: (jnp.maximum(i - 1, 0), c))

    def prev_pair(c):
        return pl.BlockSpec((PAIR, kv_dup), lambda i: (jnp.maximum(i * sub - 1, 0), c))

    kern = functools.partial(_attn_kernel, tiles_per_seq=tiles_per_seq,
                             pairs_per_kv=(b_width // LANES) // (kv_dup // LANES))
    return pl.pallas_call(
        kern,
        grid=(n // ATTN_TILE,),
        in_specs=[
            pl.BlockSpec(memory_space=pltpu.SMEM),
            cur(a_width, qa), prev(ka), cur(a_width, ka), prev(va), cur(a_width, va),
            cur(b_width, qb), prev_pair(kb), cur(kv_dup, kb), prev_pair(vb), cur(kv_dup, vb),
            _resident(bias_a.shape), _resident(bias_b.shape),
        ],
        out_specs=[cur(a_width, 0), cur(b_width, 0)],
        out_shape=[jax.ShapeDtypeStruct((n, a_width), BF16),
                   jax.ShapeDtypeStruct((n, b_width), BF16)],
        compiler_params=pltpu.CompilerParams(
            dimension_semantics=("arbitrary",), vmem_limit_bytes=VMEM_LIMIT),
        name="band_attention",
    )(sink, proj, proj, proj, proj, proj, proj, proj, proj, proj, proj, bias_a, bias_b)


def _mlp_rows(n_rows, early_fn, h1_fn, o_ref, gpre_ref, wup_ref, wdown_ref, gpost_ref,
              side_jobs=()):
    n_sub = n_rows // SUB_ROWS
    n_chunks = wup_ref.shape[1] // FF_CHUNK
    n_slots = n_sub * n_chunks * 2
    jobs = list(side_jobs)
    job_slot = {round((j + 0.5) * n_slots / len(jobs)): job for j, job in enumerate(jobs)}
    assert len(job_slot) == len(jobs) and all(0 < k <= n_slots for k in job_slot)
    rows = [slice(s * SUB_ROWS, (s + 1) * SUB_ROWS) for s in range(n_sub)]
    early = {0: early_fn(rows[0])} if early_fn else {}
    slot = 0
    for s in range(n_sub):
        if early_fn and s + 1 < n_sub:
            early[s + 1] = early_fn(rows[s + 1])
        h1 = h1_fn(rows[s], early.pop(s, None))
        hn = _rms(h1, gpre_ref[...]).astype(BF16)
        acc = None
        for c in range(n_chunks):
            ff = slice(c * FF_CHUNK, (c + 1) * FF_CHUNK)
            u = jnp.maximum(_dot(hn, wup_ref[:, ff]), 0.0)
            slot += 1
            if slot in job_slot:
                job_slot[slot]()
            part = _dot((u * u).astype(BF16), wdown_ref[ff, :])
            slot += 1
            if slot in job_slot:
                job_slot[slot]()
            acc = part if acc is None else acc + part
        o_ref[rows[s], :] = h1 + _rms(acc, gpost_ref[...])


def _outproj_mlp_kernel(oa_ref, ob_ref, x_ref, woa_ref, wob_ref, gpost_ref,
                        gpre_ref, wup_ref, wdown_ref, gmpost_ref, o_ref):
    def out_proj(rows):
        return _dot(oa_ref[rows, :], woa_ref[...]) + _dot(ob_ref[rows, :], wob_ref[...])

    def h1_fn(rows, y):
        return x_ref[rows, :] + _rms(y, gpost_ref[...])

    _mlp_rows(x_ref.shape[0], out_proj, h1_fn, o_ref, gpre_ref, wup_ref, wdown_ref, gmpost_ref)


def _outproj_mlp(oa, ob, x2, w_oa, w_ob, g_post, g_mpre, w_up, w_down, g_mpost):
    n, d = x2.shape
    assert n % OUTPROJ_TILE == 0
    row = lambda w: pl.BlockSpec((OUTPROJ_TILE, w), lambda i: (i, 0))
    return pl.pallas_call(
        _outproj_mlp_kernel,
        grid=(n // OUTPROJ_TILE,),
        in_specs=[row(oa.shape[1]), row(ob.shape[1]), row(d),
                  _resident(w_oa.shape), _resident(w_ob.shape), _resident((1, d)),
                  _resident((1, d)), _resident(w_up.shape), _resident(w_down.shape),
                  _resident((1, d))],
        out_specs=row(d),
        out_shape=jax.ShapeDtypeStruct((n, d), F32),
        compiler_params=pltpu.CompilerParams(
            dimension_semantics=("arbitrary",), vmem_limit_bytes=VMEM_LIMIT),
        name="outproj_mlp",
    )(oa, ob, x2, w_oa, w_ob, g_post, g_mpre, w_up, w_down, g_mpost)


def _pool_mixer_jobs(h_ref, halo_fn, g_ref, pw_ref, pscale_ref, t_in_seq, y_ref):
    tm = h_ref.shape[0]
    pc = pw_ref.shape[1]
    st = {}

    def prepare():
        st["hn"] = _rms(h_ref[...], g_ref[...])
        st["halo"] = halo_fn()
        pos1 = t_in_seq * tm + 1 + lax.broadcasted_iota(jnp.int32, (tm, 1), 0)
        st["pos1"] = pos1
        st["inv_pos1"] = 1.0 / pos1.astype(F32)

    def group(gi, w):
        cols = slice(gi * pc, (gi + 1) * pc)
        hn = st["hn"][:, cols]
        wsum = jnp.concatenate([st["halo"][:, cols], hn], axis=0)
        k = 1
        while k < w:
            wsum = wsum + pltpu.roll(wsum, k, axis=0)
            k *= 2
        inv_cnt = jnp.where(st["pos1"] < w, st["inv_pos1"], 1.0 / w)
        d = (wsum[POOL_HALO:, :] * inv_cnt - hn).astype(BF16)
        y_ref[:, cols] = _dot(d, pw_ref[gi]) * pscale_ref[:, cols]

    return [prepare] + [functools.partial(group, gi, w) for gi, w in enumerate(POOL_WINDOWS)]


def _pool_mlp_kernel(h_ref, hnext_ref, gpre_ref, pw_ref, pscale_ref, gpost_ref,
                     gmpre_ref, wup_ref, wdown_ref, gmpost_ref, o_ref, y_ref, *, tiles_per_seq):
    i = pl.program_id(0)
    tm, dm = h_ref.shape
    @pl.when(i == 0)
    def _():
        for job in _pool_mixer_jobs(h_ref, lambda: jnp.zeros((POOL_HALO, dm), F32),
                                    gpre_ref, pw_ref, pscale_ref, 0, y_ref):
            job()

    y = y_ref[...]

    def h1_fn(rows, _):
        return h_ref[rows, :] + _rms(y[rows, :], gpost_ref[...])

    t_next = (i + 1) % tiles_per_seq

    def next_halo():
        keep = jnp.where(t_next == 0, 0.0, 1.0)
        return _rms(h_ref[tm - POOL_HALO:, :], gpre_ref[...]) * keep

    jobs = _pool_mixer_jobs(hnext_ref, next_halo, gpre_ref, pw_ref, pscale_ref, t_next, y_ref)
    _mlp_rows(tm, None, h1_fn, o_ref, gmpre_ref, wup_ref, wdown_ref, gmpost_ref, jobs)


def _pool_mlp(h2, seq_len, g_pre, pool_w, pool_scale, g_post, g_mpre, w_up, w_down, g_mpost):
    n, d = h2.shape
    tiles_per_seq = seq_len // TOKEN_TILE
    n_tiles = n // TOKEN_TILE
    row = pl.BlockSpec((TOKEN_TILE, d), lambda i: (i, 0))
    nxt = pl.BlockSpec((TOKEN_TILE, d), lambda i: (jnp.minimum(i + 1, n_tiles - 1), 0))
    kern = functools.partial(_pool_mlp_kernel, tiles_per_seq=tiles_per_seq)
    return pl.pallas_call(
        kern,
        grid=(n_tiles,),
        in_specs=[row, nxt, _resident((1, d)), _resident(pool_w.shape), _resident((1, d)),
                  _resident((1, d)), _resident((1, d)), _resident(w_up.shape),
                  _resident(w_down.shape), _resident((1, d))],
        out_specs=row,
        out_shape=jax.ShapeDtypeStruct((n, d), F32),
        scratch_shapes=[pltpu.VMEM((TOKEN_TILE, d), F32)],
        compiler_params=pltpu.CompilerParams(
            dimension_semantics=("arbitrary",), vmem_limit_bytes=VMEM_LIMIT),
        name="pool_mlp",
    )(h2, h2, g_pre, pool_w, pool_scale, g_post, g_mpre, w_up, w_down, g_mpost)


def _pair_band_index(n_prev):
    band = (n_prev + 2) * CHUNK
    r = np.arange(PAIR)[:, None]
    k = np.arange(band)[None, :]
    e, qi = r // CHUNK, r % CHUNK
    kc = k // CHUNK - e
    ok = (kc >= 0) & (kc <= n_prev)
    key_pos = kc * CHUNK + k % CHUNK - n_prev * CHUNK
    return qi - key_pos, ok


def _t5_bucket(rel_kq):
    nb = T5_BUCKETS // 2
    ret = (rel_kq > 0).astype(np.int32) * nb
    n = np.abs(rel_kq)
    max_exact = nb // 2
    large = max_exact + (np.log(np.maximum(n, 1) / max_exact)
                         / math.log(T5_MAX_DIST / max_exact) * (nb - max_exact)).astype(np.int32)
    large = np.minimum(large, nb - 1)
    return ret + np.where(n < max_exact, n, large)


def _toeplitz_bias(table, index_of_rel, n_prev):
    band = (n_prev + 2) * CHUNK
    period = band + PAIR
    delta = np.arange(period)
    delta = np.where(delta < band, delta, delta - period)
    diag = jnp.take(table, index_of_rel(n_prev * CHUNK - delta), axis=0).T
    skew = jnp.tile(diag, (1, PAIR))[:, :PAIR * (period - 1)].reshape(-1, PAIR, period - 1)
    bias = skew[:, :, :band]
    _, ok = _pair_band_index(n_prev)
    return jnp.where(ok[None], bias, NEG_INF).astype(F32)


def _bias_tables(relpos_a, t5_table):
    bias_a = _toeplitz_bias(
        relpos_a, lambda rel: np.clip(rel, -A_MAX_REL, A_MAX_REL) + A_MAX_REL, A_PREV_CHUNKS)
    bias_b = _toeplitz_bias(t5_table, lambda rel: _t5_bucket(-rel), B_PREV_CHUNKS)
    kc_a = np.arange(A_BAND) // CHUNK
    hide_a = np.zeros((PAIRS_PER_TILE + 1, 1, 1, A_BAND), bool)
    for jj in range(PAIRS_PER_TILE):
        hide_a[jj, 0, 0] = 2 * jj - A_PREV_CHUNKS + kc_a < 0
    kc_b = np.arange(B_BAND) // CHUNK
    hide_b = np.zeros((2, 1, 1, B_BAND), bool)
    hide_b[0, 0, 0] = kc_b - B_PREV_CHUNKS < 0
    bias_a = jnp.where(hide_a, NEG_INF, bias_a[None]) * LOG2E
    bias_b = jnp.where(hide_b, NEG_INF, bias_b[None]) * LOG2E
    return bias_a, bias_b


def kernel(x, t5_table, e_norm_pre, e_norm_post, e_w_in, e_w_out, e_relpos_a, e_sink_b,
           o_norm_pre, o_norm_post, o_pool_w, o_pool_scale,
           mlp_norm_pre, mlp_norm_post, mlp_w_up, mlp_w_down):
    bsz, seq_len, d = x.shape
    depth = mlp_w_up.shape[0]
    a_heads = e_relpos_a.shape[2]
    b_heads = e_sink_b.shape[1]
    a_width, b_width = a_heads * HEAD_DIM, b_heads * HEAD_DIM
    b_kv_width = (b_heads // B_GROUP) * HEAD_DIM
    assert seq_len % ATTN_TILE == 0 and seq_len % TOKEN_TILE == 0
    assert a_heads == b_heads and a_heads % 2 == 0
    row = lambda v: v.reshape(1, -1).astype(F32)

    h = x.reshape(bsz * seq_len, d)
    for layer in range(depth):
        i = layer // 2
        w_up = mlp_w_up[layer].astype(BF16)
        w_down = mlp_w_down[layer].astype(BF16)
        g_mpre, g_mpost = row(mlp_norm_pre[layer]), row(mlp_norm_post[layer])
        if layer % 2 == 0:
            seg = a_width
            q_segs = (0, 3)
            assert b_width == seg and (2 * b_kv_width) % (2 * LANES) == 0
            proj = _inproj(h, row(e_norm_pre[i]), e_w_in[i].astype(BF16), q_segs, seg,
                           3 * a_width + b_width)
            bias_a, bias_b = _bias_tables(e_relpos_a[i], t5_table)
            oa, ob = _attention(proj, e_sink_b[i].astype(F32) * LOG2E, bias_a, bias_b,
                                seq_len, a_width, b_width, b_kv_width)
            w_out = e_w_out[i].astype(BF16)
            h = _outproj_mlp(oa, ob, h, w_out[:a_width], w_out[a_width:], row(e_norm_post[i]),
                             g_mpre, w_up, w_down, g_mpost)
        else:
            h = _pool_mlp(h, seq_len, row(o_norm_pre[i]), o_pool_w[i].astype(BF16),
                          row(o_pool_scale[i]), row(o_norm_post[i]),
                          g_mpre, w_up, w_down, g_mpost)
    return h.reshape(bsz, seq_len, d)
```

```python
import functools
import math

import jax
import jax.numpy as jnp
import numpy as np
from jax import lax
from jax.experimental import pallas as pl
from jax.experimental.pallas import tpu as pltpu

CHUNK = 64
HEAD_DIM = 64
A_PREV_CHUNKS = 8
A_MAX_REL = 128
B_GROUP = 4
B_PREV_CHUNKS = 2
T5_BUCKETS = 32
T5_MAX_DIST = 128
POOL_WINDOWS = (2, 4, 8, 16)
RMS_EPS = 1e-6
NEG_INF = -1e30
LOG2E = math.log2(math.e)

LANES = 128
V7X_VMEM_BYTES = 64 * 1024 * 1024

TOKEN_TILE = 512
INPROJ_TILE = 1024
OUTPROJ_TILE = 512
SUB_ROWS = 256
ATTN_TILE = 512
PAIR = 2 * CHUNK
PAIRS_PER_TILE = ATTN_TILE // PAIR
A_BAND = (A_PREV_CHUNKS + 2) * CHUNK
B_BAND = (B_PREV_CHUNKS + 2) * CHUNK
ATTN_LOOKAHEAD = 2
POOL_HALO = 16
FF_CHUNK = 1024
VMEM_LIMIT = 56 * 1024 * 1024

BF16 = jnp.bfloat16
F32 = jnp.float32


def _rms(x, g):
    ms = jnp.mean(x * x, axis=-1, keepdims=True)
    return (x * lax.rsqrt(ms + RMS_EPS)) * g


def _dot(a, b):
    return jnp.dot(a, b, preferred_element_type=F32)


def _dot_nt(a, b):
    return lax.dot_general(a, b, (((1,), (1,)), ((), ())), preferred_element_type=F32)


def _resident(shape):
    nd = len(shape)
    return pl.BlockSpec(shape, lambda *_: (0,) * nd, pipeline_mode=pl.Buffered(1))


def _inproj_kernel(x_ref, g_ref, w_ref, o_ref, *, seg, q_segs, q_scale, kv_b0):
    lo_half = lax.broadcasted_iota(jnp.int32, (1, LANES), 1) < HEAD_DIM
    for r in range(x_ref.shape[0] // SUB_ROWS):
        rows = slice(r * SUB_ROWS, (r + 1) * SUB_ROWS)
        xb = _rms(x_ref[rows, :], g_ref[...]).astype(BF16)
        for s in range(kv_b0 // seg):
            y = _dot(xb, w_ref[:, s * seg:(s + 1) * seg])
            if s in q_segs:
                y = y * q_scale
            o_ref[rows, s * seg:(s + 1) * seg] = y.astype(BF16)
        y = _dot(xb, w_ref[:, kv_b0:])
        for j in range(y.shape[1] // LANES):
            two = y[:, j * LANES:(j + 1) * LANES]
            swapped = pltpu.roll(two, HEAD_DIM, axis=1)
            c0 = kv_b0 + 2 * j * LANES
            o_ref[rows, c0:c0 + LANES] = jnp.where(lo_half, two, swapped).astype(BF16)
            o_ref[rows, c0 + LANES:c0 + 2 * LANES] = jnp.where(lo_half, swapped, two).astype(BF16)


def _inproj(x2, g, w_in, q_segs, seg, kv_b0):
    n, d = x2.shape
    out_cols = kv_b0 + 2 * (w_in.shape[1] - kv_b0)
    assert n % INPROJ_TILE == 0 and kv_b0 % seg == 0
    kern = functools.partial(_inproj_kernel, seg=seg, q_segs=q_segs,
                             q_scale=HEAD_DIM ** -0.5 * LOG2E, kv_b0=kv_b0)
    return pl.pallas_call(
        kern,
        grid=(n // INPROJ_TILE,),
        in_specs=[
            pl.BlockSpec((INPROJ_TILE, d), lambda i: (i, 0)),
            _resident((1, d)),
            _resident(w_in.shape),
        ],
        out_specs=pl.BlockSpec((INPROJ_TILE, out_cols), lambda i: (i, 0)),
        out_shape=jax.ShapeDtypeStruct((n, out_cols), BF16),
        compiler_params=pltpu.CompilerParams(
            dimension_semantics=("arbitrary",), vmem_limit_bytes=VMEM_LIMIT),
        name="inproj",
    )(x2, g, w_in)


def _softmax_pv(s, v_ones, sink):
    m = jnp.max(s, axis=-1, keepdims=True)
    if sink is not None:
        m = jnp.maximum(m, sink)
    pb = jnp.exp2((s - m).astype(BF16))
    if isinstance(v_ones, tuple):
        w1 = v_ones[0].shape[0]
        o = _dot(pb[:, :w1], v_ones[0]) + _dot(pb[:, w1:], v_ones[1])
    else:
        o = _dot(pb, v_ones)
    den = o[:, LANES:]
    if sink is not None:
        den = den + jnp.exp2(sink - m)
    out = o[:, :LANES] * (1.0 / den)
    half = out.shape[0] // 2
    lo_half = lax.broadcasted_iota(jnp.int32, (1, LANES), 1) < HEAD_DIM
    return jnp.where(lo_half, out[:half], out[half:])


def _attn_kernel(sink_ref, qa_ref, kap_ref, kac_ref, vap_ref, vac_ref,
                 qb_ref, kbp_ref, kbc_ref, vbp_ref, vbc_ref, biasa_ref, biasb_ref,
                 oa_ref, ob_ref, *, tiles_per_seq, pairs_per_kv):
    i = pl.program_id(0)
    first = (i % tiles_per_seq) == 0
    lo_half = lax.broadcasted_iota(jnp.int32, (1, LANES), 1) < HEAD_DIM
    zero = jnp.zeros((), BF16)
    n_head_pairs = qa_ref.shape[1] // LANES

    def lanes(hp):
        return slice(hp * LANES, (hp + 1) * LANES)

    def stacked_q(q_ref, r0, hp):
        q2 = q_ref[r0:r0 + PAIR, lanes(hp)]
        return jnp.concatenate([jnp.where(lo_half, q2, zero), jnp.where(lo_half, zero, q2)],
                               axis=0)

    def with_ones(v):
        return jnp.concatenate([v, jnp.ones(v.shape, BF16)], axis=1)

    def logits_a(jj, hp):
        r0 = jj * PAIR
        q = stacked_q(qa_ref, r0, hp)
        s = jnp.concatenate([_dot_nt(q, kap_ref[r0:, lanes(hp)]),
                             _dot_nt(q, kac_ref[:r0 + PAIR, lanes(hp)])], axis=1)
        bias = biasa_ref[jnp.where(first, jj, PAIRS_PER_TILE), 2 * hp:2 * hp + 2]
        return s + bias.reshape(s.shape)

    def finish_a(jj, hp, s):
        r0 = jj * PAIR
        return _softmax_pv(s, (with_ones(vap_ref[r0:, lanes(hp)]),
                               with_ones(vac_ref[:r0 + PAIR, lanes(hp)])), None)

    def band_b(prev_ref, cur_ref, jj, hp):
        r0 = jj * PAIR
        kv = lanes(hp // pairs_per_kv)
        if jj == 0:
            return jnp.concatenate([prev_ref[:, kv], cur_ref[:PAIR, kv]], axis=0)
        return cur_ref[r0 - PAIR:r0 + PAIR, kv]

    def logits_b(jj, hp):
        s = _dot_nt(stacked_q(qb_ref, jj * PAIR, hp), band_b(kbp_ref, kbc_ref, jj, hp))
        variant = jnp.where(first, 0, 1) if jj == 0 else 1
        return s + biasb_ref[variant, 2 * hp:2 * hp + 2].reshape(s.shape)

    def finish_b(jj, hp, s):
        sink = jnp.concatenate(
            [jnp.full((PAIR, 1), sink_ref[2 * hp + e], F32) for e in range(2)], axis=0)
        return _softmax_pv(s, with_ones(band_b(vbp_ref, vbc_ref, jj, hp)), sink)

    mixers = {"a": (logits_a, finish_a, oa_ref), "b": (logits_b, finish_b, ob_ref)}
    units = [(mx, jj, hp) for hp in range(n_head_pairs) for jj in range(PAIRS_PER_TILE)
             for mx in ("a", "b")]
    pending = {}
    for t in range(len(units) + ATTN_LOOKAHEAD):
        if t < len(units):
            mx, jj, hp = units[t]
            pending[t] = mixers[mx][0](jj, hp)
        if t >= ATTN_LOOKAHEAD:
            mx, jj, hp = units[t - ATTN_LOOKAHEAD]
            out = mixers[mx][1](jj, hp, pending.pop(t - ATTN_LOOKAHEAD))
            mixers[mx][2][jj * PAIR:(jj + 1) * PAIR, lanes(hp)] = out.astype(BF16)


def _attention(proj, sink, bias_a, bias_b, seq_len, a_width, b_width, b_kv_width):
    n = proj.shape[0]
    tiles_per_seq = seq_len // ATTN_TILE
    kv_dup = 2 * b_kv_width
    assert a_width == b_width and kv_dup % LANES == 0 and a_width % kv_dup == 0
    sub = ATTN_TILE // PAIR
    qa, ka, va, qb = 0, 1, 2, 3
    kb = 4 * a_width // kv_dup
    vb = kb + 1

    def cur(width, c):
        return pl.BlockSpec((ATTN_TILE, width), lambda i: (i, c))

    def prev(c):
        return pl.BlockSpec((ATTN_TILE, a_width), lambda i: (jnp.maximum(i - 1, 0), c))

    def prev_pair(c):
        return pl.BlockSpec((PAIR, kv_dup), lambda i: (jnp.maximum(i * sub - 1, 0), c))

    kern = functools.partial(_attn_kernel, tiles_per_seq=tiles_per_seq,
                             pairs_per_kv=(b_width // LANES) // (kv_dup // LANES))
    return pl.pallas_call(
        kern,
        grid=(n // ATTN_TILE,),
        in_specs=[
            pl.BlockSpec(memory_space=pltpu.SMEM),
            cur(a_width, qa), prev(ka), cur(a_width, ka), prev(va), cur(a_width, va),
            cur(b_width, qb), prev_pair(kb), cur(kv_dup, kb), prev_pair(vb), cur(kv_dup, vb),
            _resident(bias_a.shape), _resident(bias_b.shape),
        ],
        out_specs=[cur(a_width, 0), cur(b_width, 0)],
        out_shape=[jax.ShapeDtypeStruct((n, a_width), BF16),
                   jax.ShapeDtypeStruct((n, b_width), BF16)],
        compiler_params=pltpu.CompilerParams(
            dimension_semantics=("arbitrary",), vmem_limit_bytes=VMEM_LIMIT),
        name="band_attention",
    )(sink, proj, proj, proj, proj, proj, proj, proj, proj, proj, proj, bias_a, bias_b)


def _mlp_rows(n_rows, early_fn, h1_fn, o_ref, gpre_ref, wup_ref, wdown_ref, gpost_ref,
              side_jobs=()):
    n_sub = n_rows // SUB_ROWS
    n_chunks = wup_ref.shape[1] // FF_CHUNK
    n_slots = n_sub * n_chunks * 2
    jobs = list(side_jobs)
    job_slot = {round((j + 0.5) * n_slots / len(jobs)): job for j, job in enumerate(jobs)}
    assert len(job_slot) == len(jobs) and all(0 < k <= n_slots for k in job_slot)
    rows = [slice(s * SUB_ROWS, (s + 1) * SUB_ROWS) for s in range(n_sub)]
    early = {0: early_fn(rows[0])} if early_fn else {}
    slot = 0
    for s in range(n_sub):
        if early_fn and s + 1 < n_sub:
            early[s + 1] = early_fn(rows[s + 1])
        h1 = h1_fn(rows[s], early.pop(s, None))
        hn = _rms(h1, gpre_ref[...]).astype(BF16)
        acc = None
        for c in range(n_chunks):
            ff = slice(c * FF_CHUNK, (c + 1) * FF_CHUNK)
            u = jnp.maximum(_dot(hn, wup_ref[:, ff]), 0.0)
            slot += 1
            if slot in job_slot:
                job_slot[slot]()
            part = _dot((u * u).astype(BF16), wdown_ref[ff, :])
            slot += 1
            if slot in job_slot:
                job_slot[slot]()
            acc = part if acc is None else acc + part
        o_ref[rows[s], :] = h1 + _rms(acc, gpost_ref[...])


def _outproj_mlp_kernel(oa_ref, ob_ref, x_ref, woa_ref, wob_ref, gpost_ref,
                        gpre_ref, wup_ref, wdown_ref, gmpost_ref, o_ref):
    def out_proj(rows):
        return _dot(oa_ref[rows, :], woa_ref[...]) + _dot(ob_ref[rows, :], wob_ref[...])

    def h1_fn(rows, y):
        return x_ref[rows, :] + _rms(y, gpost_ref[...])

    _mlp_rows(x_ref.shape[0], out_proj, h1_fn, o_ref, gpre_ref, wup_ref, wdown_ref, gmpost_ref)


def _outproj_mlp(oa, ob, x2, w_oa, w_ob, g_post, g_mpre, w_up, w_down, g_mpost):
    n, d = x2.shape
    assert n % OUTPROJ_TILE == 0
    row = lambda w: pl.BlockSpec((OUTPROJ_TILE, w), lambda i: (i, 0))
    return pl.pallas_call(
        _outproj_mlp_kernel,
        grid=(n // OUTPROJ_TILE,),
        in_specs=[row(oa.shape[1]), row(ob.shape[1]), row(d),
                  _resident(w_oa.shape), _resident(w_ob.shape), _resident((1, d)),
                  _resident((1, d)), _resident(w_up.shape), _resident(w_down.shape),
                  _resident((1, d))],
        out_specs=row(d),
        out_shape=jax.ShapeDtypeStruct((n, d), F32),
        compiler_params=pltpu.CompilerParams(
            dimension_semantics=("arbitrary",), vmem_limit_bytes=VMEM_LIMIT),
        name="outproj_mlp",
    )(oa, ob, x2, w_oa, w_ob, g_post, g_mpre, w_up, w_down, g_mpost)


def _pool_mixer_jobs(h_ref, halo_fn, g_ref, pw_ref, pscale_ref, t_in_seq, y_ref):
    tm = h_ref.shape[0]
    pc = pw_ref.shape[1]
    st = {}

    def prepare():
        st["hn"] = _rms(h_ref[...], g_ref[...])
        st["halo"] = halo_fn()
        pos1 = t_in_seq * tm + 1 + lax.broadcasted_iota(jnp.int32, (tm, 1), 0)
        st["pos1"] = pos1
        st["inv_pos1"] = 1.0 / pos1.astype(F32)

    def group(gi, w):
        cols = slice(gi * pc, (gi + 1) * pc)
        hn = st["hn"][:, cols]
        wsum = jnp.concatenate([st["halo"][:, cols], hn], axis=0)
        k = 1
        while k < w:
            wsum = wsum + pltpu.roll(wsum, k, axis=0)
            k *= 2
        inv_cnt = jnp.where(st["pos1"] < w, st["inv_pos1"], 1.0 / w)
        d = (wsum[POOL_HALO:, :] * inv_cnt - hn).astype(BF16)
        y_ref[:, cols] = _dot(d, pw_ref[gi]) * pscale_ref[:, cols]

    return [prepare] + [functools.partial(group, gi, w) for gi, w in enumerate(POOL_WINDOWS)]


def _pool_mlp_kernel(h_ref, hnext_ref, gpre_ref, pw_ref, pscale_ref, gpost_ref,
                     gmpre_ref, wup_ref, wdown_ref, gmpost_ref, o_ref, y_ref, *, tiles_per_seq):
    i = pl.program_id(0)
    tm, dm = h_ref.shape
    @pl.when(i == 0)
    def _():
        for job in _pool_mixer_jobs(h_ref, lambda: jnp.zeros((POOL_HALO, dm), F32),
                                    gpre_ref, pw_ref, pscale_ref, 0, y_ref):
            job()

    y = y_ref[...]

    def h1_fn(rows, _):
        return h_ref[rows, :] + _rms(y[rows, :], gpost_ref[...])

    t_next = (i + 1) % tiles_per_seq

    def next_halo():
        keep = jnp.where(t_next == 0, 0.0, 1.0)
        return _rms(h_ref[tm - POOL_HALO:, :], gpre_ref[...]) * keep

    jobs = _pool_mixer_jobs(hnext_ref, next_halo, gpre_ref, pw_ref, pscale_ref, t_next, y_ref)
    _mlp_rows(tm, None, h1_fn, o_ref, gmpre_ref, wup_ref, wdown_ref, gmpost_ref, jobs)


def _pool_mlp(h2, seq_len, g_pre, pool_w, pool_scale, g_post, g_mpre, w_up, w_down, g_mpost):
    n, d = h2.shape
    tiles_per_seq = seq_len // TOKEN_TILE
    n_tiles = n // TOKEN_TILE
    row = pl.BlockSpec((TOKEN_TILE, d), lambda i: (i, 0))
    nxt = pl.BlockSpec((TOKEN_TILE, d), lambda i: (jnp.minimum(i + 1, n_tiles - 1), 0))
    kern = functools.partial(_pool_mlp_kernel, tiles_per_seq=tiles_per_seq)
    return pl.pallas_call(
        kern,
        grid=(n_tiles,),
        in_specs=[row, nxt, _resident((1, d)), _resident(pool_w.shape), _resident((1, d)),
                  _resident((1, d)), _resident((1, d)), _resident(w_up.shape),
                  _resident(w_down.shape), _resident((1, d))],
        out_specs=row,
        out_shape=jax.ShapeDtypeStruct((n, d), F32),
        scratch_shapes=[pltpu.VMEM((TOKEN_TILE, d), F32)],
        compiler_params=pltpu.CompilerParams(
            dimension_semantics=("arbitrary",), vmem_limit_bytes=VMEM_LIMIT),
        name="pool_mlp",
    )(h2, h2, g_pre, pool_w, pool_scale, g_post, g_mpre, w_up, w_down, g_mpost)


def _pair_band_index(n_prev):
    band = (n_prev + 2) * CHUNK
    r = np.arange(PAIR)[:, None]
    k = np.arange(band)[None, :]
    e, qi = r // CHUNK, r % CHUNK
    kc = k // CHUNK - e
    ok = (kc >= 0) & (kc <= n_prev)
    key_pos = kc * CHUNK + k % CHUNK - n_prev * CHUNK
    return qi - key_pos, ok


def _t5_bucket(rel_kq):
    nb = T5_BUCKETS // 2
    ret = (rel_kq > 0).astype(np.int32) * nb
    n = np.abs(rel_kq)
    max_exact = nb // 2
    large = max_exact + (np.log(np.maximum(n, 1) / max_exact)
                         / math.log(T5_MAX_DIST / max_exact) * (nb - max_exact)).astype(np.int32)
    large = np.minimum(large, nb - 1)
    return ret + np.where(n < max_exact, n, large)


def _toeplitz_bias(table, index_of_rel, n_prev):
    band = (n_prev + 2) * CHUNK
    period = band + PAIR
    delta = np.arange(period)
    delta = np.where(delta < band, delta, delta - period)
    diag = jnp.take(table, index_of_rel(n_prev * CHUNK - delta), axis=0).T
    skew = jnp.tile(diag, (1, PAIR))[:, :PAIR * (period - 1)].reshape(-1, PAIR, period - 1)
    bias = skew[:, :, :band]
    _, ok = _pair_band_index(n_prev)
    return jnp.where(ok[None], bias, NEG_INF).astype(F32)


def _bias_tables(relpos_a, t5_table):
    bias_a = _toeplitz_bias(
        relpos_a, lambda rel: np.clip(rel, -A_MAX_REL, A_MAX_REL) + A_MAX_REL, A_PREV_CHUNKS)
    bias_b = _toeplitz_bias(t5_table, lambda rel: _t5_bucket(-rel), B_PREV_CHUNKS)
    kc_a = np.arange(A_BAND) // CHUNK
    hide_a = np.zeros((PAIRS_PER_TILE + 1, 1, 1, A_BAND), bool)
    for jj in range(PAIRS_PER_TILE):
        hide_a[jj, 0, 0] = 2 * jj - A_PREV_CHUNKS + kc_a < 0
    kc_b = np.arange(B_BAND) // CHUNK
    hide_b = np.zeros((2, 1, 1, B_BAND), bool)
    hide_b[0, 0, 0] = kc_b - B_PREV_CHUNKS < 0
    bias_a = jnp.where(hide_a, NEG_INF, bias_a[None]) * LOG2E
    bias_b = jnp.where(hide_b, NEG_INF, bias_b[None]) * LOG2E
    return bias_a, bias_b


def kernel(x, t5_table, e_norm_pre, e_norm_post, e_w_in, e_w_out, e_relpos_a, e_sink_b,
           o_norm_pre, o_norm_post, o_pool_w, o_pool_scale,
           mlp_norm_pre, mlp_norm_post, mlp_w_up, mlp_w_down):
    bsz, seq_len, d = x.shape
    depth = mlp_w_up.shape[0]
    a_heads = e_relpos_a.shape[2]
    b_heads = e_sink_b.shape[1]
    a_width, b_width = a_heads * HEAD_DIM, b_heads * HEAD_DIM
    b_kv_width = (b_heads // B_GROUP) * HEAD_DIM
    assert seq_len % ATTN_TILE == 0 and seq_len % TOKEN_TILE == 0
    assert a_heads == b_heads and a_heads % 2 == 0
    row = lambda v: v.reshape(1, -1).astype(F32)

    h = x.reshape(bsz * seq_len, d)
    for layer in range(depth):
        i = layer // 2
        w_up = mlp_w_up[layer].astype(BF16)
        w_down = mlp_w_down[layer].astype(BF16)
        g_mpre, g_mpost = row(mlp_norm_pre[layer]), row(mlp_norm_post[layer])
        if layer % 2 == 0:
            seg = a_width
            q_segs = (0, 3)
            assert b_width == seg and (2 * b_kv_width) % (2 * LANES) == 0
            proj = _inproj(h, row(e_norm_pre[i]), e_w_in[i].astype(BF16), q_segs, seg,
                           3 * a_width + b_width)
            bias_a, bias_b = _bias_tables(e_relpos_a[i], t5_table)
            oa, ob = _attention(proj, e_sink_b[i].astype(F32) * LOG2E, bias_a, bias_b,
                                seq_len, a_width, b_width, b_kv_width)
            w_out = e_w_out[i].astype(BF16)
            h = _outproj_mlp(oa, ob, h, w_out[:a_width], w_out[a_width:], row(e_norm_post[i]),
                             g_mpre, w_up, w_down, g_mpost)
        else:
            h = _pool_mlp(h, seq_len, row(o_norm_pre[i]), o_pool_w[i].astype(BF16),
                          row(o_pool_scale[i]), row(o_norm_post[i]),
                          g_mpre, w_up, w_down, g_mpost)
    return h.reshape(bsz, seq_len, d)
```

```python
import functools
import math

import jax
import jax.numpy as jnp
import numpy as np
from jax import lax
from jax.experimental import pallas as pl
from jax.experimental.pallas import tpu as pltpu

CHUNK = 64
HEAD_DIM = 64
A_PREV_CHUNKS = 8
A_MAX_REL = 128
B_GROUP = 4
B_PREV_CHUNKS = 2
T5_BUCKETS = 32
T5_MAX_DIST = 128
POOL_WINDOWS = (2, 4, 8, 16)
RMS_EPS = 1e-6
NEG_INF = -1e30
LOG2E = math.log2(math.e)

LANES = 128
V7X_VMEM_BYTES = 64 * 1024 * 1024

TOKEN_TILE = 1024
INPROJ_TILE = 1024
OUTPROJ_TILE = 1024
SUB_ROWS = 512
ATTN_TILE = 512
PAIR = 2 * CHUNK
PAIRS_PER_TILE = ATTN_TILE // PAIR
A_BAND = (A_PREV_CHUNKS + 2) * CHUNK
B_BAND = (B_PREV_CHUNKS + 2) * CHUNK
ATTN_LOOKAHEAD = 2
POOL_HALO = 16
FF_CHUNK = 1024
VMEM_LIMIT = 56 * 1024 * 1024

BF16 = jnp.bfloat16
F32 = jnp.float32


def _rms(x, g):
    ms = jnp.mean(x * x, axis=-1, keepdims=True)
    return (x * lax.rsqrt(ms + RMS_EPS)) * g


def _dot(a, b):
    return jnp.dot(a, b, preferred_element_type=F32)


def _dot_nt(a, b):
    return lax.dot_general(a, b, (((1,), (1,)), ((), ())), preferred_element_type=F32)


def _resident(shape):
    nd = len(shape)
    return pl.BlockSpec(shape, lambda *_: (0,) * nd, pipeline_mode=pl.Buffered(1))


def _inproj_kernel(x_ref, g_ref, w_ref, o_ref, *, seg, q_segs, q_scale, kv_b0):
    lo_half = lax.broadcasted_iota(jnp.int32, (1, LANES), 1) < HEAD_DIM
    for r in range(x_ref.shape[0] // SUB_ROWS):
        rows = slice(r * SUB_ROWS, (r + 1) * SUB_ROWS)
        xb = _rms(x_ref[rows, :], g_ref[...]).astype(BF16)
        for s in range(kv_b0 // seg):
            y = _dot(xb, w_ref[:, s * seg:(s + 1) * seg])
            if s in q_segs:
                y = y * q_scale
            o_ref[rows, s * seg:(s + 1) * seg] = y.astype(BF16)
        y = _dot(xb, w_ref[:, kv_b0:])
        for j in range(y.shape[1] // LANES):
            two = y[:, j * LANES:(j + 1) * LANES]
            swapped = pltpu.roll(two, HEAD_DIM, axis=1)
            c0 = kv_b0 + 2 * j * LANES
            o_ref[rows, c0:c0 + LANES] = jnp.where(lo_half, two, swapped).astype(BF16)
            o_ref[rows, c0 + LANES:c0 + 2 * LANES] = jnp.where(lo_half, swapped, two).astype(BF16)


def _inproj(x2, g, w_in, q_segs, seg, kv_b0):
    n, d = x2.shape
    out_cols = kv_b0 + 2 * (w_in.shape[1] - kv_b0)
    assert n % INPROJ_TILE == 0 and kv_b0 % seg == 0
    kern = functools.partial(_inproj_kernel, seg=seg, q_segs=q_segs,
                             q_scale=HEAD_DIM ** -0.5 * LOG2E, kv_b0=kv_b0)
    return pl.pallas_call(
        kern,
        grid=(n // INPROJ_TILE,),
        in_specs=[
            pl.BlockSpec((INPROJ_TILE, d), lambda i: (i, 0)),
            _resident((1, d)),
            _resident(w_in.shape),
        ],
        out_specs=pl.BlockSpec((INPROJ_TILE, out_cols), lambda i: (i, 0)),
        out_shape=jax.ShapeDtypeStruct((n, out_cols), BF16),
        compiler_params=pltpu.CompilerParams(
            dimension_semantics=("arbitrary",), vmem_limit_bytes=VMEM_LIMIT),
        name="inproj",
    )(x2, g, w_in)


def _softmax_pv(s, v_ones, sink):
    m = jnp.max(s, axis=-1, keepdims=True)
    if sink is not None:
        m = jnp.maximum(m, sink)
    pb = jnp.exp2((s - m).astype(BF16))
    if isinstance(v_ones, tuple):
        w1 = v_ones[0].shape[0]
        o = _dot(pb[:, :w1], v_ones[0]) + _dot(pb[:, w1:], v_ones[1])
    else:
        o = _dot(pb, v_ones)
    den = o[:, LANES:]
    if sink is not None:
        den = den + jnp.exp2(sink - m)
    out = o[:, :LANES] * (1.0 / den)
    half = out.shape[0] // 2
    lo_half = lax.broadcasted_iota(jnp.int32, (1, LANES), 1) < HEAD_DIM
    return jnp.where(lo_half, out[:half], out[half:])


def _attn_kernel(sink_ref, qa_ref, kap_ref, kac_ref, vap_ref, vac_ref,
                 qb_ref, kbp_ref, kbc_ref, vbp_ref, vbc_ref, biasa_ref, biasb_ref,
                 oa_ref, ob_ref, *, tiles_per_seq, pairs_per_kv):
    i = pl.program_id(0)
    first = (i % tiles_per_seq) == 0
    lo_half = lax.broadcasted_iota(jnp.int32, (1, LANES), 1) < HEAD_DIM
    zero = jnp.zeros((), BF16)
    n_head_pairs = qa_ref.shape[1] // LANES

    def lanes(hp):
        return slice(hp * LANES, (hp + 1) * LANES)

    def stacked_q(q_ref, r0, hp):
        q2 = q_ref[r0:r0 + PAIR, lanes(hp)]
        return jnp.concatenate([jnp.where(lo_half, q2, zero), jnp.where(lo_half, zero, q2)],
                               axis=0)

    def with_ones(v):
        return jnp.concatenate([v, jnp.ones(v.shape, BF16)], axis=1)

    def logits_a(jj, hp):
        r0 = jj * PAIR
        q = stacked_q(qa_ref, r0, hp)
        s = jnp.concatenate([_dot_nt(q, kap_ref[r0:, lanes(hp)]),
                             _dot_nt(q, kac_ref[:r0 + PAIR, lanes(hp)])], axis=1)
        bias = biasa_ref[jnp.where(first, jj, PAIRS_PER_TILE), 2 * hp:2 * hp + 2]
        return s + bias.reshape(s.shape)

    def finish_a(jj, hp, s):
        r0 = jj * PAIR
        return _softmax_pv(s, (with_ones(vap_ref[r0:, lanes(hp)]),
                               with_ones(vac_ref[:r0 + PAIR, lanes(hp)])), None)

    def band_b(prev_ref, cur_ref, jj, hp):
        r0 = jj * PAIR
        kv = lanes(hp // pairs_per_kv)
        if jj == 0:
            return jnp.concatenate([prev_ref[:, kv], cur_ref[:PAIR, kv]], axis=0)
        return cur_ref[r0 - PAIR:r0 + PAIR, kv]

    def logits_b(jj, hp):
        s = _dot_nt(stacked_q(qb_ref, jj * PAIR, hp), band_b(kbp_ref, kbc_ref, jj, hp))
        variant = jnp.where(first, 0, 1) if jj == 0 else 1
        return s + biasb_ref[variant, 2 * hp:2 * hp + 2].reshape(s.shape)

    def finish_b(jj, hp, s):
        sink = jnp.concatenate(
            [jnp.full((PAIR, 1), sink_ref[2 * hp + e], F32) for e in range(2)], axis=0)
        return _softmax_pv(s, with_ones(band_b(vbp_ref, vbc_ref, jj, hp)), sink)

    mixers = {"a": (logits_a, finish_a, oa_ref), "b": (logits_b, finish_b, ob_ref)}
    units = [(mx, jj, hp) for hp in range(n_head_pairs) for jj in range(PAIRS_PER_TILE)
             for mx in ("a", "b")]
    pending = {}
    for t in range(len(units) + ATTN_LOOKAHEAD):
        if t < len(units):
            mx, jj, hp = units[t]
            pending[t] = mixers[mx][0](jj, hp)
        if t >= ATTN_LOOKAHEAD:
            mx, jj, hp = units[t - ATTN_LOOKAHEAD]
            out = mixers[mx][1](jj, hp, pending.pop(t - ATTN_LOOKAHEAD))
            mixers[mx][2][jj * PAIR:(jj + 1) * PAIR, lanes(hp)] = out.astype(BF16)


def _attention(proj, sink, bias_a, bias_b, seq_len, a_width, b_width, b_kv_width):
    n = proj.shape[0]
    tiles_per_seq = seq_len // ATTN_TILE
    kv_dup = 2 * b_kv_width
    assert a_width == b_width and kv_dup % LANES == 0 and a_width % kv_dup == 0
    sub = ATTN_TILE // PAIR
    qa, ka, va, qb = 0, 1, 2, 3
    kb = 4 * a_width // kv_dup
    vb = kb + 1

    def cur(width, c):
        return pl.BlockSpec((ATTN_TILE, width), lambda i: (i, c))

    def prev(c):
        return pl.BlockSpec((ATTN_TILE, a_width), lambda i: (jnp.maximum(i - 1, 0), c))

    def prev_pair(c):
        return pl.BlockSpec((PAIR, kv_dup), lambda i: (jnp.maximum(i * sub - 1, 0), c))

    kern = functools.partial(_attn_kernel, tiles_per_seq=tiles_per_seq,
                             pairs_per_kv=(b_width // LANES) // (kv_dup // LANES))
    return pl.pallas_call(
        kern,
        grid=(n // ATTN_TILE,),
        in_specs=[
            pl.BlockSpec(memory_space=pltpu.SMEM),
            cur(a_width, qa), prev(ka), cur(a_width, ka), prev(va), cur(a_width, va),
            cur(b_width, qb), prev_pair(kb), cur(kv_dup, kb), prev_pair(vb), cur(kv_dup, vb),
            _resident(bias_a.shape), _resident(bias_b.shape),
        ],
        out_specs=[cur(a_width, 0), cur(b_width, 0)],
        out_shape=[jax.ShapeDtypeStruct((n, a_width), BF16),
                   jax.ShapeDtypeStruct((n, b_width), BF16)],
        compiler_params=pltpu.CompilerParams(
            dimension_semantics=("arbitrary",), vmem_limit_bytes=VMEM_LIMIT),
        name="band_attention",
    )(sink, proj, proj, proj, proj, proj, proj, proj, proj, proj, proj, bias_a, bias_b)


def _mlp_rows(n_rows, early_fn, h1_fn, o_ref, gpre_ref, wup_ref, wdown_ref, gpost_ref,
              side_jobs=()):
    n_sub = n_rows // SUB_ROWS
    n_chunks = wup_ref.shape[1] // FF_CHUNK
    n_slots = n_sub * n_chunks * 2
    jobs = list(side_jobs)
    job_slot = {round((j + 0.5) * n_slots / len(jobs)): job for j, job in enumerate(jobs)}
    assert len(job_slot) == len(jobs) and all(0 < k <= n_slots for k in job_slot)
    rows = [slice(s * SUB_ROWS, (s + 1) * SUB_ROWS) for s in range(n_sub)]
    early = {0: early_fn(rows[0])} if early_fn else {}
    slot = 0
    for s in range(n_sub):
        if early_fn and s + 1 < n_sub:
            early[s + 1] = early_fn(rows[s + 1])
        h1 = h1_fn(rows[s], early.pop(s, None))
        hn = _rms(h1, gpre_ref[...]).astype(BF16)
        acc = None
        for c in range(n_chunks):
            ff = slice(c * FF_CHUNK, (c + 1) * FF_CHUNK)
            u = jnp.maximum(_dot(hn, wup_ref[:, ff]), 0.0)
            slot += 1
            if slot in job_slot:
                job_slot[slot]()
            part = _dot((u * u).astype(BF16), wdown_ref[ff, :])
            slot += 1
            if slot in job_slot:
                job_slot[slot]()
            acc = part if acc is None else acc + part
        o_ref[rows[s], :] = h1 + _rms(acc, gpost_ref[...])


def _outproj_mlp_kernel(oa_ref, ob_ref, x_ref, woa_ref, wob_ref, gpost_ref,
                        gpre_ref, wup_ref, wdown_ref, gmpost_ref, o_ref):
    def out_proj(rows):
        return _dot(oa_ref[rows, :], woa_ref[...]) + _dot(ob_ref[rows, :], wob_ref[...])

    def h1_fn(rows, y):
        return x_ref[rows, :] + _rms(y, gpost_ref[...])

    _mlp_rows(x_ref.shape[0], out_proj, h1_fn, o_ref, gpre_ref, wup_ref, wdown_ref, gmpost_ref)


def _outproj_mlp(oa, ob, x2, w_oa, w_ob, g_post, g_mpre, w_up, w_down, g_mpost):
    n, d = x2.shape
    assert n % OUTPROJ_TILE == 0
    row = lambda w: pl.BlockSpec((OUTPROJ_TILE, w), lambda i: (i, 0))
    return pl.pallas_call(
        _outproj_mlp_kernel,
        grid=(n // OUTPROJ_TILE,),
        in_specs=[row(oa.shape[1]), row(ob.shape[1]), row(d),
                  _resident(w_oa.shape), _resident(w_ob.shape), _resident((1, d)),
                  _resident((1, d)), _resident(w_up.shape), _resident(w_down.shape),
                  _resident((1, d))],
        out_specs=row(d),
        out_shape=jax.ShapeDtypeStruct((n, d), F32),
        compiler_params=pltpu.CompilerParams(
            dimension_semantics=("arbitrary",), vmem_limit_bytes=VMEM_LIMIT),
        name="outproj_mlp",
    )(oa, ob, x2, w_oa, w_ob, g_post, g_mpre, w_up, w_down, g_mpost)


def _pool_mixer_jobs(h_ref, halo_fn, g_ref, pw_ref, pscale_ref, t_in_seq, y_ref):
    tm = h_ref.shape[0]
    pc = pw_ref.shape[1]
    st = {}

    def prepare():
        st["hn"] = _rms(h_ref[...], g_ref[...])
        st["halo"] = halo_fn()
        pos1 = t_in_seq * tm + 1 + lax.broadcasted_iota(jnp.int32, (tm, 1), 0)
        st["pos1"] = pos1
        st["inv_pos1"] = 1.0 / pos1.astype(F32)

    def group(gi, w):
        cols = slice(gi * pc, (gi + 1) * pc)
        hn = st["hn"][:, cols]
        wsum = jnp.concatenate([st["halo"][:, cols], hn], axis=0)
        k = 1
        while k < w:
            wsum = wsum + pltpu.roll(wsum, k, axis=0)
            k *= 2
        inv_cnt = jnp.where(st["pos1"] < w, st["inv_pos1"], 1.0 / w)
        d = (wsum[POOL_HALO:, :] * inv_cnt - hn).astype(BF16)
        y_ref[:, cols] = _dot(d, pw_ref[gi]) * pscale_ref[:, cols]

    return [prepare] + [functools.partial(group, gi, w) for gi, w in enumerate(POOL_WINDOWS)]


def _pool_mlp_kernel(h_ref, hnext_ref, gpre_ref, pw_ref, pscale_ref, gpost_ref,
                     gmpre_ref, wup_ref, wdown_ref, gmpost_ref, o_ref, y_ref, *, tiles_per_seq):
    i = pl.program_id(0)
    tm, dm = h_ref.shape
    @pl.when(i == 0)
    def _():
        for job in _pool_mixer_jobs(h_ref, lambda: jnp.zeros((POOL_HALO, dm), F32),
                                    gpre_ref, pw_ref, pscale_ref, 0, y_ref):
            job()

    y = y_ref[...]

    def h1_fn(rows, _):
        return h_ref[rows, :] + _rms(y[rows, :], gpost_ref[...])

    t_next = (i + 1) % tiles_per_seq

    def next_halo():
        keep = jnp.where(t_next == 0, 0.0, 1.0)
        return _rms(h_ref[tm - POOL_HALO:, :], gpre_ref[...]) * keep

    jobs = _pool_mixer_jobs(hnext_ref, next_halo, gpre_ref, pw_ref, pscale_ref, t_next, y_ref)
    _mlp_rows(tm, None, h1_fn, o_ref, gmpre_ref, wup_ref, wdown_ref, gmpost_ref, jobs)


def _pool_mlp(h2, seq_len, g_pre, pool_w, pool_scale, g_post, g_mpre, w_up, w_down, g_mpost):
    n, d = h2.shape
    tiles_per_seq = seq_len // TOKEN_TILE
    n_tiles = n // TOKEN_TILE
    row = pl.BlockSpec((TOKEN_TILE, d), lambda i: (i, 0))
    nxt = pl.BlockSpec((TOKEN_TILE, d), lambda i: (jnp.minimum(i + 1, n_tiles - 1), 0))
    kern = functools.partial(_pool_mlp_kernel, tiles_per_seq=tiles_per_seq)
    return pl.pallas_call(
        kern,
        grid=(n_tiles,),
        in_specs=[row, nxt, _resident((1, d)), _resident(pool_w.shape), _resident((1, d)),
                  _resident((1, d)), _resident((1, d)), _resident(w_up.shape),
                  _resident(w_down.shape), _resident((1, d))],
        out_specs=row,
        out_shape=jax.ShapeDtypeStruct((n, d), F32),
        scratch_shapes=[pltpu.VMEM((TOKEN_TILE, d), F32)],
        compiler_params=pltpu.CompilerParams(
            dimension_semantics=("arbitrary",), vmem_limit_bytes=VMEM_LIMIT),
        name="pool_mlp",
    )(h2, h2, g_pre, pool_w, pool_scale, g_post, g_mpre, w_up, w_down, g_mpost)


def _pair_band_index(n_prev):
    band = (n_prev + 2) * CHUNK
    r = np.arange(PAIR)[:, None]
    k = np.arange(band)[None, :]
    e, qi = r // CHUNK, r % CHUNK
    kc = k // CHUNK - e
    ok = (kc >= 0) & (kc <= n_prev)
    key_pos = kc * CHUNK + k % CHUNK - n_prev * CHUNK
    return qi - key_pos, ok


def _t5_bucket(rel_kq):
    nb = T5_BUCKETS // 2
    ret = (rel_kq > 0).astype(np.int32) * nb
    n = np.abs(rel_kq)
    max_exact = nb // 2
    large = max_exact + (np.log(np.maximum(n, 1) / max_exact)
                         / math.log(T5_MAX_DIST / max_exact) * (nb - max_exact)).astype(np.int32)
    large = np.minimum(large, nb - 1)
    return ret + np.where(n < max_exact, n, large)


def _toeplitz_bias(table, index_of_rel, n_prev):
    band = (n_prev + 2) * CHUNK
    period = band + PAIR
    delta = np.arange(period)
    delta = np.where(delta < band, delta, delta - period)
    diag = jnp.take(table, index_of_rel(n_prev * CHUNK - delta), axis=0).T
    skew = jnp.tile(diag, (1, PAIR))[:, :PAIR * (period - 1)].reshape(-1, PAIR, period - 1)
    bias = skew[:, :, :band]
    _, ok = _pair_band_index(n_prev)
    return jnp.where(ok[None], bias, NEG_INF).astype(F32)


def _bias_tables(relpos_a, t5_table):
    bias_a = _toeplitz_bias(
        relpos_a, lambda rel: np.clip(rel, -A_MAX_REL, A_MAX_REL) + A_MAX_REL, A_PREV_CHUNKS)
    bias_b = _toeplitz_bias(t5_table, lambda rel: _t5_bucket(-rel), B_PREV_CHUNKS)
    kc_a = np.arange(A_BAND) // CHUNK
    hide_a = np.zeros((PAIRS_PER_TILE + 1, 1, 1, A_BAND), bool)
    for jj in range(PAIRS_PER_TILE):
        hide_a[jj, 0, 0] = 2 * jj - A_PREV_CHUNKS + kc_a < 0
    kc_b = np.arange(B_BAND) // CHUNK
    hide_b = np.zeros((2, 1, 1, B_BAND), bool)
    hide_b[0, 0, 0] = kc_b - B_PREV_CHUNKS < 0
    bias_a = jnp.where(hide_a, NEG_INF, bias_a[None]) * LOG2E
    bias_b = jnp.where(hide_b, NEG_INF, bias_b[None]) * LOG2E
    return bias_a, bias_b


def kernel(x, t5_table, e_norm_pre, e_norm_post, e_w_in, e_w_out, e_relpos_a, e_sink_b,
           o_norm_pre, o_norm_post, o_pool_w, o_pool_scale,
           mlp_norm_pre, mlp_norm_post, mlp_w_up, mlp_w_down):
    bsz, seq_len, d = x.shape
    depth = mlp_w_up.shape[0]
    a_heads = e_relpos_a.shape[2]
    b_heads = e_sink_b.shape[1]
    a_width, b_width = a_heads * HEAD_DIM, b_heads * HEAD_DIM
    b_kv_width = (b_heads // B_GROUP) * HEAD_DIM
    assert seq_len % ATTN_TILE == 0 and seq_len % TOKEN_TILE == 0
    assert a_heads == b_heads and a_heads % 2 == 0
    row = lambda v: v.reshape(1, -1).astype(F32)

    h = x.reshape(bsz * seq_len, d)
    for layer in range(depth):
        i = layer // 2
        w_up = mlp_w_up[layer].astype(BF16)
        w_down = mlp_w_down[layer].astype(BF16)
        g_mpre, g_mpost = row(mlp_norm_pre[layer]), row(mlp_norm_post[layer])
        if layer % 2 == 0:
            seg = a_width
            q_segs = (0, 3)
            assert b_width == seg and (2 * b_kv_width) % (2 * LANES) == 0
            proj = _inproj(h, row(e_norm_pre[i]), e_w_in[i].astype(BF16), q_segs, seg,
                           3 * a_width + b_width)
            bias_a, bias_b = _bias_tables(e_relpos_a[i], t5_table)
            oa, ob = _attention(proj, e_sink_b[i].astype(F32) * LOG2E, bias_a, bias_b,
                                seq_len, a_width, b_width, b_kv_width)
            w_out = e_w_out[i].astype(BF16)
            h = _outproj_mlp(oa, ob, h, w_out[:a_width], w_out[a_width:], row(e_norm_post[i]),
                             g_mpre, w_up, w_down, g_mpost)
        else:
            h = _pool_mlp(h, seq_len, row(o_norm_pre[i]), o_pool_w[i].astype(BF16),
                          row(o_pool_scale[i]), row(o_norm_post[i]),
                          g_mpre, w_up, w_down, g_mpost)
    return h.reshape(bsz, seq_len, d)
```

```python
import functools
import math

import jax
import jax.numpy as jnp
import numpy as np
from jax import lax
from jax.experimental import pallas as pl
from jax.experimental.pallas import tpu as pltpu

CHUNK = 64
HEAD_DIM = 64
A_PREV_CHUNKS = 8
A_MAX_REL = 128
B_GROUP = 4
B_PREV_CHUNKS = 2
T5_BUCKETS = 32
T5_MAX_DIST = 128
POOL_WINDOWS = (2, 4, 8, 16)
RMS_EPS = 1e-6
NEG_INF = -1e30
LOG2E = math.log2(math.e)

LANES = 128
V7X_VMEM_BYTES = 64 * 1024 * 1024

TOKEN_TILE = 1024
INPROJ_TILE = 1024
OUTPROJ_TILE = 1024
SUB_ROWS = 512
OUTPROJ_SUB_ROWS = 512
ATTN_TILE = 512
PAIR = 2 * CHUNK
PAIRS_PER_TILE = ATTN_TILE // PAIR
A_BAND = (A_PREV_CHUNKS + 2) * CHUNK
B_BAND = (B_PREV_CHUNKS + 2) * CHUNK
ATTN_LOOKAHEAD = 2
POOL_HALO = 16
FF_CHUNK = 2048
VMEM_LIMIT = 56 * 1024 * 1024

BF16 = jnp.bfloat16
F32 = jnp.float32


def _rms(x, g):
    ms = jnp.mean(x * x, axis=-1, keepdims=True)
    return (x * lax.rsqrt(ms + RMS_EPS)) * g


def _dot(a, b):
    return jnp.dot(a, b, preferred_element_type=F32)


def _dot_nt(a, b):
    return lax.dot_general(a, b, (((1,), (1,)), ((), ())), preferred_element_type=F32)


def _resident(shape):
    nd = len(shape)
    return pl.BlockSpec(shape, lambda *_: (0,) * nd, pipeline_mode=pl.Buffered(1))


def _inproj_kernel(x_ref, g_ref, w_ref, o_ref, *, seg, q_segs, q_scale, kv_b0):
    lo_half = lax.broadcasted_iota(jnp.int32, (1, LANES), 1) < HEAD_DIM
    for r in range(x_ref.shape[0] // SUB_ROWS):
        rows = slice(r * SUB_ROWS, (r + 1) * SUB_ROWS)
        xb = _rms(x_ref[rows, :], g_ref[...]).astype(BF16)
        for s in range(kv_b0 // seg):
            y = _dot(xb, w_ref[:, s * seg:(s + 1) * seg])
            if s in q_segs:
                y = y * q_scale
            o_ref[rows, s * seg:(s + 1) * seg] = y.astype(BF16)
        y = _dot(xb, w_ref[:, kv_b0:])
        for j in range(y.shape[1] // LANES):
            two = y[:, j * LANES:(j + 1) * LANES]
            swapped = pltpu.roll(two, HEAD_DIM, axis=1)
            c0 = kv_b0 + 2 * j * LANES
            o_ref[rows, c0:c0 + LANES] = jnp.where(lo_half, two, swapped).astype(BF16)
            o_ref[rows, c0 + LANES:c0 + 2 * LANES] = jnp.where(lo_half, swapped, two).astype(BF16)


def _inproj(x2, g, w_in, q_segs, seg, kv_b0):
    n, d = x2.shape
    out_cols = kv_b0 + 2 * (w_in.shape[1] - kv_b0)
    assert n % INPROJ_TILE == 0 and kv_b0 % seg == 0
    kern = functools.partial(_inproj_kernel, seg=seg, q_segs=q_segs,
                             q_scale=HEAD_DIM ** -0.5 * LOG2E, kv_b0=kv_b0)
    return pl.pallas_call(
        kern,
        grid=(n // INPROJ_TILE,),
        in_specs=[
            pl.BlockSpec((INPROJ_TILE, d), lambda i: (i, 0)),
            _resident((1, d)),
            _resident(w_in.shape),
        ],
        out_specs=pl.BlockSpec((INPROJ_TILE, out_cols), lambda i: (i, 0)),
        out_shape=jax.ShapeDtypeStruct((n, out_cols), BF16),
        compiler_params=pltpu.CompilerParams(
            dimension_semantics=("arbitrary",), vmem_limit_bytes=VMEM_LIMIT),
        name="inproj",
    )(x2, g, w_in)


def _softmax_pv(s, v_ones, sink):
    m = jnp.max(s, axis=-1, keepdims=True)
    if sink is not None:
        m = jnp.maximum(m, sink)
    pb = jnp.exp2((s - m).astype(BF16))
    if isinstance(v_ones, tuple):
        w1 = v_ones[0].shape[0]
        o = _dot(pb[:, :w1], v_ones[0]) + _dot(pb[:, w1:], v_ones[1])
    else:
        o = _dot(pb, v_ones)
    den = o[:, LANES:]
    if sink is not None:
        den = den + jnp.exp2(sink - m)
    out = o[:, :LANES] * (1.0 / den)
    half = out.shape[0] // 2
    lo_half = lax.broadcasted_iota(jnp.int32, (1, LANES), 1) < HEAD_DIM
    return jnp.where(lo_half, out[:half], out[half:])


def _attn_kernel(sink_ref, qa_ref, kap_ref, kac_ref, vap_ref, vac_ref,
                 qb_ref, kbp_ref, kbc_ref, vbp_ref, vbc_ref,
                 basea_ref, baseb_ref, hida_ref, hidb_ref,
                 o_ref, biasa_ref, biasb_ref, *, tiles_per_seq, pairs_per_kv):
    i = pl.program_id(0)

    @pl.when(i == 0)
    def _():
        for base_ref, hid_ref, bias_ref in ((basea_ref, hida_ref, biasa_ref),
                                            (baseb_ref, hidb_ref, biasb_ref)):
            for v in range(hid_ref.shape[0]):
                for hd in range(base_ref.shape[0]):
                    bias_ref[v, hd] = base_ref[hd] + hid_ref[v]

    oa_ref = o_ref.at[:, :qa_ref.shape[1]]
    ob_ref = o_ref.at[:, qa_ref.shape[1]:]
    first = (i % tiles_per_seq) == 0
    lo_half = lax.broadcasted_iota(jnp.int32, (1, LANES), 1) < HEAD_DIM
    zero = jnp.zeros((), BF16)
    n_head_pairs = qa_ref.shape[1] // LANES

    def lanes(hp):
        return slice(hp * LANES, (hp + 1) * LANES)

    def stacked_q(q_ref, r0, hp):
        q2 = q_ref[r0:r0 + PAIR, lanes(hp)]
        return jnp.concatenate([jnp.where(lo_half, q2, zero), jnp.where(lo_half, zero, q2)],
                               axis=0)

    def with_ones(v):
        return jnp.concatenate([v, jnp.ones(v.shape, BF16)], axis=1)

    def logits_a(jj, hp):
        r0 = jj * PAIR
        q = stacked_q(qa_ref, r0, hp)
        s = jnp.concatenate([_dot_nt(q, kap_ref[r0:, lanes(hp)]),
                             _dot_nt(q, kac_ref[:r0 + PAIR, lanes(hp)])], axis=1)
        bias = biasa_ref[jnp.where(first, jj, PAIRS_PER_TILE), 2 * hp:2 * hp + 2]
        return s + bias.reshape(s.shape)

    def finish_a(jj, hp, s):
        r0 = jj * PAIR
        return _softmax_pv(s, (with_ones(vap_ref[r0:, lanes(hp)]),
                               with_ones(vac_ref[:r0 + PAIR, lanes(hp)])), None)

    def band_b(prev_ref, cur_ref, jj, hp):
        r0 = jj * PAIR
        kv = lanes(hp // pairs_per_kv)
        if jj == 0:
            return jnp.concatenate([prev_ref[:, kv], cur_ref[:PAIR, kv]], axis=0)
        return cur_ref[r0 - PAIR:r0 + PAIR, kv]

    def logits_b(jj, hp):
        s = _dot_nt(stacked_q(qb_ref, jj * PAIR, hp), band_b(kbp_ref, kbc_ref, jj, hp))
        variant = jnp.where(first, 0, 1) if jj == 0 else 1
        return s + biasb_ref[variant, 2 * hp:2 * hp + 2].reshape(s.shape)

    def finish_b(jj, hp, s):
        sink = jnp.concatenate(
            [jnp.full((PAIR, 1), sink_ref[2 * hp + e], F32) for e in range(2)], axis=0)
        return _softmax_pv(s, with_ones(band_b(vbp_ref, vbc_ref, jj, hp)), sink)

    mixers = {"a": (logits_a, finish_a, oa_ref), "b": (logits_b, finish_b, ob_ref)}
    units = [(mx, jj, hp) for hp in range(n_head_pairs) for jj in range(PAIRS_PER_TILE)
             for mx in ("a", "b")]
    pending = {}
    for t in range(len(units) + ATTN_LOOKAHEAD):
        if t < len(units):
            mx, jj, hp = units[t]
            pending[t] = mixers[mx][0](jj, hp)
        if t >= ATTN_LOOKAHEAD:
            mx, jj, hp = units[t - ATTN_LOOKAHEAD]
            out = mixers[mx][1](jj, hp, pending.pop(t - ATTN_LOOKAHEAD))
            mixers[mx][2][jj * PAIR:(jj + 1) * PAIR, lanes(hp)] = out.astype(BF16)


def _attention(proj, sink, bias_a, bias_b, hidden_a, hidden_b, seq_len,
               a_width, b_width, b_kv_width):
    n = proj.shape[0]
    tiles_per_seq = seq_len // ATTN_TILE
    kv_dup = 2 * b_kv_width
    assert a_width == b_width and kv_dup % LANES == 0 and a_width % kv_dup == 0
    sub = ATTN_TILE // PAIR
    qa, ka, va, qb = 0, 1, 2, 3
    kb = 4 * a_width // kv_dup
    vb = kb + 1

    def cur(width, c):
        return pl.BlockSpec((ATTN_TILE, width), lambda i: (i, c))

    def prev(c):
        return pl.BlockSpec((ATTN_TILE, a_width), lambda i: (jnp.maximum(i - 1, 0), c))

    def prev_pair(c):
        return pl.BlockSpec((PAIR, kv_dup), lambda i: (jnp.maximum(i * sub - 1, 0), c))

    kern = functools.partial(_attn_kernel, tiles_per_seq=tiles_per_seq,
                             pairs_per_kv=(b_width // LANES) // (kv_dup // LANES))
    return pl.pallas_call(
        kern,
        grid=(n // ATTN_TILE,),
        in_specs=[
            pl.BlockSpec(memory_space=pltpu.SMEM),
            cur(a_width, qa), prev(ka), cur(a_width, ka), prev(va), cur(a_width, va),
            cur(b_width, qb), prev_pair(kb), cur(kv_dup, kb), prev_pair(vb), cur(kv_dup, vb),
            _resident(bias_a.shape), _resident(bias_b.shape),
            _resident(hidden_a.shape), _resident(hidden_b.shape),
        ],
        out_specs=cur(a_width + b_width, 0),
        out_shape=jax.ShapeDtypeStruct((n, a_width + b_width), BF16),
        scratch_shapes=[pltpu.VMEM((hidden_a.shape[0],) + bias_a.shape, F32),
                        pltpu.VMEM((hidden_b.shape[0],) + bias_b.shape, F32)],
        compiler_params=pltpu.CompilerParams(
            dimension_semantics=("arbitrary",), vmem_limit_bytes=VMEM_LIMIT),
        name="band_attention",
    )(sink, proj, proj, proj, proj, proj, proj, proj, proj, proj, proj,
      bias_a, bias_b, hidden_a, hidden_b)


def _mlp_rows(n_rows, sub_rows, early_fn, h1_fn, o_ref, gpre_ref, wup_ref, wdown_ref,
              gpost_ref, side_jobs=()):
    n_sub = n_rows // sub_rows
    n_chunks = wup_ref.shape[1] // FF_CHUNK
    n_slots = n_sub * n_chunks * 2
    jobs = list(side_jobs)
    job_slot = {round((j + 0.5) * n_slots / len(jobs)): job for j, job in enumerate(jobs)}
    assert len(job_slot) == len(jobs) and all(0 < k <= n_slots for k in job_slot)
    rows = [slice(s * sub_rows, (s + 1) * sub_rows) for s in range(n_sub)]
    early = {0: early_fn(rows[0])} if early_fn else {}
    slot = 0
    for s in range(n_sub):
        if early_fn and s + 1 < n_sub:
            early[s + 1] = early_fn(rows[s + 1])
        h1 = h1_fn(rows[s], early.pop(s, None))
        hn = _rms(h1, gpre_ref[...]).astype(BF16)
        acc = None
        for c in range(n_chunks):
            ff = slice(c * FF_CHUNK, (c + 1) * FF_CHUNK)
            u = jnp.maximum(_dot(hn, wup_ref[:, ff]), 0.0)
            slot += 1
            if slot in job_slot:
                job_slot[slot]()
            part = _dot((u * u).astype(BF16), wdown_ref[ff, :])
            slot += 1
            if slot in job_slot:
                job_slot[slot]()
            acc = part if acc is None else acc + part
        o_ref[rows[s], :] = h1 + _rms(acc, gpost_ref[...])


def _outproj_mlp_kernel(a_ref, x_ref, wout_ref, gpost_ref,
                        gpre_ref, wup_ref, wdown_ref, gmpost_ref, o_ref):
    def out_proj(rows):
        return _dot(a_ref[rows, :], wout_ref[...])

    def h1_fn(rows, y):
        return x_ref[rows, :] + _rms(y, gpost_ref[...])

    _mlp_rows(x_ref.shape[0], OUTPROJ_SUB_ROWS, out_proj, h1_fn, o_ref,
              gpre_ref, wup_ref, wdown_ref, gmpost_ref)


def _outproj_mlp(attn, x2, w_out, g_post, g_mpre, w_up, w_down, g_mpost):
    n, d = x2.shape
    assert n % OUTPROJ_TILE == 0
    row = lambda w: pl.BlockSpec((OUTPROJ_TILE, w), lambda i: (i, 0))
    return pl.pallas_call(
        _outproj_mlp_kernel,
        grid=(n // OUTPROJ_TILE,),
        in_specs=[row(attn.shape[1]), row(d), _resident(w_out.shape), _resident((1, d)),
                  _resident((1, d)), _resident(w_up.shape), _resident(w_down.shape),
                  _resident((1, d))],
        out_specs=row(d),
        out_shape=jax.ShapeDtypeStruct((n, d), F32),
        compiler_params=pltpu.CompilerParams(
            dimension_semantics=("arbitrary",), vmem_limit_bytes=VMEM_LIMIT),
        name="outproj_mlp",
    )(attn, x2, w_out, g_post, g_mpre, w_up, w_down, g_mpost)


def _pool_mixer_jobs(h_ref, halo_fn, g_ref, pw_ref, pscale_ref, t_in_seq, y_ref):
    tm = h_ref.shape[0]
    pc = pw_ref.shape[1]
    st = {}

    def prepare():
        st["hn"] = _rms(h_ref[...], g_ref[...])
        st["halo"] = halo_fn()
        pos1 = t_in_seq * tm + 1 + lax.broadcasted_iota(jnp.int32, (tm, 1), 0)
        st["pos1"] = pos1
        st["inv_pos1"] = 1.0 / pos1.astype(F32)

    def group(gi, w):
        cols = slice(gi * pc, (gi + 1) * pc)
        hn = st["hn"][:, cols]
        wsum = jnp.concatenate([st["halo"][:, cols], hn], axis=0)
        k = 1
        while k < w:
            wsum = wsum + pltpu.roll(wsum, k, axis=0)
            k *= 2
        inv_cnt = jnp.where(st["pos1"] < w, st["inv_pos1"], 1.0 / w)
        d = (wsum[POOL_HALO:, :] * inv_cnt - hn).astype(BF16)
        y_ref[:, cols] = _dot(d, pw_ref[gi]) * pscale_ref[:, cols]

    return [prepare] + [functools.partial(group, gi, w) for gi, w in enumerate(POOL_WINDOWS)]


def _pool_mlp_kernel(h_ref, hnext_ref, gpre_ref, pw_ref, pscale_ref, gpost_ref,
                     gmpre_ref, wup_ref, wdown_ref, gmpost_ref, o_ref, y_ref, *, tiles_per_seq):
    i = pl.program_id(0)
    tm, dm = h_ref.shape
    @pl.when(i == 0)
    def _():
        for job in _pool_mixer_jobs(h_ref, lambda: jnp.zeros((POOL_HALO, dm), F32),
                                    gpre_ref, pw_ref, pscale_ref, 0, y_ref):
            job()

    y = y_ref[...]

    def h1_fn(rows, _):
        return h_ref[rows, :] + _rms(y[rows, :], gpost_ref[...])

    t_next = (i + 1) % tiles_per_seq

    def next_halo():
        keep = jnp.where(t_next == 0, 0.0, 1.0)
        return _rms(h_ref[tm - POOL_HALO:, :], gpre_ref[...]) * keep

    jobs = _pool_mixer_jobs(hnext_ref, next_halo, gpre_ref, pw_ref, pscale_ref, t_next, y_ref)
    _mlp_rows(tm, SUB_ROWS, None, h1_fn, o_ref, gmpre_ref, wup_ref, wdown_ref, gmpost_ref, jobs)


def _pool_mlp(h2, seq_len, g_pre, pool_w, pool_scale, g_post, g_mpre, w_up, w_down, g_mpost):
    n, d = h2.shape
    tiles_per_seq = seq_len // TOKEN_TILE
    n_tiles = n // TOKEN_TILE
    row = pl.BlockSpec((TOKEN_TILE, d), lambda i: (i, 0))
    nxt = pl.BlockSpec((TOKEN_TILE, d), lambda i: (jnp.minimum(i + 1, n_tiles - 1), 0))
    kern = functools.partial(_pool_mlp_kernel, tiles_per_seq=tiles_per_seq)
    return pl.pallas_call(
        kern,
        grid=(n_tiles,),
        in_specs=[row, nxt, _resident((1, d)), _resident(pool_w.shape), _resident((1, d)),
                  _resident((1, d)), _resident((1, d)), _resident(w_up.shape),
                  _resident(w_down.shape), _resident((1, d))],
        out_specs=row,
        out_shape=jax.ShapeDtypeStruct((n, d), F32),
        scratch_shapes=[pltpu.VMEM((TOKEN_TILE, d), F32)],
        compiler_params=pltpu.CompilerParams(
            dimension_semantics=("arbitrary",), vmem_limit_bytes=VMEM_LIMIT),
        name="pool_mlp",
    )(h2, h2, g_pre, pool_w, pool_scale, g_post, g_mpre, w_up, w_down, g_mpost)


def _pair_band_index(n_prev):
    band = (n_prev + 2) * CHUNK
    r = np.arange(PAIR)[:, None]
    k = np.arange(band)[None, :]
    e, qi = r // CHUNK, r % CHUNK
    kc = k // CHUNK - e
    ok = (kc >= 0) & (kc <= n_prev)
    key_pos = kc * CHUNK + k % CHUNK - n_prev * CHUNK
    return qi - key_pos, ok


def _t5_bucket(rel_kq):
    nb = T5_BUCKETS // 2
    ret = (rel_kq > 0).astype(np.int32) * nb
    n = np.abs(rel_kq)
    max_exact = nb // 2
    large = max_exact + (np.log(np.maximum(n, 1) / max_exact)
                         / math.log(T5_MAX_DIST / max_exact) * (nb - max_exact)).astype(np.int32)
    large = np.minimum(large, nb - 1)
    return ret + np.where(n < max_exact, n, large)


def _toeplitz_bias(table, index_of_rel, n_prev):
    band = (n_prev + 2) * CHUNK
    period = band + PAIR
    delta = np.arange(period)
    delta = np.where(delta < band, delta, delta - period)
    diag = jnp.take(table, index_of_rel(n_prev * CHUNK - delta), axis=0).T
    skew = jnp.tile(diag, (1, PAIR))[:, :PAIR * (period - 1)].reshape(-1, PAIR, period - 1)
    bias = skew[:, :, :band]
    _, ok = _pair_band_index(n_prev)
    return jnp.where(ok[None], bias, NEG_INF).astype(F32)


def _bias_tables(relpos_a, t5_table):
    bias_a = _toeplitz_bias(
        relpos_a, lambda rel: np.clip(rel, -A_MAX_REL, A_MAX_REL) + A_MAX_REL, A_PREV_CHUNKS)
    bias_b = _toeplitz_bias(t5_table, lambda rel: _t5_bucket(-rel), B_PREV_CHUNKS)
    kc_a = np.arange(A_BAND) // CHUNK
    hide_a = np.zeros((PAIRS_PER_TILE + 1, 1, 1, A_BAND), bool)
    for jj in range(PAIRS_PER_TILE):
        hide_a[jj, 0, 0] = 2 * jj - A_PREV_CHUNKS + kc_a < 0
    kc_b = np.arange(B_BAND) // CHUNK
    hide_b = np.zeros((2, 1, 1, B_BAND), bool)
    hide_b[0, 0, 0] = kc_b - B_PREV_CHUNKS < 0
    hidden = lambda hide: jnp.asarray(np.where(hide[:, 0], NEG_INF * LOG2E, 0.0), F32)
    return bias_a * LOG2E, bias_b * LOG2E, hidden(hide_a), hidden(hide_b)


def kernel(x, t5_table, e_norm_pre, e_norm_post, e_w_in, e_w_out, e_relpos_a, e_sink_b,
           o_norm_pre, o_norm_post, o_pool_w, o_pool_scale,
           mlp_norm_pre, mlp_norm_post, mlp_w_up, mlp_w_down):
    bsz, seq_len, d = x.shape
    depth = mlp_w_up.shape[0]
    a_heads = e_relpos_a.shape[2]
    b_heads = e_sink_b.shape[1]
    a_width, b_width = a_heads * HEAD_DIM, b_heads * HEAD_DIM
    b_kv_width = (b_heads // B_GROUP) * HEAD_DIM
    assert seq_len % ATTN_TILE == 0 and seq_len % TOKEN_TILE == 0
    assert a_heads == b_heads and a_heads % 2 == 0
    row = lambda v: v.reshape(1, -1).astype(F32)

    h = x.reshape(bsz * seq_len, d)
    for layer in range(depth):
        i = layer // 2
        w_up = mlp_w_up[layer].astype(BF16)
        w_down = mlp_w_down[layer].astype(BF16)
        g_mpre, g_mpost = row(mlp_norm_pre[layer]), row(mlp_norm_post[layer])
        if layer % 2 == 0:
            seg = a_width
            q_segs = (0, 3)
            assert b_width == seg and (2 * b_kv_width) % (2 * LANES) == 0
            proj = _inproj(h, row(e_norm_pre[i]), e_w_in[i].astype(BF16), q_segs, seg,
                           3 * a_width + b_width)
            attn = _attention(proj, e_sink_b[i].astype(F32) * LOG2E,
                              *_bias_tables(e_relpos_a[i], t5_table),
                              seq_len, a_width, b_width, b_kv_width)
            h = _outproj_mlp(attn, h, e_w_out[i].astype(BF16), row(e_norm_post[i]),
                             g_mpre, w_up, w_down, g_mpost)
        else:
            h = _pool_mlp(h, seq_len, row(o_norm_pre[i]), o_pool_w[i].astype(BF16),
                          row(o_pool_scale[i]), row(o_norm_post[i]),
                          g_mpre, w_up, w_down, g_mpost)
    return h.reshape(bsz, seq_len, d)
```

```python
import functools
import math

import jax
import jax.numpy as jnp
import numpy as np
from jax import lax
from jax.experimental import pallas as pl
from jax.experimental.pallas import tpu as pltpu

CHUNK = 64
HEAD_DIM = 64
A_PREV_CHUNKS = 8
A_MAX_REL = 128
B_GROUP = 4
B_PREV_CHUNKS = 2
T5_BUCKETS = 32
T5_MAX_DIST = 128
POOL_WINDOWS = (2, 4, 8, 16)
RMS_EPS = 1e-6
NEG_INF = -1e30
LOG2E = math.log2(math.e)

LANES = 128
BF16_ROWS = 16
V7X_VMEM_BYTES = 64 * 1024 * 1024

TOKEN_TILE = 1024
INPROJ_TILE = 1024
OUTPROJ_TILE = 1024
SUB_ROWS = 512
OUTPROJ_SUB_ROWS = 512
ATTN_TILE = 512
PAIR = 2 * CHUNK
PAIRS_PER_TILE = ATTN_TILE // PAIR
A_BAND = (A_PREV_CHUNKS + 2) * CHUNK
B_BAND = (B_PREV_CHUNKS + 2) * CHUNK
ATTN_LOOKAHEAD = 2
POOL_HALO = 16
FF_CHUNK = 2048
VMEM_LIMIT = 56 * 1024 * 1024

BF16 = jnp.bfloat16
F32 = jnp.float32


def _rms(x, g):
    ms = jnp.mean(x * x, axis=-1, keepdims=True)
    return (x * lax.rsqrt(ms + RMS_EPS)) * g


def _dot(a, b):
    return jnp.dot(a, b, preferred_element_type=F32)


def _dot_nt(a, b):
    return lax.dot_general(a, b, (((1,), (1,)), ((), ())), preferred_element_type=F32)


def _resident(shape):
    nd = len(shape)
    return pl.BlockSpec(shape, lambda *_: (0,) * nd, pipeline_mode=pl.Buffered(1))


def _inproj_kernel(x_ref, g_ref, w_ref, *refs, n_cast, seg, q_segs, q_scale, kv_b0):
    o_ref = refs[n_cast]
    for src_ref, dst_ref in zip(refs[:n_cast], refs[n_cast + 1:]):
        dst_ref[...] = src_ref[...].astype(BF16)
    lo_half = lax.broadcasted_iota(jnp.int32, (1, LANES), 1) < HEAD_DIM
    for r in range(x_ref.shape[0] // SUB_ROWS):
        rows = slice(r * SUB_ROWS, (r + 1) * SUB_ROWS)
        xb = _rms(x_ref[rows, :], g_ref[...]).astype(BF16)
        for s in range(kv_b0 // seg):
            y = _dot(xb, w_ref[:, s * seg:(s + 1) * seg])
            if s in q_segs:
                y = y * q_scale
            o_ref[rows, s * seg:(s + 1) * seg] = y.astype(BF16)
        y = _dot(xb, w_ref[:, kv_b0:])
        for j in range(y.shape[1] // LANES):
            two = y[:, j * LANES:(j + 1) * LANES]
            swapped = pltpu.roll(two, HEAD_DIM, axis=1)
            c0 = kv_b0 + 2 * j * LANES
            o_ref[rows, c0:c0 + LANES] = jnp.where(lo_half, two, swapped).astype(BF16)
            o_ref[rows, c0 + LANES:c0 + 2 * LANES] = jnp.where(lo_half, swapped, two).astype(BF16)


def _inproj(x2, g, w_in, q_segs, seg, kv_b0, casts):
    n, d = x2.shape
    out_cols = kv_b0 + 2 * (w_in.shape[1] - kv_b0)
    steps = n // INPROJ_TILE
    assert n % INPROJ_TILE == 0 and kv_b0 % seg == 0
    cast_in, cast_out, cast_shape = [], [], []
    for w, li in casts:
        _, r, c = w.shape
        assert r % (steps * BF16_ROWS) == 0
        cast_in.append(pl.BlockSpec((None, r // steps, c), lambda i, li=li: (li, i, 0)))
        cast_out.append(pl.BlockSpec((r // steps, c), lambda i: (i, 0)))
        cast_shape.append(jax.ShapeDtypeStruct((r, c), BF16))
    kern = functools.partial(_inproj_kernel, n_cast=len(casts), seg=seg, q_segs=q_segs,
                             q_scale=HEAD_DIM ** -0.5 * LOG2E, kv_b0=kv_b0)
    proj, *cast = pl.pallas_call(
        kern,
        grid=(steps,),
        in_specs=[
            pl.BlockSpec((INPROJ_TILE, d), lambda i: (i, 0)),
            _resident((1, d)),
            _resident(w_in.shape),
        ] + cast_in,
        out_specs=[pl.BlockSpec((INPROJ_TILE, out_cols), lambda i: (i, 0))] + cast_out,
        out_shape=[jax.ShapeDtypeStruct((n, out_cols), BF16)] + cast_shape,
        compiler_params=pltpu.CompilerParams(
            dimension_semantics=("arbitrary",), vmem_limit_bytes=VMEM_LIMIT),
        name="inproj",
    )(x2, g, w_in, *[w for w, _ in casts])
    return proj, cast


def _softmax_pv(s, v_ones, sink):
    m = jnp.max(s, axis=-1, keepdims=True)
    if sink is not None:
        m = jnp.maximum(m, sink)
    pb = jnp.exp2((s - m).astype(BF16))
    if isinstance(v_ones, tuple):
        w1 = v_ones[0].shape[0]
        o = _dot(pb[:, :w1], v_ones[0]) + _dot(pb[:, w1:], v_ones[1])
    else:
        o = _dot(pb, v_ones)
    den = o[:, LANES:]
    if sink is not None:
        den = den + jnp.exp2(sink - m)
    out = o[:, :LANES] * (1.0 / den)
    half = out.shape[0] // 2
    lo_half = lax.broadcasted_iota(jnp.int32, (1, LANES), 1) < HEAD_DIM
    return jnp.where(lo_half, out[:half], out[half:])


def _attn_kernel(sink_ref, qa_ref, kap_ref, kac_ref, vap_ref, vac_ref,
                 qb_ref, kbp_ref, kbc_ref, vbp_ref, vbc_ref,
                 basea_ref, baseb_ref, hida_ref, hidb_ref,
                 o_ref, biasa_ref, biasb_ref, *, tiles_per_seq, pairs_per_kv):
    i = pl.program_id(0)

    @pl.when(i == 0)
    def _():
        for base_ref, hid_ref, bias_ref in ((basea_ref, hida_ref, biasa_ref),
                                            (baseb_ref, hidb_ref, biasb_ref)):
            for v in range(hid_ref.shape[0]):
                for hd in range(base_ref.shape[0]):
                    bias_ref[v, hd] = base_ref[hd] + hid_ref[v]

    oa_ref = o_ref.at[:, :qa_ref.shape[1]]
    ob_ref = o_ref.at[:, qa_ref.shape[1]:]
    first = (i % tiles_per_seq) == 0
    lo_half = lax.broadcasted_iota(jnp.int32, (1, LANES), 1) < HEAD_DIM
    zero = jnp.zeros((), BF16)
    n_head_pairs = qa_ref.shape[1] // LANES

    def lanes(hp):
        return slice(hp * LANES, (hp + 1) * LANES)

    def stacked_q(q_ref, r0, hp):
        q2 = q_ref[r0:r0 + PAIR, lanes(hp)]
        return jnp.concatenate([jnp.where(lo_half, q2, zero), jnp.where(lo_half, zero, q2)],
                               axis=0)

    def with_ones(v):
        return jnp.concatenate([v, jnp.ones(v.shape, BF16)], axis=1)

    def logits_a(jj, hp):
        r0 = jj * PAIR
        q = stacked_q(qa_ref, r0, hp)
        s = jnp.concatenate([_dot_nt(q, kap_ref[r0:, lanes(hp)]),
                             _dot_nt(q, kac_ref[:r0 + PAIR, lanes(hp)])], axis=1)
        bias = biasa_ref[jnp.where(first, jj, PAIRS_PER_TILE), 2 * hp:2 * hp + 2]
        return s + bias.reshape(s.shape)

    def finish_a(jj, hp, s):
        r0 = jj * PAIR
        return _softmax_pv(s, (with_ones(vap_ref[r0:, lanes(hp)]),
                               with_ones(vac_ref[:r0 + PAIR, lanes(hp)])), None)

    def band_b(prev_ref, cur_ref, jj, hp):
        r0 = jj * PAIR
        kv = lanes(hp // pairs_per_kv)
        if jj == 0:
            return jnp.concatenate([prev_ref[:, kv], cur_ref[:PAIR, kv]], axis=0)
        return cur_ref[r0 - PAIR:r0 + PAIR, kv]

    def logits_b(jj, hp):
        s = _dot_nt(stacked_q(qb_ref, jj * PAIR, hp), band_b(kbp_ref, kbc_ref, jj, hp))
        variant = jnp.where(first, 0, 1) if jj == 0 else 1
        return s + biasb_ref[variant, 2 * hp:2 * hp + 2].reshape(s.shape)

    def finish_b(jj, hp, s):
        sink = jnp.concatenate(
            [jnp.full((PAIR, 1), sink_ref[2 * hp + e], F32) for e in range(2)], axis=0)
        return _softmax_pv(s, with_ones(band_b(vbp_ref, vbc_ref, jj, hp)), sink)

    mixers = {"a": (logits_a, finish_a, oa_ref), "b": (logits_b, finish_b, ob_ref)}
    units = [(mx, jj, hp) for hp in range(n_head_pairs) for jj in range(PAIRS_PER_TILE)
             for mx in ("a", "b")]
    pending = {}
    for t in range(len(units) + ATTN_LOOKAHEAD):
        if t < len(units):
            mx, jj, hp = units[t]
            pending[t] = mixers[mx][0](jj, hp)
        if t >= ATTN_LOOKAHEAD:
            mx, jj, hp = units[t - ATTN_LOOKAHEAD]
            out = mixers[mx][1](jj, hp, pending.pop(t - ATTN_LOOKAHEAD))
            mixers[mx][2][jj * PAIR:(jj + 1) * PAIR, lanes(hp)] = out.astype(BF16)


def _attention(proj, sink, bias_a, bias_b, hidden_a, hidden_b, seq_len,
               a_width, b_width, b_kv_width):
    n = proj.shape[0]
    tiles_per_seq = seq_len // ATTN_TILE
    kv_dup = 2 * b_kv_width
    assert a_width == b_width and kv_dup % LANES == 0 and a_width % kv_dup == 0
    sub = ATTN_TILE // PAIR
    qa, ka, va, qb = 0, 1, 2, 3
    kb = 4 * a_width // kv_dup
    vb = kb + 1

    def cur(width, c):
        return pl.BlockSpec((ATTN_TILE, width), lambda i: (i, c))

    def prev(c):
        return pl.BlockSpec((ATTN_TILE, a_width), lambda i: (jnp.maximum(i - 1, 0), c))

    def prev_pair(c):
        return pl.BlockSpec((PAIR, kv_dup), lambda i: (jnp.maximum(i * sub - 1, 0), c))

    kern = functools.partial(_attn_kernel, tiles_per_seq=tiles_per_seq,
                             pairs_per_kv=(b_width // LANES) // (kv_dup // LANES))
    return pl.pallas_call(
        kern,
        grid=(n // ATTN_TILE,),
        in_specs=[
            pl.BlockSpec(memory_space=pltpu.SMEM),
            cur(a_width, qa), prev(ka), cur(a_width, ka), prev(va), cur(a_width, va),
            cur(b_width, qb), prev_pair(kb), cur(kv_dup, kb), prev_pair(vb), cur(kv_dup, vb),
            _resident(bias_a.shape), _resident(bias_b.shape),
            _resident(hidden_a.shape), _resident(hidden_b.shape),
        ],
        out_specs=cur(a_width + b_width, 0),
        out_shape=jax.ShapeDtypeStruct((n, a_width + b_width), BF16),
        scratch_shapes=[pltpu.VMEM((hidden_a.shape[0],) + bias_a.shape, F32),
                        pltpu.VMEM((hidden_b.shape[0],) + bias_b.shape, F32)],
        compiler_params=pltpu.CompilerParams(
            dimension_semantics=("arbitrary",), vmem_limit_bytes=VMEM_LIMIT),
        name="band_attention",
    )(sink, proj, proj, proj, proj, proj, proj, proj, proj, proj, proj,
      bias_a, bias_b, hidden_a, hidden_b)


def _mlp_rows(n_rows, sub_rows, early_fn, h1_fn, o_ref, gpre_ref, wup_ref, wdown_ref,
              gpost_ref, side_jobs=()):
    n_sub = n_rows // sub_rows
    n_chunks = wup_ref.shape[1] // FF_CHUNK
    n_slots = n_sub * n_chunks * 2
    jobs = list(side_jobs)
    job_slot = {round((j + 0.5) * n_slots / len(jobs)): job for j, job in enumerate(jobs)}
    assert len(job_slot) == len(jobs) and all(0 < k <= n_slots for k in job_slot)
    rows = [slice(s * sub_rows, (s + 1) * sub_rows) for s in range(n_sub)]
    early = {0: early_fn(rows[0])} if early_fn else {}
    slot = 0
    for s in range(n_sub):
        if early_fn and s + 1 < n_sub:
            early[s + 1] = early_fn(rows[s + 1])
        h1 = h1_fn(rows[s], early.pop(s, None))
        hn = _rms(h1, gpre_ref[...]).astype(BF16)
        acc = None
        for c in range(n_chunks):
            ff = slice(c * FF_CHUNK, (c + 1) * FF_CHUNK)
            u = jnp.maximum(_dot(hn, wup_ref[:, ff]), 0.0)
            slot += 1
            if slot in job_slot:
                job_slot[slot]()
            part = _dot((u * u).astype(BF16), wdown_ref[ff, :])
            slot += 1
            if slot in job_slot:
                job_slot[slot]()
            acc = part if acc is None else acc + part
        o_ref[rows[s], :] = h1 + _rms(acc, gpost_ref[...])


def _outproj_mlp_kernel(a_ref, x_ref, wout_ref, gpost_ref,
                        gpre_ref, wup_ref, wdown_ref, gmpost_ref, o_ref):
    def out_proj(rows):
        return _dot(a_ref[rows, :], wout_ref[...])

    def h1_fn(rows, y):
        return x_ref[rows, :] + _rms(y, gpost_ref[...])

    _mlp_rows(x_ref.shape[0], OUTPROJ_SUB_ROWS, out_proj, h1_fn, o_ref,
              gpre_ref, wup_ref, wdown_ref, gmpost_ref)


def _outproj_mlp(attn, x2, w_out, g_post, g_mpre, w_up, w_down, g_mpost):
    n, d = x2.shape
    assert n % OUTPROJ_TILE == 0
    row = lambda w: pl.BlockSpec((OUTPROJ_TILE, w), lambda i: (i, 0))
    return pl.pallas_call(
        _outproj_mlp_kernel,
        grid=(n // OUTPROJ_TILE,),
        in_specs=[row(attn.shape[1]), row(d), _resident(w_out.shape), _resident((1, d)),
                  _resident((1, d)), _resident(w_up.shape), _resident(w_down.shape),
                  _resident((1, d))],
        out_specs=row(d),
        out_shape=jax.ShapeDtypeStruct((n, d), F32),
        compiler_params=pltpu.CompilerParams(
            dimension_semantics=("arbitrary",), vmem_limit_bytes=VMEM_LIMIT),
        name="outproj_mlp",
    )(attn, x2, w_out, g_post, g_mpre, w_up, w_down, g_mpost)


def _pool_mixer_jobs(h_ref, halo_fn, g_ref, pw_ref, pscale_ref, t_in_seq, y_ref):
    tm = h_ref.shape[0]
    pc = pw_ref.shape[1]
    st = {}

    def prepare():
        st["hn"] = _rms(h_ref[...], g_ref[...])
        st["halo"] = halo_fn()
        pos1 = t_in_seq * tm + 1 + lax.broadcasted_iota(jnp.int32, (tm, 1), 0)
        st["pos1"] = pos1
        st["inv_pos1"] = 1.0 / pos1.astype(F32)

    def group(gi, w):
        cols = slice(gi * pc, (gi + 1) * pc)
        hn = st["hn"][:, cols]
        wsum = jnp.concatenate([st["halo"][:, cols], hn], axis=0)
        k = 1
        while k < w:
            wsum = wsum + pltpu.roll(wsum, k, axis=0)
            k *= 2
        inv_cnt = jnp.where(st["pos1"] < w, st["inv_pos1"], 1.0 / w)
        d = (wsum[POOL_HALO:, :] * inv_cnt - hn).astype(BF16)
        y_ref[:, cols] = _dot(d, pw_ref[gi]) * pscale_ref[:, cols]

    return [prepare] + [functools.partial(group, gi, w) for gi, w in enumerate(POOL_WINDOWS)]


def _pool_mlp_kernel(h_ref, hnext_ref, gpre_ref, pw_ref, pscale_ref, gpost_ref,
                     gmpre_ref, wup_ref, wdown_ref, gmpost_ref, o_ref, y_ref, *, tiles_per_seq):
    i = pl.program_id(0)
    tm, dm = h_ref.shape
    @pl.when(i == 0)
    def _():
        for job in _pool_mixer_jobs(h_ref, lambda: jnp.zeros((POOL_HALO, dm), F32),
                                    gpre_ref, pw_ref, pscale_ref, 0, y_ref):
            job()

    y = y_ref[...]

    def h1_fn(rows, _):
        return h_ref[rows, :] + _rms(y[rows, :], gpost_ref[...])

    t_next = (i + 1) % tiles_per_seq

    def next_halo():
        keep = jnp.where(t_next == 0, 0.0, 1.0)
        return _rms(h_ref[tm - POOL_HALO:, :], gpre_ref[...]) * keep

    jobs = _pool_mixer_jobs(hnext_ref, next_halo, gpre_ref, pw_ref, pscale_ref, t_next, y_ref)
    _mlp_rows(tm, SUB_ROWS, None, h1_fn, o_ref, gmpre_ref, wup_ref, wdown_ref, gmpost_ref, jobs)


def _pool_mlp(h2, seq_len, g_pre, pool_w, pool_scale, g_post, g_mpre, w_up, w_down, g_mpost):
    n, d = h2.shape
    tiles_per_seq = seq_len // TOKEN_TILE
    n_tiles = n // TOKEN_TILE
    row = pl.BlockSpec((TOKEN_TILE, d), lambda i: (i, 0))
    nxt = pl.BlockSpec((TOKEN_TILE, d), lambda i: (jnp.minimum(i + 1, n_tiles - 1), 0))
    kern = functools.partial(_pool_mlp_kernel, tiles_per_seq=tiles_per_seq)
    return pl.pallas_call(
        kern,
        grid=(n_tiles,),
        in_specs=[row, nxt, _resident((1, d)), _resident(pool_w.shape), _resident((1, d)),
                  _resident((1, d)), _resident((1, d)), _resident(w_up.shape),
                  _resident(w_down.shape), _resident((1, d))],
        out_specs=row,
        out_shape=jax.ShapeDtypeStruct((n, d), F32),
        scratch_shapes=[pltpu.VMEM((TOKEN_TILE, d), F32)],
        compiler_params=pltpu.CompilerParams(
            dimension_semantics=("arbitrary",), vmem_limit_bytes=VMEM_LIMIT),
        name="pool_mlp",
    )(h2, h2, g_pre, pool_w, pool_scale, g_post, g_mpre, w_up, w_down, g_mpost)


def _pair_band_index(n_prev):
    band = (n_prev + 2) * CHUNK
    r = np.arange(PAIR)[:, None]
    k = np.arange(band)[None, :]
    e, qi = r // CHUNK, r % CHUNK
    kc = k // CHUNK - e
    ok = (kc >= 0) & (kc <= n_prev)
    key_pos = kc * CHUNK + k % CHUNK - n_prev * CHUNK
    return qi - key_pos, ok


def _t5_bucket(rel_kq):
    nb = T5_BUCKETS // 2
    ret = (rel_kq > 0).astype(np.int32) * nb
    n = np.abs(rel_kq)
    max_exact = nb // 2
    large = max_exact + (np.log(np.maximum(n, 1) / max_exact)
                         / math.log(T5_MAX_DIST / max_exact) * (nb - max_exact)).astype(np.int32)
    large = np.minimum(large, nb - 1)
    return ret + np.where(n < max_exact, n, large)


def _toeplitz_bias(table, index_of_rel, n_prev):
    band = (n_prev + 2) * CHUNK
    period = band + PAIR
    delta = np.arange(period)
    delta = np.where(delta < band, delta, delta - period)
    diag = jnp.take(table, index_of_rel(n_prev * CHUNK - delta), axis=0).T
    skew = jnp.tile(diag, (1, PAIR))[:, :PAIR * (period - 1)].reshape(-1, PAIR, period - 1)
    bias = skew[:, :, :band]
    _, ok = _pair_band_index(n_prev)
    return jnp.where(ok[None], bias, NEG_INF).astype(F32)


def _bias_tables(relpos_a, t5_table):
    bias_a = _toeplitz_bias(
        relpos_a, lambda rel: np.clip(rel, -A_MAX_REL, A_MAX_REL) + A_MAX_REL, A_PREV_CHUNKS)
    bias_b = _toeplitz_bias(t5_table, lambda rel: _t5_bucket(-rel), B_PREV_CHUNKS)
    kc_a = np.arange(A_BAND) // CHUNK
    hide_a = np.zeros((PAIRS_PER_TILE + 1, 1, 1, A_BAND), bool)
    for jj in range(PAIRS_PER_TILE):
        hide_a[jj, 0, 0] = 2 * jj - A_PREV_CHUNKS + kc_a < 0
    kc_b = np.arange(B_BAND) // CHUNK
    hide_b = np.zeros((2, 1, 1, B_BAND), bool)
    hide_b[0, 0, 0] = kc_b - B_PREV_CHUNKS < 0
    hidden = lambda hide: jnp.asarray(np.where(hide[:, 0], NEG_INF * LOG2E, 0.0), F32)
    return bias_a * LOG2E, bias_b * LOG2E, hidden(hide_a), hidden(hide_b)


def kernel(x, t5_table, e_norm_pre, e_norm_post, e_w_in, e_w_out, e_relpos_a, e_sink_b,
           o_norm_pre, o_norm_post, o_pool_w, o_pool_scale,
           mlp_norm_pre, mlp_norm_post, mlp_w_up, mlp_w_down):
    bsz, seq_len, d = x.shape
    depth = mlp_w_up.shape[0]
    a_heads = e_relpos_a.shape[2]
    b_heads = e_sink_b.shape[1]
    a_width, b_width = a_heads * HEAD_DIM, b_heads * HEAD_DIM
    b_kv_width = (b_heads // B_GROUP) * HEAD_DIM
    assert seq_len % ATTN_TILE == 0 and seq_len % TOKEN_TILE == 0
    assert a_heads == b_heads and a_heads % 2 == 0
    row = lambda v: v.reshape(1, -1).astype(F32)

    h = x.reshape(bsz * seq_len, d)
    pool_w2 = o_pool_w.reshape(o_pool_w.shape[0], -1, o_pool_w.shape[-1])
    names, casts = [], []
    for layer in range(depth):
        names += [("up", layer), ("down", layer)]
        casts += [(mlp_w_up, layer), (mlp_w_down, layer)]
        names.append(("mix", layer))
        casts.append((e_w_out, layer // 2) if layer % 2 == 0 else (pool_w2, layer // 2))
    bf = {}
    for layer in range(depth):
        i = layer // 2
        g_mpre, g_mpost = row(mlp_norm_pre[layer]), row(mlp_norm_post[layer])
        if layer % 2 == 0:
            seg = a_width
            q_segs = (0, 3)
            assert b_width == seg and (2 * b_kv_width) % (2 * LANES) == 0
            proj, cast = _inproj(h, row(e_norm_pre[i]), e_w_in[i].astype(BF16), q_segs, seg,
                                 3 * a_width + b_width, [] if bf else casts)
            bf = bf or dict(zip(names, cast))
            attn = _attention(proj, e_sink_b[i].astype(F32) * LOG2E,
                              *_bias_tables(e_relpos_a[i], t5_table),
                              seq_len, a_width, b_width, b_kv_width)
            h = _outproj_mlp(attn, h, bf["mix", layer], row(e_norm_post[i]),
                             g_mpre, bf["up", layer], bf["down", layer], g_mpost)
        else:
            h = _pool_mlp(h, seq_len, row(o_norm_pre[i]),
                          bf["mix", layer].reshape(o_pool_w.shape[1:]),
                          row(o_pool_scale[i]), row(o_norm_post[i]),
                          g_mpre, bf["up", layer], bf["down", layer], g_mpost)
    return h.reshape(bsz, seq_len, d)
```

```python
import functools
import math

import jax
import jax.numpy as jnp
import numpy as np
from jax import lax
from jax.experimental import pallas as pl
from jax.experimental.pallas import tpu as pltpu

CHUNK = 64
HEAD_DIM = 64
A_PREV_CHUNKS = 8
A_MAX_REL = 128
B_GROUP = 4
B_PREV_CHUNKS = 2
T5_BUCKETS = 32
T5_MAX_DIST = 128
POOL_WINDOWS = (2, 4, 8, 16)
RMS_EPS = 1e-6
NEG_INF = -1e30
LOG2E = math.log2(math.e)

LANES = 128
BF16_ROWS = 16
V7X_VMEM_BYTES = 64 * 1024 * 1024

TOKEN_TILE = 1024
INPROJ_TILE = 1024
OUTPROJ_TILE = 1024
SUB_ROWS = 512
OUTPROJ_SUB_ROWS = 1024
ATTN_TILE = 512
PAIR = 2 * CHUNK
PAIRS_PER_TILE = ATTN_TILE // PAIR
A_BAND = (A_PREV_CHUNKS + 2) * CHUNK
B_BAND = (B_PREV_CHUNKS + 2) * CHUNK
ATTN_LOOKAHEAD = 2
POOL_HALO = 16
FF_CHUNK = 2048
VMEM_LIMIT = 56 * 1024 * 1024

BF16 = jnp.bfloat16
F32 = jnp.float32


def _rms(x, g):
    ms = jnp.mean(x * x, axis=-1, keepdims=True)
    return (x * lax.rsqrt(ms + RMS_EPS)) * g


def _dot(a, b):
    return jnp.dot(a, b, preferred_element_type=F32)


def _dot_nt(a, b):
    return lax.dot_general(a, b, (((1,), (1,)), ((), ())), preferred_element_type=F32)


def _resident(shape):
    nd = len(shape)
    return pl.BlockSpec(shape, lambda *_: (0,) * nd, pipeline_mode=pl.Buffered(1))


def _inproj_kernel(x_ref, g_ref, w_ref, *refs, n_cast, seg, q_segs, q_scale, kv_b0):
    o_ref = refs[n_cast]
    for src_ref, dst_ref in zip(refs[:n_cast], refs[n_cast + 1:]):
        dst_ref[...] = src_ref[...].astype(BF16)
    lo_half = lax.broadcasted_iota(jnp.int32, (1, LANES), 1) < HEAD_DIM
    for r in range(x_ref.shape[0] // SUB_ROWS):
        rows = slice(r * SUB_ROWS, (r + 1) * SUB_ROWS)
        xb = _rms(x_ref[rows, :], g_ref[...]).astype(BF16)
        for s in range(kv_b0 // seg):
            y = _dot(xb, w_ref[:, s * seg:(s + 1) * seg])
            if s in q_segs:
                y = y * q_scale
            o_ref[rows, s * seg:(s + 1) * seg] = y.astype(BF16)
        y = _dot(xb, w_ref[:, kv_b0:])
        for j in range(y.shape[1] // LANES):
            two = y[:, j * LANES:(j + 1) * LANES]
            swapped = pltpu.roll(two, HEAD_DIM, axis=1)
            c0 = kv_b0 + 2 * j * LANES
            o_ref[rows, c0:c0 + LANES] = jnp.where(lo_half, two, swapped).astype(BF16)
            o_ref[rows, c0 + LANES:c0 + 2 * LANES] = jnp.where(lo_half, swapped, two).astype(BF16)


def _inproj(x2, g, w_in, q_segs, seg, kv_b0, casts):
    n, d = x2.shape
    out_cols = kv_b0 + 2 * (w_in.shape[1] - kv_b0)
    steps = n // INPROJ_TILE
    assert n % INPROJ_TILE == 0 and kv_b0 % seg == 0
    cast_in, cast_out, cast_shape = [], [], []
    for w, li in casts:
        _, r, c = w.shape
        assert r % (steps * BF16_ROWS) == 0
        cast_in.append(pl.BlockSpec((None, r // steps, c), lambda i, li=li: (li, i, 0)))
        cast_out.append(pl.BlockSpec((r // steps, c), lambda i: (i, 0)))
        cast_shape.append(jax.ShapeDtypeStruct((r, c), BF16))
    kern = functools.partial(_inproj_kernel, n_cast=len(casts), seg=seg, q_segs=q_segs,
                             q_scale=HEAD_DIM ** -0.5 * LOG2E, kv_b0=kv_b0)
    proj, *cast = pl.pallas_call(
        kern,
        grid=(steps,),
        in_specs=[
            pl.BlockSpec((INPROJ_TILE, d), lambda i: (i, 0)),
            _resident((1, d)),
            _resident(w_in.shape),
        ] + cast_in,
        out_specs=[pl.BlockSpec((INPROJ_TILE, out_cols), lambda i: (i, 0))] + cast_out,
        out_shape=[jax.ShapeDtypeStruct((n, out_cols), BF16)] + cast_shape,
        compiler_params=pltpu.CompilerParams(
            dimension_semantics=("arbitrary",), vmem_limit_bytes=VMEM_LIMIT),
        name="inproj",
    )(x2, g, w_in, *[w for w, _ in casts])
    return proj, cast


def _softmax_pv(s, v_ones, sink):
    m = jnp.max(s, axis=-1, keepdims=True)
    if sink is not None:
        m = jnp.maximum(m, sink)
    pb = jnp.exp2((s - m).astype(BF16))
    if isinstance(v_ones, tuple):
        w1 = v_ones[0].shape[0]
        o = _dot(pb[:, :w1], v_ones[0]) + _dot(pb[:, w1:], v_ones[1])
    else:
        o = _dot(pb, v_ones)
    den = o[:, LANES:]
    if sink is not None:
        den = den + jnp.exp2(sink - m)
    out = o[:, :LANES] * (1.0 / den)
    half = out.shape[0] // 2
    lo_half = lax.broadcasted_iota(jnp.int32, (1, LANES), 1) < HEAD_DIM
    return jnp.where(lo_half, out[:half], out[half:])


def _attn_kernel(sink_ref, qa_ref, kap_ref, kac_ref, vap_ref, vac_ref,
                 qb_ref, kbp_ref, kbc_ref, vbp_ref, vbc_ref,
                 basea_ref, baseb_ref, hida_ref, hidb_ref,
                 o_ref, biasa_ref, biasb_ref, *, tiles_per_seq, pairs_per_kv):
    i = pl.program_id(0)

    @pl.when(i == 0)
    def _():
        for base_ref, hid_ref, bias_ref in ((basea_ref, hida_ref, biasa_ref),
                                            (baseb_ref, hidb_ref, biasb_ref)):
            for v in range(hid_ref.shape[0]):
                for hd in range(base_ref.shape[0]):
                    bias_ref[v, hd] = base_ref[hd] + hid_ref[v]

    oa_ref = o_ref.at[:, :qa_ref.shape[1]]
    ob_ref = o_ref.at[:, qa_ref.shape[1]:]
    first = (i % tiles_per_seq) == 0
    lo_half = lax.broadcasted_iota(jnp.int32, (1, LANES), 1) < HEAD_DIM
    zero = jnp.zeros((), BF16)
    n_head_pairs = qa_ref.shape[1] // LANES

    def lanes(hp):
        return slice(hp * LANES, (hp + 1) * LANES)

    def stacked_q(q_ref, r0, hp):
        q2 = q_ref[r0:r0 + PAIR, lanes(hp)]
        return jnp.concatenate([jnp.where(lo_half, q2, zero), jnp.where(lo_half, zero, q2)],
                               axis=0)

    def with_ones(v):
        return jnp.concatenate([v, jnp.ones(v.shape, BF16)], axis=1)

    def logits_a(jj, hp):
        r0 = jj * PAIR
        q = stacked_q(qa_ref, r0, hp)
        s = jnp.concatenate([_dot_nt(q, kap_ref[r0:, lanes(hp)]),
                             _dot_nt(q, kac_ref[:r0 + PAIR, lanes(hp)])], axis=1)
        bias = biasa_ref[jnp.where(first, jj, PAIRS_PER_TILE), 2 * hp:2 * hp + 2]
        return s + bias.reshape(s.shape)

    def finish_a(jj, hp, s):
        r0 = jj * PAIR
        return _softmax_pv(s, (with_ones(vap_ref[r0:, lanes(hp)]),
                               with_ones(vac_ref[:r0 + PAIR, lanes(hp)])), None)

    def band_b(prev_ref, cur_ref, jj, hp):
        r0 = jj * PAIR
        kv = lanes(hp // pairs_per_kv)
        if jj == 0:
            return jnp.concatenate([prev_ref[:, kv], cur_ref[:PAIR, kv]], axis=0)
        return cur_ref[r0 - PAIR:r0 + PAIR, kv]

    def logits_b(jj, hp):
        s = _dot_nt(stacked_q(qb_ref, jj * PAIR, hp), band_b(kbp_ref, kbc_ref, jj, hp))
        variant = jnp.where(first, 0, 1) if jj == 0 else 1
        return s + biasb_ref[variant, 2 * hp:2 * hp + 2].reshape(s.shape)

    def finish_b(jj, hp, s):
        sink = jnp.concatenate(
            [jnp.full((PAIR, 1), sink_ref[2 * hp + e], F32) for e in range(2)], axis=0)
        return _softmax_pv(s, with_ones(band_b(vbp_ref, vbc_ref, jj, hp)), sink)

    mixers = {"a": (logits_a, finish_a, oa_ref), "b": (logits_b, finish_b, ob_ref)}
    units = [(mx, jj, hp) for hp in range(n_head_pairs) for jj in range(PAIRS_PER_TILE)
             for mx in ("a", "b")]
    pending = {}
    for t in range(len(units) + ATTN_LOOKAHEAD):
        if t < len(units):
            mx, jj, hp = units[t]
            pending[t] = mixers[mx][0](jj, hp)
        if t >= ATTN_LOOKAHEAD:
            mx, jj, hp = units[t - ATTN_LOOKAHEAD]
            out = mixers[mx][1](jj, hp, pending.pop(t - ATTN_LOOKAHEAD))
            mixers[mx][2][jj * PAIR:(jj + 1) * PAIR, lanes(hp)] = out.astype(BF16)


def _attention(proj, sink, bias_a, bias_b, hidden_a, hidden_b, seq_len,
               a_width, b_width, b_kv_width):
    n = proj.shape[0]
    tiles_per_seq = seq_len // ATTN_TILE
    kv_dup = 2 * b_kv_width
    assert a_width == b_width and kv_dup % LANES == 0 and a_width % kv_dup == 0
    sub = ATTN_TILE // PAIR
    qa, ka, va, qb = 0, 1, 2, 3
    kb = 4 * a_width // kv_dup
    vb = kb + 1

    def cur(width, c):
        return pl.BlockSpec((ATTN_TILE, width), lambda i: (i, c))

    def prev(c):
        return pl.BlockSpec((ATTN_TILE, a_width), lambda i: (jnp.maximum(i - 1, 0), c))

    def prev_pair(c):
        return pl.BlockSpec((PAIR, kv_dup), lambda i: (jnp.maximum(i * sub - 1, 0), c))

    kern = functools.partial(_attn_kernel, tiles_per_seq=tiles_per_seq,
                             pairs_per_kv=(b_width // LANES) // (kv_dup // LANES))
    return pl.pallas_call(
        kern,
        grid=(n // ATTN_TILE,),
        in_specs=[
            pl.BlockSpec(memory_space=pltpu.SMEM),
            cur(a_width, qa), prev(ka), cur(a_width, ka), prev(va), cur(a_width, va),
            cur(b_width, qb), prev_pair(kb), cur(kv_dup, kb), prev_pair(vb), cur(kv_dup, vb),
            _resident(bias_a.shape), _resident(bias_b.shape),
            _resident(hidden_a.shape), _resident(hidden_b.shape),
        ],
        out_specs=cur(a_width + b_width, 0),
        out_shape=jax.ShapeDtypeStruct((n, a_width + b_width), BF16),
        scratch_shapes=[pltpu.VMEM((hidden_a.shape[0],) + bias_a.shape, F32),
                        pltpu.VMEM((hidden_b.shape[0],) + bias_b.shape, F32)],
        compiler_params=pltpu.CompilerParams(
            dimension_semantics=("arbitrary",), vmem_limit_bytes=VMEM_LIMIT),
        name="band_attention",
    )(sink, proj, proj, proj, proj, proj, proj, proj, proj, proj, proj,
      bias_a, bias_b, hidden_a, hidden_b)


def _mlp_rows(n_rows, sub_rows, early_fn, h1_fn, o_ref, gpre_ref, wup_ref, wdown_ref,
              gpost_ref, side_jobs=()):
    n_sub = n_rows // sub_rows
    n_chunks = wup_ref.shape[1] // FF_CHUNK
    n_slots = n_sub * n_chunks * 2
    jobs = list(side_jobs)
    job_slot = {round((j + 0.5) * n_slots / len(jobs)): job for j, job in enumerate(jobs)}
    assert len(job_slot) == len(jobs) and all(0 < k <= n_slots for k in job_slot)
    rows = [slice(s * sub_rows, (s + 1) * sub_rows) for s in range(n_sub)]
    early = {0: early_fn(rows[0])} if early_fn else {}
    slot = 0
    for s in range(n_sub):
        if early_fn and s + 1 < n_sub:
            early[s + 1] = early_fn(rows[s + 1])
        h1 = h1_fn(rows[s], early.pop(s, None))
        hn = _rms(h1, gpre_ref[...]).astype(BF16)
        acc = None
        for c in range(n_chunks):
            ff = slice(c * FF_CHUNK, (c + 1) * FF_CHUNK)
            u = jnp.maximum(_dot(hn, wup_ref[:, ff]), 0.0)
            slot += 1
            if slot in job_slot:
                job_slot[slot]()
            part = _dot((u * u).astype(BF16), wdown_ref[ff, :])
            slot += 1
            if slot in job_slot:
                job_slot[slot]()
            acc = part if acc is None else acc + part
        o_ref[rows[s], :] = h1 + _rms(acc, gpost_ref[...])


def _outproj_mlp_kernel(a_ref, x_ref, wout_ref, gpost_ref,
                        gpre_ref, wup_ref, wdown_ref, gmpost_ref, o_ref):
    def out_proj(rows):
        return _dot(a_ref[rows, :], wout_ref[...])

    def h1_fn(rows, y):
        return x_ref[rows, :] + _rms(y, gpost_ref[...])

    _mlp_rows(x_ref.shape[0], OUTPROJ_SUB_ROWS, out_proj, h1_fn, o_ref,
              gpre_ref, wup_ref, wdown_ref, gmpost_ref)


def _outproj_mlp(attn, x2, w_out, g_post, g_mpre, w_up, w_down, g_mpost):
    n, d = x2.shape
    assert n % OUTPROJ_TILE == 0
    row = lambda w: pl.BlockSpec((OUTPROJ_TILE, w), lambda i: (i, 0))
    return pl.pallas_call(
        _outproj_mlp_kernel,
        grid=(n // OUTPROJ_TILE,),
        in_specs=[row(attn.shape[1]), row(d), _resident(w_out.shape), _resident((1, d)),
                  _resident((1, d)), _resident(w_up.shape), _resident(w_down.shape),
                  _resident((1, d))],
        out_specs=row(d),
        out_shape=jax.ShapeDtypeStruct((n, d), F32),
        compiler_params=pltpu.CompilerParams(
            dimension_semantics=("arbitrary",), vmem_limit_bytes=VMEM_LIMIT),
        name="outproj_mlp",
    )(attn, x2, w_out, g_post, g_mpre, w_up, w_down, g_mpost)


def _pool_mixer_jobs(h_ref, halo_fn, g_ref, pw_ref, pscale_ref, t_in_seq, y_ref):
    tm = h_ref.shape[0]
    pc = pw_ref.shape[1]
    st = {}

    def prepare():
        st["hn"] = _rms(h_ref[...], g_ref[...])
        st["halo"] = halo_fn()
        pos1 = t_in_seq * tm + 1 + lax.broadcasted_iota(jnp.int32, (tm, 1), 0)
        st["pos1"] = pos1
        st["inv_pos1"] = 1.0 / pos1.astype(F32)

    def group(gi, w):
        cols = slice(gi * pc, (gi + 1) * pc)
        hn = st["hn"][:, cols]
        wsum = jnp.concatenate([st["halo"][:, cols], hn], axis=0)
        k = 1
        while k < w:
            wsum = wsum + pltpu.roll(wsum, k, axis=0)
            k *= 2
        inv_cnt = jnp.where(st["pos1"] < w, st["inv_pos1"], 1.0 / w)
        d = (wsum[POOL_HALO:, :] * inv_cnt - hn).astype(BF16)
        y_ref[:, cols] = _dot(d, pw_ref[gi]) * pscale_ref[:, cols]

    return [prepare] + [functools.partial(group, gi, w) for gi, w in enumerate(POOL_WINDOWS)]


def _pool_mlp_kernel(h_ref, hnext_ref, gpre_ref, pw_ref, pscale_ref, gpost_ref,
                     gmpre_ref, wup_ref, wdown_ref, gmpost_ref, o_ref, y_ref, *, tiles_per_seq):
    i = pl.program_id(0)
    tm, dm = h_ref.shape
    @pl.when(i == 0)
    def _():
        for job in _pool_mixer_jobs(h_ref, lambda: jnp.zeros((POOL_HALO, dm), F32),
                                    gpre_ref, pw_ref, pscale_ref, 0, y_ref):
            job()

    y = y_ref[...]

    def h1_fn(rows, _):
        return h_ref[rows, :] + _rms(y[rows, :], gpost_ref[...])

    t_next = (i + 1) % tiles_per_seq

    def next_halo():
        keep = jnp.where(t_next == 0, 0.0, 1.0)
        return _rms(h_ref[tm - POOL_HALO:, :], gpre_ref[...]) * keep

    jobs = _pool_mixer_jobs(hnext_ref, next_halo, gpre_ref, pw_ref, pscale_ref, t_next, y_ref)
    _mlp_rows(tm, SUB_ROWS, None, h1_fn, o_ref, gmpre_ref, wup_ref, wdown_ref, gmpost_ref, jobs)


def _pool_mlp(h2, seq_len, g_pre, pool_w, pool_scale, g_post, g_mpre, w_up, w_down, g_mpost):
    n, d = h2.shape
    tiles_per_seq = seq_len // TOKEN_TILE
    n_tiles = n // TOKEN_TILE
    row = pl.BlockSpec((TOKEN_TILE, d), lambda i: (i, 0))
    nxt = pl.BlockSpec((TOKEN_TILE, d), lambda i: (jnp.minimum(i + 1, n_tiles - 1), 0))
    kern = functools.partial(_pool_mlp_kernel, tiles_per_seq=tiles_per_seq)
    return pl.pallas_call(
        kern,
        grid=(n_tiles,),
        in_specs=[row, nxt, _resident((1, d)), _resident(pool_w.shape), _resident((1, d)),
                  _resident((1, d)), _resident((1, d)), _resident(w_up.shape),
                  _resident(w_down.shape), _resident((1, d))],
        out_specs=row,
        out_shape=jax.ShapeDtypeStruct((n, d), F32),
        scratch_shapes=[pltpu.VMEM((TOKEN_TILE, d), F32)],
        compiler_params=pltpu.CompilerParams(
            dimension_semantics=("arbitrary",), vmem_limit_bytes=VMEM_LIMIT),
        name="pool_mlp",
    )(h2, h2, g_pre, pool_w, pool_scale, g_post, g_mpre, w_up, w_down, g_mpost)


def _pair_band_index(n_prev):
    band = (n_prev + 2) * CHUNK
    r = np.arange(PAIR)[:, None]
    k = np.arange(band)[None, :]
    e, qi = r // CHUNK, r % CHUNK
    kc = k // CHUNK - e
    ok = (kc >= 0) & (kc <= n_prev)
    key_pos = kc * CHUNK + k % CHUNK - n_prev * CHUNK
    return qi - key_pos, ok


def _t5_bucket(rel_kq):
    nb = T5_BUCKETS // 2
    ret = (rel_kq > 0).astype(np.int32) * nb
    n = np.abs(rel_kq)
    max_exact = nb // 2
    large = max_exact + (np.log(np.maximum(n, 1) / max_exact)
                         / math.log(T5_MAX_DIST / max_exact) * (nb - max_exact)).astype(np.int32)
    large = np.minimum(large, nb - 1)
    return ret + np.where(n < max_exact, n, large)


def _toeplitz_bias(table, index_of_rel, n_prev):
    band = (n_prev + 2) * CHUNK
    period = band + PAIR
    delta = np.arange(period)
    delta = np.where(delta < band, delta, delta - period)
    diag = jnp.take(table, index_of_rel(n_prev * CHUNK - delta), axis=0).T
    skew = jnp.tile(diag, (1, PAIR))[:, :PAIR * (period - 1)].reshape(-1, PAIR, period - 1)
    bias = skew[:, :, :band]
    _, ok = _pair_band_index(n_prev)
    return jnp.where(ok[None], bias, NEG_INF).astype(F32)


def _bias_tables(relpos_a, t5_table):
    bias_a = _toeplitz_bias(
        relpos_a, lambda rel: np.clip(rel, -A_MAX_REL, A_MAX_REL) + A_MAX_REL, A_PREV_CHUNKS)
    bias_b = _toeplitz_bias(t5_table, lambda rel: _t5_bucket(-rel), B_PREV_CHUNKS)
    kc_a = np.arange(A_BAND) // CHUNK
    hide_a = np.zeros((PAIRS_PER_TILE + 1, 1, 1, A_BAND), bool)
    for jj in range(PAIRS_PER_TILE):
        hide_a[jj, 0, 0] = 2 * jj - A_PREV_CHUNKS + kc_a < 0
    kc_b = np.arange(B_BAND) // CHUNK
    hide_b = np.zeros((2, 1, 1, B_BAND), bool)
    hide_b[0, 0, 0] = kc_b - B_PREV_CHUNKS < 0
    hidden = lambda hide: jnp.asarray(np.where(hide[:, 0], NEG_INF * LOG2E, 0.0), F32)
    return bias_a * LOG2E, bias_b * LOG2E, hidden(hide_a), hidden(hide_b)


def kernel(x, t5_table, e_norm_pre, e_norm_post, e_w_in, e_w_out, e_relpos_a, e_sink_b,
           o_norm_pre, o_norm_post, o_pool_w, o_pool_scale,
           mlp_norm_pre, mlp_norm_post, mlp_w_up, mlp_w_down):
    bsz, seq_len, d = x.shape
    depth = mlp_w_up.shape[0]
    a_heads = e_relpos_a.shape[2]
    b_heads = e_sink_b.shape[1]
    a_width, b_width = a_heads * HEAD_DIM, b_heads * HEAD_DIM
    b_kv_width = (b_heads // B_GROUP) * HEAD_DIM
    assert seq_len % ATTN_TILE == 0 and seq_len % TOKEN_TILE == 0
    assert a_heads == b_heads and a_heads % 2 == 0
    row = lambda v: v.reshape(1, -1).astype(F32)

    h = x.reshape(bsz * seq_len, d)
    pool_w2 = o_pool_w.reshape(o_pool_w.shape[0], -1, o_pool_w.shape[-1])
    names, casts = [], []
    for layer in range(depth):
        names += [("up", layer), ("down", layer)]
        casts += [(mlp_w_up, layer), (mlp_w_down, layer)]
        names.append(("mix", layer))
        casts.append((e_w_out, layer // 2) if layer % 2 == 0 else (pool_w2, layer // 2))
    bf = {}
    for layer in range(depth):
        i = layer // 2
        g_mpre, g_mpost = row(mlp_norm_pre[layer]), row(mlp_norm_post[layer])
        if layer % 2 == 0:
            seg = a_width
            q_segs = (0, 3)
            assert b_width == seg and (2 * b_kv_width) % (2 * LANES) == 0
            proj, cast = _inproj(h, row(e_norm_pre[i]), e_w_in[i].astype(BF16), q_segs, seg,
                                 3 * a_width + b_width, [] if bf else casts)
            bf = bf or dict(zip(names, cast))
            attn = _attention(proj, e_sink_b[i].astype(F32) * LOG2E,
                              *_bias_tables(e_relpos_a[i], t5_table),
                              seq_len, a_width, b_width, b_kv_width)
            h = _outproj_mlp(attn, h, bf["mix", layer], row(e_norm_post[i]),
                             g_mpre, bf["up", layer], bf["down", layer], g_mpost)
        else:
            h = _pool_mlp(h, seq_len, row(o_norm_pre[i]),
                          bf["mix", layer].reshape(o_pool_w.shape[1:]),
                          row(o_pool_scale[i]), row(o_norm_post[i]),
                          g_mpre, bf["up", layer], bf["down", layer], g_mpost)
    return h.reshape(bsz, seq_len, d)
```

```python
import functools
import math

import jax
import jax.numpy as jnp
import numpy as np
from jax import lax
from jax.experimental import pallas as pl
from jax.experimental.pallas import tpu as pltpu

CHUNK = 64
HEAD_DIM = 64
A_PREV_CHUNKS = 8
A_MAX_REL = 128
B_GROUP = 4
B_PREV_CHUNKS = 2
T5_BUCKETS = 32
T5_MAX_DIST = 128
POOL_WINDOWS = (2, 4, 8, 16)
RMS_EPS = 1e-6
NEG_INF = -1e30
LOG2E = math.log2(math.e)

LANES = 128
BF16_ROWS = 16
V7X_VMEM_BYTES = 64 * 1024 * 1024

TOKEN_TILE = 1024
INPROJ_TILE = 1024
OUTPROJ_TILE = 1024
SUB_ROWS = 512
OUTPROJ_SUB_ROWS = 512
ATTN_TILE = 512
PAIR = 2 * CHUNK
PAIRS_PER_TILE = ATTN_TILE // PAIR
A_BAND = (A_PREV_CHUNKS + 2) * CHUNK
B_BAND = (B_PREV_CHUNKS + 2) * CHUNK
ATTN_LOOKAHEAD = 2
POOL_HALO = 16
FF_CHUNK = 2048
VMEM_LIMIT = 56 * 1024 * 1024

BF16 = jnp.bfloat16
F32 = jnp.float32


def _rms(x, g):
    ms = jnp.mean(x * x, axis=-1, keepdims=True)
    return (x * lax.rsqrt(ms + RMS_EPS)) * g


def _dot(a, b):
    return jnp.dot(a, b, preferred_element_type=F32)


def _dot_nt(a, b):
    return lax.dot_general(a, b, (((1,), (1,)), ((), ())), preferred_element_type=F32)


def _resident(shape):
    nd = len(shape)
    return pl.BlockSpec(shape, lambda *_: (0,) * nd, pipeline_mode=pl.Buffered(1))


def _inproj_kernel(x_ref, g_ref, w_ref, *refs, n_cast, seg, q_segs, q_scale, kv_b0):
    o_ref = refs[n_cast]
    for src_ref, dst_ref in zip(refs[:n_cast], refs[n_cast + 1:]):
        dst_ref[...] = src_ref[...].astype(BF16)
    lo_half = lax.broadcasted_iota(jnp.int32, (1, LANES), 1) < HEAD_DIM
    for r in range(x_ref.shape[0] // SUB_ROWS):
        rows = slice(r * SUB_ROWS, (r + 1) * SUB_ROWS)
        xb = _rms(x_ref[rows, :], g_ref[...]).astype(BF16)
        for s in range(kv_b0 // seg):
            y = _dot(xb, w_ref[:, s * seg:(s + 1) * seg])
            if s in q_segs:
                y = y * q_scale
            o_ref[rows, s * seg:(s + 1) * seg] = y.astype(BF16)
        y = _dot(xb, w_ref[:, kv_b0:])
        for j in range(y.shape[1] // LANES):
            two = y[:, j * LANES:(j + 1) * LANES]
            swapped = pltpu.roll(two, HEAD_DIM, axis=1)
            c0 = kv_b0 + 2 * j * LANES
            o_ref[rows, c0:c0 + LANES] = jnp.where(lo_half, two, swapped).astype(BF16)
            o_ref[rows, c0 + LANES:c0 + 2 * LANES] = jnp.where(lo_half, swapped, two).astype(BF16)


def _inproj(x2, g, w_in, q_segs, seg, kv_b0, casts):
    n, d = x2.shape
    out_cols = kv_b0 + 2 * (w_in.shape[1] - kv_b0)
    steps = n // INPROJ_TILE
    assert n % INPROJ_TILE == 0 and kv_b0 % seg == 0
    cast_in, cast_out, cast_shape = [], [], []
    for w, li in casts:
        _, r, c = w.shape
        assert r % (steps * BF16_ROWS) == 0
        cast_in.append(pl.BlockSpec((None, r // steps, c), lambda i, li=li: (li, i, 0)))
        cast_out.append(pl.BlockSpec((r // steps, c), lambda i: (i, 0)))
        cast_shape.append(jax.ShapeDtypeStruct((r, c), BF16))
    kern = functools.partial(_inproj_kernel, n_cast=len(casts), seg=seg, q_segs=q_segs,
                             q_scale=HEAD_DIM ** -0.5 * LOG2E, kv_b0=kv_b0)
    proj, *cast = pl.pallas_call(
        kern,
        grid=(steps,),
        in_specs=[
            pl.BlockSpec((INPROJ_TILE, d), lambda i: (i, 0)),
            _resident((1, d)),
            _resident(w_in.shape),
        ] + cast_in,
        out_specs=[pl.BlockSpec((INPROJ_TILE, out_cols), lambda i: (i, 0))] + cast_out,
        out_shape=[jax.ShapeDtypeStruct((n, out_cols), BF16)] + cast_shape,
        compiler_params=pltpu.CompilerParams(
            dimension_semantics=("arbitrary",), vmem_limit_bytes=VMEM_LIMIT),
        name="inproj",
    )(x2, g, w_in, *[w for w, _ in casts])
    return proj, cast


def _softmax_pv(s, v_ones, sink):
    m = jnp.max(s, axis=-1, keepdims=True)
    if sink is not None:
        m = jnp.maximum(m, sink)
    pb = jnp.exp2((s - m).astype(BF16))
    if isinstance(v_ones, tuple):
        w1 = v_ones[0].shape[0]
        o = _dot(pb[:, :w1], v_ones[0]) + _dot(pb[:, w1:], v_ones[1])
    else:
        o = _dot(pb, v_ones)
    den = o[:, LANES:]
    if sink is not None:
        den = den + jnp.exp2(sink - m)
    out = o[:, :LANES] * (1.0 / den)
    half = out.shape[0] // 2
    lo_half = lax.broadcasted_iota(jnp.int32, (1, LANES), 1) < HEAD_DIM
    return jnp.where(lo_half, out[:half], out[half:])


def _attn_kernel(sink_ref, qa_ref, kap_ref, kac_ref, vap_ref, vac_ref,
                 qb_ref, kbp_ref, kbc_ref, vbp_ref, vbc_ref,
                 diaga_ref, cornera_ref, hida_ref, diagb_ref, cornerb_ref, hidb_ref,
                 o_ref, biasa_ref, biasb_ref, *, tiles_per_seq, pairs_per_kv):
    i = pl.program_id(0)

    @pl.when(i == 0)
    def _():
        for diag_ref, corner_ref, hid_ref, bias_ref in (
                (diaga_ref, cornera_ref, hida_ref, biasa_ref),
                (diagb_ref, cornerb_ref, hidb_ref, biasb_ref)):
            rows, band = corner_ref.shape
            for hd in range(diag_ref.shape[0]):
                diag = jnp.broadcast_to(diag_ref[hd:hd + 1, :], (rows, diag_ref.shape[1]))
                skew = pltpu.roll(diag, 0, axis=1, stride=1, stride_axis=0)
                base = skew[:, :band] + corner_ref[...]
                for v in range(hid_ref.shape[0]):
                    bias_ref[v, hd] = base + hid_ref[v]

    oa_ref = o_ref.at[:, :qa_ref.shape[1]]
    ob_ref = o_ref.at[:, qa_ref.shape[1]:]
    first = (i % tiles_per_seq) == 0
    lo_half = lax.broadcasted_iota(jnp.int32, (1, LANES), 1) < HEAD_DIM
    zero = jnp.zeros((), BF16)
    n_head_pairs = qa_ref.shape[1] // LANES

    def lanes(hp):
        return slice(hp * LANES, (hp + 1) * LANES)

    def stacked_q(q_ref, r0, hp):
        q2 = q_ref[r0:r0 + PAIR, lanes(hp)]
        return jnp.concatenate([jnp.where(lo_half, q2, zero), jnp.where(lo_half, zero, q2)],
                               axis=0)

    def with_ones(v):
        return jnp.concatenate([v, jnp.ones(v.shape, BF16)], axis=1)

    def logits_a(jj, hp):
        r0 = jj * PAIR
        q = stacked_q(qa_ref, r0, hp)
        s = jnp.concatenate([_dot_nt(q, kap_ref[r0:, lanes(hp)]),
                             _dot_nt(q, kac_ref[:r0 + PAIR, lanes(hp)])], axis=1)
        bias = biasa_ref[jnp.where(first, jj, PAIRS_PER_TILE), 2 * hp:2 * hp + 2]
        return s + bias.reshape(s.shape)

    def finish_a(jj, hp, s):
        r0 = jj * PAIR
        return _softmax_pv(s, (with_ones(vap_ref[r0:, lanes(hp)]),
                               with_ones(vac_ref[:r0 + PAIR, lanes(hp)])), None)

    def band_b(prev_ref, cur_ref, jj, hp):
        r0 = jj * PAIR
        kv = lanes(hp // pairs_per_kv)
        if jj == 0:
            return jnp.concatenate([prev_ref[:, kv], cur_ref[:PAIR, kv]], axis=0)
        return cur_ref[r0 - PAIR:r0 + PAIR, kv]

    def logits_b(jj, hp):
        s = _dot_nt(stacked_q(qb_ref, jj * PAIR, hp), band_b(kbp_ref, kbc_ref, jj, hp))
        variant = jnp.where(first, 0, 1) if jj == 0 else 1
        return s + biasb_ref[variant, 2 * hp:2 * hp + 2].reshape(s.shape)

    def finish_b(jj, hp, s):
        sink = jnp.concatenate(
            [jnp.full((PAIR, 1), sink_ref[2 * hp + e], F32) for e in range(2)], axis=0)
        return _softmax_pv(s, with_ones(band_b(vbp_ref, vbc_ref, jj, hp)), sink)

    mixers = {"a": (logits_a, finish_a, oa_ref), "b": (logits_b, finish_b, ob_ref)}
    units = [(mx, jj, hp) for hp in range(n_head_pairs) for jj in range(PAIRS_PER_TILE)
             for mx in ("a", "b")]
    pending = {}
    for t in range(len(units) + ATTN_LOOKAHEAD):
        if t < len(units):
            mx, jj, hp = units[t]
            pending[t] = mixers[mx][0](jj, hp)
        if t >= ATTN_LOOKAHEAD:
            mx, jj, hp = units[t - ATTN_LOOKAHEAD]
            out = mixers[mx][1](jj, hp, pending.pop(t - ATTN_LOOKAHEAD))
            mixers[mx][2][jj * PAIR:(jj + 1) * PAIR, lanes(hp)] = out.astype(BF16)


def _attention(proj, sink, bias_a, bias_b, seq_len, a_width, b_width, b_kv_width):
    n = proj.shape[0]
    tiles_per_seq = seq_len // ATTN_TILE
    kv_dup = 2 * b_kv_width
    assert a_width == b_width and kv_dup % LANES == 0 and a_width % kv_dup == 0
    sub = ATTN_TILE // PAIR
    qa, ka, va, qb = 0, 1, 2, 3
    kb = 4 * a_width // kv_dup
    vb = kb + 1

    def cur(width, c):
        return pl.BlockSpec((ATTN_TILE, width), lambda i: (i, c))

    def prev(c):
        return pl.BlockSpec((ATTN_TILE, a_width), lambda i: (jnp.maximum(i - 1, 0), c))

    def prev_pair(c):
        return pl.BlockSpec((PAIR, kv_dup), lambda i: (jnp.maximum(i * sub - 1, 0), c))

    kern = functools.partial(_attn_kernel, tiles_per_seq=tiles_per_seq,
                             pairs_per_kv=(b_width // LANES) // (kv_dup // LANES))
    return pl.pallas_call(
        kern,
        grid=(n // ATTN_TILE,),
        in_specs=[
            pl.BlockSpec(memory_space=pltpu.SMEM),
            cur(a_width, qa), prev(ka), cur(a_width, ka), prev(va), cur(a_width, va),
            cur(b_width, qb), prev_pair(kb), cur(kv_dup, kb), prev_pair(vb), cur(kv_dup, vb),
        ] + [_resident(t.shape) for t in (*bias_a, *bias_b)],
        out_specs=cur(a_width + b_width, 0),
        out_shape=jax.ShapeDtypeStruct((n, a_width + b_width), BF16),
        scratch_shapes=[pltpu.VMEM((hid.shape[0], diag.shape[0]) + corner.shape, F32)
                        for diag, corner, hid in (bias_a, bias_b)],
        compiler_params=pltpu.CompilerParams(
            dimension_semantics=("arbitrary",), vmem_limit_bytes=VMEM_LIMIT),
        name="band_attention",
    )(sink, proj, proj, proj, proj, proj, proj, proj, proj, proj, proj, *bias_a, *bias_b)


def _mlp_rows(n_rows, sub_rows, early_fn, h1_fn, o_ref, gpre_ref, wup_ref, wdown_ref,
              gpost_ref, side_jobs=()):
    n_sub = n_rows // sub_rows
    n_chunks = wup_ref.shape[1] // FF_CHUNK
    n_slots = n_sub * n_chunks * 2
    jobs = list(side_jobs)
    job_slot = {round((j + 0.5) * n_slots / len(jobs)): job for j, job in enumerate(jobs)}
    assert len(job_slot) == len(jobs) and all(0 < k <= n_slots for k in job_slot)
    rows = [slice(s * sub_rows, (s + 1) * sub_rows) for s in range(n_sub)]
    early = {0: early_fn(rows[0])} if early_fn else {}
    slot = 0
    for s in range(n_sub):
        if early_fn and s + 1 < n_sub:
            early[s + 1] = early_fn(rows[s + 1])
        h1 = h1_fn(rows[s], early.pop(s, None))
        hn = _rms(h1, gpre_ref[...]).astype(BF16)
        acc = None
        for c in range(n_chunks):
            ff = slice(c * FF_CHUNK, (c + 1) * FF_CHUNK)
            u = jnp.maximum(_dot(hn, wup_ref[:, ff]), 0.0)
            slot += 1
            if slot in job_slot:
                job_slot[slot]()
            part = _dot((u * u).astype(BF16), wdown_ref[ff, :])
            slot += 1
            if slot in job_slot:
                job_slot[slot]()
            acc = part if acc is None else acc + part
        o_ref[rows[s], :] = h1 + _rms(acc, gpost_ref[...])


def _outproj_mlp_kernel(a_ref, x_ref, wout_ref, gpost_ref,
                        gpre_ref, wup_ref, wdown_ref, gmpost_ref, o_ref):
    def out_proj(rows):
        return _dot(a_ref[rows, :], wout_ref[...])

    def h1_fn(rows, y):
        return x_ref[rows, :] + _rms(y, gpost_ref[...])

    _mlp_rows(x_ref.shape[0], OUTPROJ_SUB_ROWS, out_proj, h1_fn, o_ref,
              gpre_ref, wup_ref, wdown_ref, gmpost_ref)


def _outproj_mlp(attn, x2, w_out, g_post, g_mpre, w_up, w_down, g_mpost):
    n, d = x2.shape
    assert n % OUTPROJ_TILE == 0
    row = lambda w: pl.BlockSpec((OUTPROJ_TILE, w), lambda i: (i, 0))
    return pl.pallas_call(
        _outproj_mlp_kernel,
        grid=(n // OUTPROJ_TILE,),
        in_specs=[row(attn.shape[1]), row(d), _resident(w_out.shape), _resident((1, d)),
                  _resident((1, d)), _resident(w_up.shape), _resident(w_down.shape),
                  _resident((1, d))],
        out_specs=row(d),
        out_shape=jax.ShapeDtypeStruct((n, d), F32),
        compiler_params=pltpu.CompilerParams(
            dimension_semantics=("arbitrary",), vmem_limit_bytes=VMEM_LIMIT),
        name="outproj_mlp",
    )(attn, x2, w_out, g_post, g_mpre, w_up, w_down, g_mpost)


def _pool_mixer_jobs(h_ref, halo_fn, g_ref, pw_ref, pscale_ref, t_in_seq, y_ref):
    tm = h_ref.shape[0]
    pc = pw_ref.shape[1]
    st = {}

    def prepare():
        st["hn"] = _rms(h_ref[...], g_ref[...])
        st["halo"] = halo_fn()
        pos1 = t_in_seq * tm + 1 + lax.broadcasted_iota(jnp.int32, (tm, 1), 0)
        st["pos1"] = pos1
        st["inv_pos1"] = 1.0 / pos1.astype(F32)

    def group(gi, w):
        cols = slice(gi * pc, (gi + 1) * pc)
        hn = st["hn"][:, cols]
        wsum = jnp.concatenate([st["halo"][:, cols], hn], axis=0)
        k = 1
        while k < w:
            wsum = wsum + pltpu.roll(wsum, k, axis=0)
            k *= 2
        inv_cnt = jnp.where(st["pos1"] < w, st["inv_pos1"], 1.0 / w)
        d = (wsum[POOL_HALO:, :] * inv_cnt - hn).astype(BF16)
        y_ref[:, cols] = _dot(d, pw_ref[gi]) * pscale_ref[:, cols]

    return [prepare] + [functools.partial(group, gi, w) for gi, w in enumerate(POOL_WINDOWS)]


def _pool_mlp_kernel(h_ref, hnext_ref, gpre_ref, pw_ref, pscale_ref, gpost_ref,
                     gmpre_ref, wup_ref, wdown_ref, gmpost_ref, o_ref, y_ref, *, tiles_per_seq):
    i = pl.program_id(0)
    tm, dm = h_ref.shape
    @pl.when(i == 0)
    def _():
        for job in _pool_mixer_jobs(h_ref, lambda: jnp.zeros((POOL_HALO, dm), F32),
                                    gpre_ref, pw_ref, pscale_ref, 0, y_ref):
            job()

    y = y_ref[...]

    def h1_fn(rows, _):
        return h_ref[rows, :] + _rms(y[rows, :], gpost_ref[...])

    t_next = (i + 1) % tiles_per_seq

    def next_halo():
        keep = jnp.where(t_next == 0, 0.0, 1.0)
        return _rms(h_ref[tm - POOL_HALO:, :], gpre_ref[...]) * keep

    jobs = _pool_mixer_jobs(hnext_ref, next_halo, gpre_ref, pw_ref, pscale_ref, t_next, y_ref)
    _mlp_rows(tm, SUB_ROWS, None, h1_fn, o_ref, gmpre_ref, wup_ref, wdown_ref, gmpost_ref, jobs)


def _pool_mlp(h2, seq_len, g_pre, pool_w, pool_scale, g_post, g_mpre, w_up, w_down, g_mpost):
    n, d = h2.shape
    tiles_per_seq = seq_len // TOKEN_TILE
    n_tiles = n // TOKEN_TILE
    row = pl.BlockSpec((TOKEN_TILE, d), lambda i: (i, 0))
    nxt = pl.BlockSpec((TOKEN_TILE, d), lambda i: (jnp.minimum(i + 1, n_tiles - 1), 0))
    kern = functools.partial(_pool_mlp_kernel, tiles_per_seq=tiles_per_seq)
    return pl.pallas_call(
        kern,
        grid=(n_tiles,),
        in_specs=[row, nxt, _resident((1, d)), _resident(pool_w.shape), _resident((1, d)),
                  _resident((1, d)), _resident((1, d)), _resident(w_up.shape),
                  _resident(w_down.shape), _resident((1, d))],
        out_specs=row,
        out_shape=jax.ShapeDtypeStruct((n, d), F32),
        scratch_shapes=[pltpu.VMEM((TOKEN_TILE, d), F32)],
        compiler_params=pltpu.CompilerParams(
            dimension_semantics=("arbitrary",), vmem_limit_bytes=VMEM_LIMIT),
        name="pool_mlp",
    )(h2, h2, g_pre, pool_w, pool_scale, g_post, g_mpre, w_up, w_down, g_mpost)


def _pair_band_index(n_prev):
    band = (n_prev + 2) * CHUNK
    r = np.arange(PAIR)[:, None]
    k = np.arange(band)[None, :]
    e, qi = r // CHUNK, r % CHUNK
    kc = k // CHUNK - e
    ok = (kc >= 0) & (kc <= n_prev)
    key_pos = kc * CHUNK + k % CHUNK - n_prev * CHUNK
    return qi - key_pos, ok


def _t5_bucket(rel_kq):
    nb = T5_BUCKETS // 2
    ret = (rel_kq > 0).astype(np.int32) * nb
    n = np.abs(rel_kq)
    max_exact = nb // 2
    large = max_exact + (np.log(np.maximum(n, 1) / max_exact)
                         / math.log(T5_MAX_DIST / max_exact) * (nb - max_exact)).astype(np.int32)
    large = np.minimum(large, nb - 1)
    return ret + np.where(n < max_exact, n, large)


def _toeplitz_diagonals(table, index_of_rel, n_prev):
    band = (n_prev + 2) * CHUNK
    period = band + PAIR
    delta = np.arange(period)
    delta = np.where(delta < band, delta, delta - period)
    diag = jnp.take(table, index_of_rel(n_prev * CHUNK - delta), axis=0).T
    _, ok = _pair_band_index(n_prev)
    corner = jnp.asarray(np.where(ok, 0.0, NEG_INF * LOG2E), F32)
    return diag.astype(F32) * LOG2E, corner


def _bias_tables(relpos_a, t5_table):
    diag_a, corner_a = _toeplitz_diagonals(
        relpos_a, lambda rel: np.clip(rel, -A_MAX_REL, A_MAX_REL) + A_MAX_REL, A_PREV_CHUNKS)
    diag_b, corner_b = _toeplitz_diagonals(t5_table, lambda rel: _t5_bucket(-rel),
                                           B_PREV_CHUNKS)
    kc_a = np.arange(A_BAND) // CHUNK
    hide_a = np.zeros((PAIRS_PER_TILE + 1, 1, 1, A_BAND), bool)
    for jj in range(PAIRS_PER_TILE):
        hide_a[jj, 0, 0] = 2 * jj - A_PREV_CHUNKS + kc_a < 0
    kc_b = np.arange(B_BAND) // CHUNK
    hide_b = np.zeros((2, 1, 1, B_BAND), bool)
    hide_b[0, 0, 0] = kc_b - B_PREV_CHUNKS < 0
    hidden = lambda hide: jnp.asarray(np.where(hide[:, 0], NEG_INF * LOG2E, 0.0), F32)
    return (diag_a, corner_a, hidden(hide_a)), (diag_b, corner_b, hidden(hide_b))


def kernel(x, t5_table, e_norm_pre, e_norm_post, e_w_in, e_w_out, e_relpos_a, e_sink_b,
           o_norm_pre, o_norm_post, o_pool_w, o_pool_scale,
           mlp_norm_pre, mlp_norm_post, mlp_w_up, mlp_w_down):
    bsz, seq_len, d = x.shape
    depth = mlp_w_up.shape[0]
    a_heads = e_relpos_a.shape[2]
    b_heads = e_sink_b.shape[1]
    a_width, b_width = a_heads * HEAD_DIM, b_heads * HEAD_DIM
    b_kv_width = (b_heads // B_GROUP) * HEAD_DIM
    assert seq_len % ATTN_TILE == 0 and seq_len % TOKEN_TILE == 0
    assert a_heads == b_heads and a_heads % 2 == 0
    row = lambda v: v.reshape(1, -1).astype(F32)

    h = x.reshape(bsz * seq_len, d)
    pool_w2 = o_pool_w.reshape(o_pool_w.shape[0], -1, o_pool_w.shape[-1])
    names, casts = [], []
    for layer in range(depth):
        names += [("up", layer), ("down", layer)]
        casts += [(mlp_w_up, layer), (mlp_w_down, layer)]
        names.append(("mix", layer))
        casts.append((e_w_out, layer // 2) if layer % 2 == 0 else (pool_w2, layer // 2))
    bf = {}
    for layer in range(depth):
        i = layer // 2
        g_mpre, g_mpost = row(mlp_norm_pre[layer]), row(mlp_norm_post[layer])
        if layer % 2 == 0:
            seg = a_width
            q_segs = (0, 3)
            assert b_width == seg and (2 * b_kv_width) % (2 * LANES) == 0
            proj, cast = _inproj(h, row(e_norm_pre[i]), e_w_in[i].astype(BF16), q_segs, seg,
                                 3 * a_width + b_width, [] if bf else casts)
            bf = bf or dict(zip(names, cast))
            attn = _attention(proj, e_sink_b[i].astype(F32) * LOG2E,
                              *_bias_tables(e_relpos_a[i], t5_table),
                              seq_len, a_width, b_width, b_kv_width)
            h = _outproj_mlp(attn, h, bf["mix", layer], row(e_norm_post[i]),
                             g_mpre, bf["up", layer], bf["down", layer], g_mpost)
        else:
            h = _pool_mlp(h, seq_len, row(o_norm_pre[i]),
                          bf["mix", layer].reshape(o_pool_w.shape[1:]),
                          row(o_pool_scale[i]), row(o_norm_post[i]),
                          g_mpre, bf["up", layer], bf["down", layer], g_mpost)
    return h.reshape(bsz, seq_len, d)
```

```python
import functools
import math

import jax
import jax.numpy as jnp
import numpy as np
from jax import lax
from jax.experimental import pallas as pl
from jax.experimental.pallas import tpu as pltpu

CHUNK = 64
HEAD_DIM = 64
A_PREV_CHUNKS = 8
A_MAX_REL = 128
B_GROUP = 4
B_PREV_CHUNKS = 2
T5_BUCKETS = 32
T5_MAX_DIST = 128
POOL_WINDOWS = (2, 4, 8, 16)
RMS_EPS = 1e-6
NEG_INF = -1e30
LOG2E = math.log2(math.e)

LANES = 128
BF16_ROWS = 16
V7X_VMEM_BYTES = 64 * 1024 * 1024

TOKEN_TILE = 1024
INPROJ_TILE = 1024
OUTPROJ_TILE = 1024
SUB_ROWS = 512
OUTPROJ_SUB_ROWS = 512
ATTN_TILE = 512
PAIR = 2 * CHUNK
PAIRS_PER_TILE = ATTN_TILE // PAIR
A_BAND = (A_PREV_CHUNKS + 2) * CHUNK
B_BAND = (B_PREV_CHUNKS + 2) * CHUNK
ATTN_LOOKAHEAD = 2
POOL_HALO = 16
FF_CHUNK = 2048
VMEM_LIMIT = 56 * 1024 * 1024

BF16 = jnp.bfloat16
F32 = jnp.float32


def _rms(x, g):
    ms = jnp.mean(x * x, axis=-1, keepdims=True)
    return (x * lax.rsqrt(ms + RMS_EPS)) * g


def _dot(a, b):
    return jnp.dot(a, b, preferred_element_type=F32)


def _dot_nt(a, b):
    return lax.dot_general(a, b, (((1,), (1,)), ((), ())), preferred_element_type=F32)


def _resident(shape):
    nd = len(shape)
    return pl.BlockSpec(shape, lambda *_: (0,) * nd, pipeline_mode=pl.Buffered(1))


def _inproj_kernel(x_ref, g_ref, w_ref, *refs, n_cast, seg, q_segs, q_scale, kv_b0):
    o_ref = refs[n_cast]
    for src_ref, dst_ref in zip(refs[:n_cast], refs[n_cast + 1:]):
        dst_ref[...] = src_ref[...].astype(BF16)
    lo_half = lax.broadcasted_iota(jnp.int32, (1, LANES), 1) < HEAD_DIM
    for r in range(x_ref.shape[0] // SUB_ROWS):
        rows = slice(r * SUB_ROWS, (r + 1) * SUB_ROWS)
        xb = _rms(x_ref[rows, :], g_ref[...]).astype(BF16)
        for s in range(kv_b0 // seg):
            y = _dot(xb, w_ref[:, s * seg:(s + 1) * seg])
            if s in q_segs:
                y = y * q_scale
            o_ref[rows, s * seg:(s + 1) * seg] = y.astype(BF16)
        y = _dot(xb, w_ref[:, kv_b0:])
        for j in range(y.shape[1] // LANES):
            two = y[:, j * LANES:(j + 1) * LANES]
            swapped = pltpu.roll(two, HEAD_DIM, axis=1)
            c0 = kv_b0 + 2 * j * LANES
            o_ref[rows, c0:c0 + LANES] = jnp.where(lo_half, two, swapped).astype(BF16)
            o_ref[rows, c0 + LANES:c0 + 2 * LANES] = jnp.where(lo_half, swapped, two).astype(BF16)


def _inproj(x2, g, w_in, q_segs, seg, kv_b0, casts):
    n, d = x2.shape
    out_cols = kv_b0 + 2 * (w_in.shape[1] - kv_b0)
    steps = n // INPROJ_TILE
    assert n % INPROJ_TILE == 0 and kv_b0 % seg == 0
    cast_in, cast_out, cast_shape = [], [], []
    for w, li in casts:
        _, r, c = w.shape
        assert r % (steps * BF16_ROWS) == 0
        cast_in.append(pl.BlockSpec((None, r // steps, c), lambda i, li=li: (li, i, 0)))
        cast_out.append(pl.BlockSpec((r // steps, c), lambda i: (i, 0)))
        cast_shape.append(jax.ShapeDtypeStruct((r, c), BF16))
    kern = functools.partial(_inproj_kernel, n_cast=len(casts), seg=seg, q_segs=q_segs,
                             q_scale=HEAD_DIM ** -0.5 * LOG2E, kv_b0=kv_b0)
    proj, *cast = pl.pallas_call(
        kern,
        grid=(steps,),
        in_specs=[
            pl.BlockSpec((INPROJ_TILE, d), lambda i: (i, 0)),
            _resident((1, d)),
            _resident(w_in.shape),
        ] + cast_in,
        out_specs=[pl.BlockSpec((INPROJ_TILE, out_cols), lambda i: (i, 0))] + cast_out,
        out_shape=[jax.ShapeDtypeStruct((n, out_cols), BF16)] + cast_shape,
        compiler_params=pltpu.CompilerParams(
            dimension_semantics=("arbitrary",), vmem_limit_bytes=VMEM_LIMIT),
        name="inproj",
    )(x2, g, w_in, *[w for w, _ in casts])
    return proj, cast


def _softmax_pv(s, v_ones, sink):
    m = jnp.max(s, axis=-1, keepdims=True)
    if sink is not None:
        m = jnp.maximum(m, sink)
    pb = jnp.exp2((s - m).astype(BF16))
    if isinstance(v_ones, tuple):
        w1 = v_ones[0].shape[0]
        o = _dot(pb[:, :w1], v_ones[0]) + _dot(pb[:, w1:], v_ones[1])
    else:
        o = _dot(pb, v_ones)
    den = o[:, LANES:]
    if sink is not None:
        den = den + jnp.exp2(sink - m)
    out = o[:, :LANES] * (1.0 / den)
    lo_half = lax.broadcasted_iota(jnp.int32, (1, LANES), 1) < HEAD_DIM
    return [jnp.where(lo_half, out[r:r + PAIR], out[r + PAIR:r + 2 * PAIR])
            for r in range(0, out.shape[0], 2 * PAIR)]


def _attn_kernel(sink_ref, qa_ref, kap_ref, kac_ref, vap_ref, vac_ref,
                 qb_ref, kbp_ref, kbc_ref, vbp_ref, vbc_ref,
                 diaga_ref, cornera_ref, hida_ref, diagb_ref, cornerb_ref, hidb_ref,
                 o_ref, biasa_ref, biasb_ref, *, tiles_per_seq, pairs_per_kv):
    i = pl.program_id(0)

    @pl.when(i == 0)
    def _():
        for diag_ref, corner_ref, hid_ref, bias_ref in (
                (diaga_ref, cornera_ref, hida_ref, biasa_ref),
                (diagb_ref, cornerb_ref, hidb_ref, biasb_ref)):
            rows, band = corner_ref.shape
            for hd in range(diag_ref.shape[0]):
                diag = jnp.broadcast_to(diag_ref[hd:hd + 1, :], (rows, diag_ref.shape[1]))
                skew = pltpu.roll(diag, 0, axis=1, stride=1, stride_axis=0)
                base = skew[:, :band] + corner_ref[...]
                for v in range(hid_ref.shape[0]):
                    bias_ref[v, hd] = base + hid_ref[v]

    oa_ref = o_ref.at[:, :qa_ref.shape[1]]
    ob_ref = o_ref.at[:, qa_ref.shape[1]:]
    first = (i % tiles_per_seq) == 0
    lo_half = lax.broadcasted_iota(jnp.int32, (1, LANES), 1) < HEAD_DIM
    zero = jnp.zeros((), BF16)
    n_head_pairs = qa_ref.shape[1] // LANES

    def lanes(hp):
        return slice(hp * LANES, (hp + 1) * LANES)

    def stacked_q(q_ref, r0, hp):
        q2 = q_ref[r0:r0 + PAIR, lanes(hp)]
        return jnp.concatenate([jnp.where(lo_half, q2, zero), jnp.where(lo_half, zero, q2)],
                               axis=0)

    def with_ones(v):
        return jnp.concatenate([v, jnp.ones(v.shape, BF16)], axis=1)

    def logits_a(jj, hp):
        r0 = jj * PAIR
        q = stacked_q(qa_ref, r0, hp)
        s = jnp.concatenate([_dot_nt(q, kap_ref[r0:, lanes(hp)]),
                             _dot_nt(q, kac_ref[:r0 + PAIR, lanes(hp)])], axis=1)
        bias = biasa_ref[jnp.where(first, jj, PAIRS_PER_TILE), 2 * hp:2 * hp + 2]
        return s + bias.reshape(s.shape)

    def finish_a(jj, hp, s):
        r0 = jj * PAIR
        return _softmax_pv(s, (with_ones(vap_ref[r0:, lanes(hp)]),
                               with_ones(vac_ref[:r0 + PAIR, lanes(hp)])), None)

    def band_b(prev_ref, cur_ref, jj, n):
        r0 = jj * PAIR
        if jj == 0:
            return jnp.concatenate([prev_ref[:, lanes(n)], cur_ref[:PAIR, lanes(n)]], axis=0)
        return cur_ref[r0 - PAIR:r0 + PAIR, lanes(n)]

    def blocks_b(n):
        return range(n * pairs_per_kv, (n + 1) * pairs_per_kv)

    def logits_b(jj, n):
        q = jnp.concatenate([stacked_q(qb_ref, jj * PAIR, hp) for hp in blocks_b(n)], axis=0)
        s = _dot_nt(q, band_b(kbp_ref, kbc_ref, jj, n))
        variant = jnp.where(first, 0, 1) if jj == 0 else 1
        heads = slice(2 * blocks_b(n)[0], 2 * blocks_b(n)[-1] + 2)
        return s + biasb_ref[variant, heads].reshape(s.shape)

    def finish_b(jj, n, s):
        sink = jnp.concatenate([jnp.full((PAIR, 1), sink_ref[2 * hp + e], F32)
                                for hp in blocks_b(n) for e in range(2)], axis=0)
        return _softmax_pv(s, with_ones(band_b(vbp_ref, vbc_ref, jj, n)), sink)

    units = []
    for n in range(n_head_pairs // pairs_per_kv):
        for hp in blocks_b(n):
            for jj in range(PAIRS_PER_TILE):
                units.append((logits_a, finish_a, oa_ref, [hp], jj, hp))
                if jj % pairs_per_kv == pairs_per_kv - 1:
                    jb = (hp - blocks_b(n)[0]) * (PAIRS_PER_TILE // pairs_per_kv) + jj // pairs_per_kv
                    units.append((logits_b, finish_b, ob_ref, list(blocks_b(n)), jb, n))
    pending = {}
    for t in range(len(units) + ATTN_LOOKAHEAD):
        if t < len(units):
            logits, _, _, _, jj, arg = units[t]
            pending[t] = logits(jj, arg)
        if t >= ATTN_LOOKAHEAD:
            _, finish, out_ref, blocks, jj, arg = units[t - ATTN_LOOKAHEAD]
            outs = finish(jj, arg, pending.pop(t - ATTN_LOOKAHEAD))
            for hp, out in zip(blocks, outs):
                out_ref[jj * PAIR:(jj + 1) * PAIR, lanes(hp)] = out.astype(BF16)


def _attention(proj, sink, bias_a, bias_b, seq_len, a_width, b_width, b_kv_width):
    n = proj.shape[0]
    tiles_per_seq = seq_len // ATTN_TILE
    kv_dup = 2 * b_kv_width
    assert a_width == b_width and kv_dup % LANES == 0 and a_width % kv_dup == 0
    sub = ATTN_TILE // PAIR
    qa, ka, va, qb = 0, 1, 2, 3
    kb = 4 * a_width // kv_dup
    vb = kb + 1

    def cur(width, c):
        return pl.BlockSpec((ATTN_TILE, width), lambda i: (i, c))

    def prev(c):
        return pl.BlockSpec((ATTN_TILE, a_width), lambda i: (jnp.maximum(i - 1, 0), c))

    def prev_pair(c):
        return pl.BlockSpec((PAIR, kv_dup), lambda i: (jnp.maximum(i * sub - 1, 0), c))

    kern = functools.partial(_attn_kernel, tiles_per_seq=tiles_per_seq,
                             pairs_per_kv=(b_width // LANES) // (kv_dup // LANES))
    return pl.pallas_call(
        kern,
        grid=(n // ATTN_TILE,),
        in_specs=[
            pl.BlockSpec(memory_space=pltpu.SMEM),
            cur(a_width, qa), prev(ka), cur(a_width, ka), prev(va), cur(a_width, va),
            cur(b_width, qb), prev_pair(kb), cur(kv_dup, kb), prev_pair(vb), cur(kv_dup, vb),
        ] + [_resident(t.shape) for t in (*bias_a, *bias_b)],
        out_specs=cur(a_width + b_width, 0),
        out_shape=jax.ShapeDtypeStruct((n, a_width + b_width), BF16),
        scratch_shapes=[pltpu.VMEM((hid.shape[0], diag.shape[0]) + corner.shape, F32)
                        for diag, corner, hid in (bias_a, bias_b)],
        compiler_params=pltpu.CompilerParams(
            dimension_semantics=("arbitrary",), vmem_limit_bytes=VMEM_LIMIT),
        name="band_attention",
    )(sink, proj, proj, proj, proj, proj, proj, proj, proj, proj, proj, *bias_a, *bias_b)


def _mlp_rows(n_rows, sub_rows, early_fn, h1_fn, o_ref, gpre_ref, wup_ref, wdown_ref,
              gpost_ref, side_jobs=()):
    n_sub = n_rows // sub_rows
    n_chunks = wup_ref.shape[1] // FF_CHUNK
    n_slots = n_sub * n_chunks * 2
    jobs = list(side_jobs)
    job_slot = {round((j + 0.5) * n_slots / len(jobs)): job for j, job in enumerate(jobs)}
    assert len(job_slot) == len(jobs) and all(0 < k <= n_slots for k in job_slot)
    rows = [slice(s * sub_rows, (s + 1) * sub_rows) for s in range(n_sub)]
    early = {0: early_fn(rows[0])} if early_fn else {}
    slot = 0
    for s in range(n_sub):
        if early_fn and s + 1 < n_sub:
            early[s + 1] = early_fn(rows[s + 1])
        h1 = h1_fn(rows[s], early.pop(s, None))
        hn = _rms(h1, gpre_ref[...]).astype(BF16)
        acc = None
        for c in range(n_chunks):
            ff = slice(c * FF_CHUNK, (c + 1) * FF_CHUNK)
            u = jnp.maximum(_dot(hn, wup_ref[:, ff]), 0.0)
            slot += 1
            if slot in job_slot:
                job_slot[slot]()
            part = _dot((u * u).astype(BF16), wdown_ref[ff, :])
            slot += 1
            if slot in job_slot:
                job_slot[slot]()
            acc = part if acc is None else acc + part
        o_ref[rows[s], :] = h1 + _rms(acc, gpost_ref[...])


def _outproj_mlp_kernel(a_ref, x_ref, wout_ref, gpost_ref,
                        gpre_ref, wup_ref, wdown_ref, gmpost_ref, o_ref):
    def out_proj(rows):
        return _dot(a_ref[rows, :], wout_ref[...])

    def h1_fn(rows, y):
        return x_ref[rows, :] + _rms(y, gpost_ref[...])

    _mlp_rows(x_ref.shape[0], OUTPROJ_SUB_ROWS, out_proj, h1_fn, o_ref,
              gpre_ref, wup_ref, wdown_ref, gmpost_ref)


def _outproj_mlp(attn, x2, w_out, g_post, g_mpre, w_up, w_down, g_mpost):
    n, d = x2.shape
    assert n % OUTPROJ_TILE == 0
    row = lambda w: pl.BlockSpec((OUTPROJ_TILE, w), lambda i: (i, 0))
    return pl.pallas_call(
        _outproj_mlp_kernel,
        grid=(n // OUTPROJ_TILE,),
        in_specs=[row(attn.shape[1]), row(d), _resident(w_out.shape), _resident((1, d)),
                  _resident((1, d)), _resident(w_up.shape), _resident(w_down.shape),
                  _resident((1, d))],
        out_specs=row(d),
        out_shape=jax.ShapeDtypeStruct((n, d), F32),
        compiler_params=pltpu.CompilerParams(
            dimension_semantics=("arbitrary",), vmem_limit_bytes=VMEM_LIMIT),
        name="outproj_mlp",
    )(attn, x2, w_out, g_post, g_mpre, w_up, w_down, g_mpost)


def _pool_mixer_jobs(h_ref, halo_fn, g_ref, pw_ref, pscale_ref, t_in_seq, y_ref):
    tm = h_ref.shape[0]
    pc = pw_ref.shape[1]
    st = {}

    def prepare():
        st["hn"] = _rms(h_ref[...], g_ref[...])
        st["halo"] = halo_fn()
        pos1 = t_in_seq * tm + 1 + lax.broadcasted_iota(jnp.int32, (tm, 1), 0)
        st["pos1"] = pos1
        st["inv_pos1"] = 1.0 / pos1.astype(F32)

    def group(gi, w):
        cols = slice(gi * pc, (gi + 1) * pc)
        hn = st["hn"][:, cols]
        wsum = jnp.concatenate([st["halo"][:, cols], hn], axis=0)
        k = 1
        while k < w:
            wsum = wsum + pltpu.roll(wsum, k, axis=0)
            k *= 2
        inv_cnt = jnp.where(st["pos1"] < w, st["inv_pos1"], 1.0 / w)
        d = (wsum[POOL_HALO:, :] * inv_cnt - hn).astype(BF16)
        y_ref[:, cols] = _dot(d, pw_ref[gi]) * pscale_ref[:, cols]

    return [prepare] + [functools.partial(group, gi, w) for gi, w in enumerate(POOL_WINDOWS)]


def _pool_mlp_kernel(h_ref, hnext_ref, gpre_ref, pw_ref, pscale_ref, gpost_ref,
                     gmpre_ref, wup_ref, wdown_ref, gmpost_ref, o_ref, y_ref, *, tiles_per_seq):
    i = pl.program_id(0)
    tm, dm = h_ref.shape
    @pl.when(i == 0)
    def _():
        for job in _pool_mixer_jobs(h_ref, lambda: jnp.zeros((POOL_HALO, dm), F32),
                                    gpre_ref, pw_ref, pscale_ref, 0, y_ref):
            job()

    y = y_ref[...]

    def h1_fn(rows, _):
        return h_ref[rows, :] + _rms(y[rows, :], gpost_ref[...])

    t_next = (i + 1) % tiles_per_seq

    def next_halo():
        keep = jnp.where(t_next == 0, 0.0, 1.0)
        return _rms(h_ref[tm - POOL_HALO:, :], gpre_ref[...]) * keep

    jobs = _pool_mixer_jobs(hnext_ref, next_halo, gpre_ref, pw_ref, pscale_ref, t_next, y_ref)
    _mlp_rows(tm, SUB_ROWS, None, h1_fn, o_ref, gmpre_ref, wup_ref, wdown_ref, gmpost_ref, jobs)


def _pool_mlp(h2, seq_len, g_pre, pool_w, pool_scale, g_post, g_mpre, w_up, w_down, g_mpost):
    n, d = h2.shape
    tiles_per_seq = seq_len // TOKEN_TILE
    n_tiles = n // TOKEN_TILE
    row = pl.BlockSpec((TOKEN_TILE, d), lambda i: (i, 0))
    nxt = pl.BlockSpec((TOKEN_TILE, d), lambda i: (jnp.minimum(i + 1, n_tiles - 1), 0))
    kern = functools.partial(_pool_mlp_kernel, tiles_per_seq=tiles_per_seq)
    return pl.pallas_call(
        kern,
        grid=(n_tiles,),
        in_specs=[row, nxt, _resident((1, d)), _resident(pool_w.shape), _resident((1, d)),
                  _resident((1, d)), _resident((1, d)), _resident(w_up.shape),
                  _resident(w_down.shape), _resident((1, d))],
        out_specs=row,
        out_shape=jax.ShapeDtypeStruct((n, d), F32),
        scratch_shapes=[pltpu.VMEM((TOKEN_TILE, d), F32)],
        compiler_params=pltpu.CompilerParams(
            dimension_semantics=("arbitrary",), vmem_limit_bytes=VMEM_LIMIT),
        name="pool_mlp",
    )(h2, h2, g_pre, pool_w, pool_scale, g_post, g_mpre, w_up, w_down, g_mpost)


def _pair_band_index(n_prev):
    band = (n_prev + 2) * CHUNK
    r = np.arange(PAIR)[:, None]
    k = np.arange(band)[None, :]
    e, qi = r // CHUNK, r % CHUNK
    kc = k // CHUNK - e
    ok = (kc >= 0) & (kc <= n_prev)
    key_pos = kc * CHUNK + k % CHUNK - n_prev * CHUNK
    return qi - key_pos, ok


def _t5_bucket(rel_kq):
    nb = T5_BUCKETS // 2
    ret = (rel_kq > 0).astype(np.int32) * nb
    n = np.abs(rel_kq)
    max_exact = nb // 2
    large = max_exact + (np.log(np.maximum(n, 1) / max_exact)
                         / math.log(T5_MAX_DIST / max_exact) * (nb - max_exact)).astype(np.int32)
    large = np.minimum(large, nb - 1)
    return ret + np.where(n < max_exact, n, large)


def _toeplitz_diagonals(table, index_of_rel, n_prev):
    band = (n_prev + 2) * CHUNK
    period = band + PAIR
    delta = np.arange(period)
    delta = np.where(delta < band, delta, delta - period)
    diag = jnp.take(table, index_of_rel(n_prev * CHUNK - delta), axis=0).T
    _, ok = _pair_band_index(n_prev)
    corner = jnp.asarray(np.where(ok, 0.0, NEG_INF * LOG2E), F32)
    return diag.astype(F32) * LOG2E, corner


def _bias_tables(relpos_a, t5_table):
    diag_a, corner_a = _toeplitz_diagonals(
        relpos_a, lambda rel: np.clip(rel, -A_MAX_REL, A_MAX_REL) + A_MAX_REL, A_PREV_CHUNKS)
    diag_b, corner_b = _toeplitz_diagonals(t5_table, lambda rel: _t5_bucket(-rel),
                                           B_PREV_CHUNKS)
    kc_a = np.arange(A_BAND) // CHUNK
    hide_a = np.zeros((PAIRS_PER_TILE + 1, 1, 1, A_BAND), bool)
    for jj in range(PAIRS_PER_TILE):
        hide_a[jj, 0, 0] = 2 * jj - A_PREV_CHUNKS + kc_a < 0
    kc_b = np.arange(B_BAND) // CHUNK
    hide_b = np.zeros((2, 1, 1, B_BAND), bool)
    hide_b[0, 0, 0] = kc_b - B_PREV_CHUNKS < 0
    hidden = lambda hide: jnp.asarray(np.where(hide[:, 0], NEG_INF * LOG2E, 0.0), F32)
    return (diag_a, corner_a, hidden(hide_a)), (diag_b, corner_b, hidden(hide_b))


def kernel(x, t5_table, e_norm_pre, e_norm_post, e_w_in, e_w_out, e_relpos_a, e_sink_b,
           o_norm_pre, o_norm_post, o_pool_w, o_pool_scale,
           mlp_norm_pre, mlp_norm_post, mlp_w_up, mlp_w_down):
    bsz, seq_len, d = x.shape
    depth = mlp_w_up.shape[0]
    a_heads = e_relpos_a.shape[2]
    b_heads = e_sink_b.shape[1]
    a_width, b_width = a_heads * HEAD_DIM, b_heads * HEAD_DIM
    b_kv_width = (b_heads // B_GROUP) * HEAD_DIM
    assert seq_len % ATTN_TILE == 0 and seq_len % TOKEN_TILE == 0
    assert a_heads == b_heads and a_heads % 2 == 0
    row = lambda v: v.reshape(1, -1).astype(F32)

    h = x.reshape(bsz * seq_len, d)
    pool_w2 = o_pool_w.reshape(o_pool_w.shape[0], -1, o_pool_w.shape[-1])
    names, casts = [], []
    for layer in range(depth):
        names += [("up", layer), ("down", layer)]
        casts += [(mlp_w_up, layer), (mlp_w_down, layer)]
        names.append(("mix", layer))
        casts.append((e_w_out, layer // 2) if layer % 2 == 0 else (pool_w2, layer // 2))
    bf = {}
    for layer in range(depth):
        i = layer // 2
        g_mpre, g_mpost = row(mlp_norm_pre[layer]), row(mlp_norm_post[layer])
        if layer % 2 == 0:
            seg = a_width
            q_segs = (0, 3)
            assert b_width == seg and (2 * b_kv_width) % (2 * LANES) == 0
            proj, cast = _inproj(h, row(e_norm_pre[i]), e_w_in[i].astype(BF16), q_segs, seg,
                                 3 * a_width + b_width, [] if bf else casts)
            bf = bf or dict(zip(names, cast))
            attn = _attention(proj, e_sink_b[i].astype(F32) * LOG2E,
                              *_bias_tables(e_relpos_a[i], t5_table),
                              seq_len, a_width, b_width, b_kv_width)
            h = _outproj_mlp(attn, h, bf["mix", layer], row(e_norm_post[i]),
                             g_mpre, bf["up", layer], bf["down", layer], g_mpost)
        else:
            h = _pool_mlp(h, seq_len, row(o_norm_pre[i]),
                          bf["mix", layer].reshape(o_pool_w.shape[1:]),
                          row(o_pool_scale[i]), row(o_norm_post[i]),
                          g_mpre, bf["up", layer], bf["down", layer], g_mpost)
    return h.reshape(bsz, seq_len, d)
```

```python
import functools
import math

import jax
import jax.numpy as jnp
import numpy as np
from jax import lax
from jax.experimental import pallas as pl
from jax.experimental.pallas import tpu as pltpu

CHUNK = 64
HEAD_DIM = 64
A_PREV_CHUNKS = 8
A_MAX_REL = 128
B_GROUP = 4
B_PREV_CHUNKS = 2
T5_BUCKETS = 32
T5_MAX_DIST = 128
POOL_WINDOWS = (2, 4, 8, 16)
RMS_EPS = 1e-6
NEG_INF = -1e30
LOG2E = math.log2(math.e)

LANES = 128
BF16_ROWS = 16
V7X_VMEM_BYTES = 64 * 1024 * 1024

TOKEN_TILE = 1024
INPROJ_TILE = 2048
OUTPROJ_TILE = 1024
SUB_ROWS = 512
OUTPROJ_SUB_ROWS = 512
ATTN_TILE = 512
PAIR = 2 * CHUNK
PAIRS_PER_TILE = ATTN_TILE // PAIR
A_BAND = (A_PREV_CHUNKS + 2) * CHUNK
B_BAND = (B_PREV_CHUNKS + 2) * CHUNK
ATTN_LOOKAHEAD = 3
POOL_HALO = 16
FF_CHUNK = 2048
VMEM_LIMIT = 56 * 1024 * 1024

BF16 = jnp.bfloat16
F32 = jnp.float32


def _rms(x, g):
    ms = jnp.mean(x * x, axis=-1, keepdims=True)
    return (x * lax.rsqrt(ms + RMS_EPS)) * g


def _dot(a, b):
    return jnp.dot(a, b, preferred_element_type=F32)


def _dot_nt(a, b):
    return lax.dot_general(a, b, (((1,), (1,)), ((), ())), preferred_element_type=F32)


def _resident(shape):
    nd = len(shape)
    return pl.BlockSpec(shape, lambda *_: (0,) * nd, pipeline_mode=pl.Buffered(1))


def _inproj_kernel(x_ref, g_ref, w_ref, *refs, n_cast, seg, q_segs, q_scale, kv_b0):
    o_ref = refs[n_cast]
    for src_ref, dst_ref in zip(refs[:n_cast], refs[n_cast + 1:]):
        dst_ref[...] = src_ref[...].astype(BF16)
    lo_half = lax.broadcasted_iota(jnp.int32, (1, LANES), 1) < HEAD_DIM
    for r in range(x_ref.shape[0] // SUB_ROWS):
        rows = slice(r * SUB_ROWS, (r + 1) * SUB_ROWS)
        xb = _rms(x_ref[rows, :], g_ref[...]).astype(BF16)
        for s in range(kv_b0 // seg):
            y = _dot(xb, w_ref[:, s * seg:(s + 1) * seg])
            if s in q_segs:
                y = y * q_scale
            o_ref[rows, s * seg:(s + 1) * seg] = y.astype(BF16)
        y = _dot(xb, w_ref[:, kv_b0:])
        for j in range(y.shape[1] // LANES):
            two = y[:, j * LANES:(j + 1) * LANES]
            swapped = pltpu.roll(two, HEAD_DIM, axis=1)
            c0 = kv_b0 + 2 * j * LANES
            o_ref[rows, c0:c0 + LANES] = jnp.where(lo_half, two, swapped).astype(BF16)
            o_ref[rows, c0 + LANES:c0 + 2 * LANES] = jnp.where(lo_half, swapped, two).astype(BF16)


def _inproj(x2, g, w_in, q_segs, seg, kv_b0, casts):
    n, d = x2.shape
    out_cols = kv_b0 + 2 * (w_in.shape[1] - kv_b0)
    steps = n // INPROJ_TILE
    assert n % INPROJ_TILE == 0 and kv_b0 % seg == 0
    cast_in, cast_out, cast_shape = [], [], []
    for w, li in casts:
        _, r, c = w.shape
        assert r % (steps * BF16_ROWS) == 0
        cast_in.append(pl.BlockSpec((None, r // steps, c), lambda i, li=li: (li, i, 0)))
        cast_out.append(pl.BlockSpec((r // steps, c), lambda i: (i, 0)))
        cast_shape.append(jax.ShapeDtypeStruct((r, c), BF16))
    kern = functools.partial(_inproj_kernel, n_cast=len(casts), seg=seg, q_segs=q_segs,
                             q_scale=HEAD_DIM ** -0.5 * LOG2E, kv_b0=kv_b0)
    proj, *cast = pl.pallas_call(
        kern,
        grid=(steps,),
        in_specs=[
            pl.BlockSpec((INPROJ_TILE, d), lambda i: (i, 0)),
            _resident((1, d)),
            _resident(w_in.shape),
        ] + cast_in,
        out_specs=[pl.BlockSpec((INPROJ_TILE, out_cols), lambda i: (i, 0))] + cast_out,
        out_shape=[jax.ShapeDtypeStruct((n, out_cols), BF16)] + cast_shape,
        compiler_params=pltpu.CompilerParams(
            dimension_semantics=("arbitrary",), vmem_limit_bytes=VMEM_LIMIT),
        name="inproj",
    )(x2, g, w_in, *[w for w, _ in casts])
    return proj, cast


def _softmax_pv(s, v_ones, sink):
    m = jnp.max(s, axis=-1, keepdims=True)
    if sink is not None:
        m = jnp.maximum(m, sink)
    pb = jnp.exp2((s - m).astype(BF16))
    if isinstance(v_ones, tuple):
        w1 = v_ones[0].shape[0]
        o = _dot(pb[:, :w1], v_ones[0]) + _dot(pb[:, w1:], v_ones[1])
    else:
        o = _dot(pb, v_ones)
    den = o[:, LANES:]
    if sink is not None:
        den = den + jnp.exp2(sink - m)
    out = o[:, :LANES] * (1.0 / den)
    half = out.shape[0] // 2
    lo_half = lax.broadcasted_iota(jnp.int32, (1, LANES), 1) < HEAD_DIM
    return jnp.where(lo_half, out[:half], out[half:])


def _attn_kernel(sink_ref, qa_ref, kap_ref, kac_ref, vap_ref, vac_ref,
                 qb_ref, kbp_ref, kbc_ref, vbp_ref, vbc_ref,
                 diaga_ref, cornera_ref, hida_ref, diagb_ref, cornerb_ref, hidb_ref,
                 o_ref, biasa_ref, biasb_ref, *, tiles_per_seq, pairs_per_kv):
    i = pl.program_id(0)

    @pl.when(i == 0)
    def _():
        for diag_ref, corner_ref, hid_ref, bias_ref in (
                (diaga_ref, cornera_ref, hida_ref, biasa_ref),
                (diagb_ref, cornerb_ref, hidb_ref, biasb_ref)):
            rows, band = corner_ref.shape
            for hd in range(diag_ref.shape[0]):
                diag = jnp.broadcast_to(diag_ref[hd:hd + 1, :], (rows, diag_ref.shape[1]))
                skew = pltpu.roll(diag, 0, axis=1, stride=1, stride_axis=0)
                base = skew[:, :band] + corner_ref[...]
                for v in range(hid_ref.shape[0]):
                    bias_ref[v, hd] = base + hid_ref[v]

    oa_ref = o_ref.at[:, :qa_ref.shape[1]]
    ob_ref = o_ref.at[:, qa_ref.shape[1]:]
    first = (i % tiles_per_seq) == 0
    lo_half = lax.broadcasted_iota(jnp.int32, (1, LANES), 1) < HEAD_DIM
    zero = jnp.zeros((), BF16)
    n_head_pairs = qa_ref.shape[1] // LANES

    def lanes(hp):
        return slice(hp * LANES, (hp + 1) * LANES)

    def stacked_q(q_ref, r0, hp):
        q2 = q_ref[r0:r0 + PAIR, lanes(hp)]
        return jnp.concatenate([jnp.where(lo_half, q2, zero), jnp.where(lo_half, zero, q2)],
                               axis=0)

    def with_ones(v):
        return jnp.concatenate([v, jnp.ones(v.shape, BF16)], axis=1)

    def logits_a(jj, hp):
        r0 = jj * PAIR
        q = stacked_q(qa_ref, r0, hp)
        s = jnp.concatenate([_dot_nt(q, kap_ref[r0:, lanes(hp)]),
                             _dot_nt(q, kac_ref[:r0 + PAIR, lanes(hp)])], axis=1)
        bias = biasa_ref[jnp.where(first, jj, PAIRS_PER_TILE), 2 * hp:2 * hp + 2]
        return s + bias.reshape(s.shape)

    def finish_a(jj, hp, s):
        r0 = jj * PAIR
        return _softmax_pv(s, (with_ones(vap_ref[r0:, lanes(hp)]),
                               with_ones(vac_ref[:r0 + PAIR, lanes(hp)])), None)

    def band_b(prev_ref, cur_ref, jj, hp):
        r0 = jj * PAIR
        kv = lanes(hp // pairs_per_kv)
        if jj == 0:
            return jnp.concatenate([prev_ref[:, kv], cur_ref[:PAIR, kv]], axis=0)
        return cur_ref[r0 - PAIR:r0 + PAIR, kv]

    def logits_b(jj, hp):
        s = _dot_nt(stacked_q(qb_ref, jj * PAIR, hp), band_b(kbp_ref, kbc_ref, jj, hp))
        variant = jnp.where(first, 0, 1) if jj == 0 else 1
        return s + biasb_ref[variant, 2 * hp:2 * hp + 2].reshape(s.shape)

    def finish_b(jj, hp, s):
        sink = jnp.concatenate(
            [jnp.full((PAIR, 1), sink_ref[2 * hp + e], F32) for e in range(2)], axis=0)
        return _softmax_pv(s, with_ones(band_b(vbp_ref, vbc_ref, jj, hp)), sink)

    mixers = {"a": (logits_a, finish_a, oa_ref), "b": (logits_b, finish_b, ob_ref)}
    units = [(mx, jj, hp) for hp in range(n_head_pairs) for jj in range(PAIRS_PER_TILE)
             for mx in ("a", "b")]
    pending = {}
    for t in range(len(units) + ATTN_LOOKAHEAD):
        if t < len(units):
            mx, jj, hp = units[t]
            pending[t] = mixers[mx][0](jj, hp)
        if t >= ATTN_LOOKAHEAD:
            mx, jj, hp = units[t - ATTN_LOOKAHEAD]
            out = mixers[mx][1](jj, hp, pending.pop(t - ATTN_LOOKAHEAD))
            mixers[mx][2][jj * PAIR:(jj + 1) * PAIR, lanes(hp)] = out.astype(BF16)


def _attention(proj, sink, bias_a, bias_b, seq_len, a_width, b_width, b_kv_width):
    n = proj.shape[0]
    tiles_per_seq = seq_len // ATTN_TILE
    kv_dup = 2 * b_kv_width
    assert a_width == b_width and kv_dup % LANES == 0 and a_width % kv_dup == 0
    sub = ATTN_TILE // PAIR
    qa, ka, va, qb = 0, 1, 2, 3
    kb = 4 * a_width // kv_dup
    vb = kb + 1

    def cur(width, c):
        return pl.BlockSpec((ATTN_TILE, width), lambda i: (i, c))

    def prev(c):
        return pl.BlockSpec((ATTN_TILE, a_width), lambda i: (jnp.maximum(i - 1, 0), c))

    def prev_pair(c):
        return pl.BlockSpec((PAIR, kv_dup), lambda i: (jnp.maximum(i * sub - 1, 0), c))

    kern = functools.partial(_attn_kernel, tiles_per_seq=tiles_per_seq,
                             pairs_per_kv=(b_width // LANES) // (kv_dup // LANES))
    return pl.pallas_call(
        kern,
        grid=(n // ATTN_TILE,),
        in_specs=[
            pl.BlockSpec(memory_space=pltpu.SMEM),
            cur(a_width, qa), prev(ka), cur(a_width, ka), prev(va), cur(a_width, va),
            cur(b_width, qb), prev_pair(kb), cur(kv_dup, kb), prev_pair(vb), cur(kv_dup, vb),
        ] + [_resident(t.shape) for t in (*bias_a, *bias_b)],
        out_specs=cur(a_width + b_width, 0),
        out_shape=jax.ShapeDtypeStruct((n, a_width + b_width), BF16),
        scratch_shapes=[pltpu.VMEM((hid.shape[0], diag.shape[0]) + corner.shape, F32)
                        for diag, corner, hid in (bias_a, bias_b)],
        compiler_params=pltpu.CompilerParams(
            dimension_semantics=("arbitrary",), vmem_limit_bytes=VMEM_LIMIT),
        name="band_attention",
    )(sink, proj, proj, proj, proj, proj, proj, proj, proj, proj, proj, *bias_a, *bias_b)


def _mlp_rows(n_rows, sub_rows, early_fn, h1_fn, o_ref, gpre_ref, wup_ref, wdown_ref,
              gpost_ref, side_jobs=()):
    n_sub = n_rows // sub_rows
    n_chunks = wup_ref.shape[1] // FF_CHUNK
    n_slots = n_sub * n_chunks * 2
    jobs = list(side_jobs)
    job_slot = {round((j + 0.5) * n_slots / len(jobs)): job for j, job in enumerate(jobs)}
    assert len(job_slot) == len(jobs) and all(0 < k <= n_slots for k in job_slot)
    rows = [slice(s * sub_rows, (s + 1) * sub_rows) for s in range(n_sub)]
    early = {0: early_fn(rows[0])} if early_fn else {}
    slot = 0
    for s in range(n_sub):
        if early_fn and s + 1 < n_sub:
            early[s + 1] = early_fn(rows[s + 1])
        h1 = h1_fn(rows[s], early.pop(s, None))
        hn = _rms(h1, gpre_ref[...]).astype(BF16)
        acc = None
        for c in range(n_chunks):
            ff = slice(c * FF_CHUNK, (c + 1) * FF_CHUNK)
            u = jnp.maximum(_dot(hn, wup_ref[:, ff]), 0.0)
            slot += 1
            if slot in job_slot:
                job_slot[slot]()
            part = _dot((u * u).astype(BF16), wdown_ref[ff, :])
            slot += 1
            if slot in job_slot:
                job_slot[slot]()
            acc = part if acc is None else acc + part
        o_ref[rows[s], :] = h1 + _rms(acc, gpost_ref[...])


def _outproj_mlp_kernel(a_ref, x_ref, wout_ref, gpost_ref,
                        gpre_ref, wup_ref, wdown_ref, gmpost_ref, o_ref):
    def out_proj(rows):
        return _dot(a_ref[rows, :], wout_ref[...])

    def h1_fn(rows, y):
        return x_ref[rows, :] + _rms(y, gpost_ref[...])

    _mlp_rows(x_ref.shape[0], OUTPROJ_SUB_ROWS, out_proj, h1_fn, o_ref,
              gpre_ref, wup_ref, wdown_ref, gmpost_ref)


def _outproj_mlp(attn, x2, w_out, g_post, g_mpre, w_up, w_down, g_mpost):
    n, d = x2.shape
    assert n % OUTPROJ_TILE == 0
    row = lambda w: pl.BlockSpec((OUTPROJ_TILE, w), lambda i: (i, 0))
    return pl.pallas_call(
        _outproj_mlp_kernel,
        grid=(n // OUTPROJ_TILE,),
        in_specs=[row(attn.shape[1]), row(d), _resident(w_out.shape), _resident((1, d)),
                  _resident((1, d)), _resident(w_up.shape), _resident(w_down.shape),
                  _resident((1, d))],
        out_specs=row(d),
        out_shape=jax.ShapeDtypeStruct((n, d), F32),
        compiler_params=pltpu.CompilerParams(
            dimension_semantics=("arbitrary",), vmem_limit_bytes=VMEM_LIMIT),
        name="outproj_mlp",
    )(attn, x2, w_out, g_post, g_mpre, w_up, w_down, g_mpost)


def _pool_mixer_jobs(h_ref, halo_fn, g_ref, pw_ref, pscale_ref, t_in_seq, y_ref):
    tm = h_ref.shape[0]
    pc = pw_ref.shape[1]
    st = {}

    def prepare():
        st["hn"] = _rms(h_ref[...], g_ref[...])
        st["halo"] = halo_fn()
        pos1 = t_in_seq * tm + 1 + lax.broadcasted_iota(jnp.int32, (tm, 1), 0)
        st["pos1"] = pos1
        st["inv_pos1"] = 1.0 / pos1.astype(F32)

    def group(gi, w):
        cols = slice(gi * pc, (gi + 1) * pc)
        hn = st["hn"][:, cols]
        wsum = jnp.concatenate([st["halo"][:, cols], hn], axis=0)
        k = 1
        while k < w:
            wsum = wsum + pltpu.roll(wsum, k, axis=0)
            k *= 2
        inv_cnt = jnp.where(st["pos1"] < w, st["inv_pos1"], 1.0 / w)
        d = (wsum[POOL_HALO:, :] * inv_cnt - hn).astype(BF16)
        y_ref[:, cols] = _dot(d, pw_ref[gi]) * pscale_ref[:, cols]

    return [prepare] + [functools.partial(group, gi, w) for gi, w in enumerate(POOL_WINDOWS)]


def _pool_mlp_kernel(h_ref, hnext_ref, gpre_ref, pw_ref, pscale_ref, gpost_ref,
                     gmpre_ref, wup_ref, wdown_ref, gmpost_ref, o_ref, y_ref, *, tiles_per_seq):
    i = pl.program_id(0)
    tm, dm = h_ref.shape
    @pl.when(i == 0)
    def _():
        for job in _pool_mixer_jobs(h_ref, lambda: jnp.zeros((POOL_HALO, dm), F32),
                                    gpre_ref, pw_ref, pscale_ref, 0, y_ref):
            job()

    y = y_ref[...]

    def h1_fn(rows, _):
        return h_ref[rows, :] + _rms(y[rows, :], gpost_ref[...])

    t_next = (i + 1) % tiles_per_seq

    def next_halo():
        keep = jnp.where(t_next == 0, 0.0, 1.0)
        return _rms(h_ref[tm - POOL_HALO:, :], gpre_ref[...]) * keep

    jobs = _pool_mixer_jobs(hnext_ref, next_halo, gpre_ref, pw_ref, pscale_ref, t_next, y_ref)
    _mlp_rows(tm, SUB_ROWS, None, h1_fn, o_ref, gmpre_ref, wup_ref, wdown_ref, gmpost_ref, jobs)


def _pool_mlp(h2, seq_len, g_pre, pool_w, pool_scale, g_post, g_mpre, w_up, w_down, g_mpost):
    n, d = h2.shape
    tiles_per_seq = seq_len // TOKEN_TILE
    n_tiles = n // TOKEN_TILE
    row = pl.BlockSpec((TOKEN_TILE, d), lambda i: (i, 0))
    nxt = pl.BlockSpec((TOKEN_TILE, d), lambda i: (jnp.minimum(i + 1, n_tiles - 1), 0))
    kern = functools.partial(_pool_mlp_kernel, tiles_per_seq=tiles_per_seq)
    return pl.pallas_call(
        kern,
        grid=(n_tiles,),
        in_specs=[row, nxt, _resident((1, d)), _resident(pool_w.shape), _resident((1, d)),
                  _resident((1, d)), _resident((1, d)), _resident(w_up.shape),
                  _resident(w_down.shape), _resident((1, d))],
        out_specs=row,
        out_shape=jax.ShapeDtypeStruct((n, d), F32),
        scratch_shapes=[pltpu.VMEM((TOKEN_TILE, d), F32)],
        compiler_params=pltpu.CompilerParams(
            dimension_semantics=("arbitrary",), vmem_limit_bytes=VMEM_LIMIT),
        name="pool_mlp",
    )(h2, h2, g_pre, pool_w, pool_scale, g_post, g_mpre, w_up, w_down, g_mpost)


def _pair_band_index(n_prev):
    band = (n_prev + 2) * CHUNK
    r = np.arange(PAIR)[:, None]
    k = np.arange(band)[None, :]
    e, qi = r // CHUNK, r % CHUNK
    kc = k // CHUNK - e
    ok = (kc >= 0) & (kc <= n_prev)
    key_pos = kc * CHUNK + k % CHUNK - n_prev * CHUNK
    return qi - key_pos, ok


def _t5_bucket(rel_kq):
    nb = T5_BUCKETS // 2
    ret = (rel_kq > 0).astype(np.int32) * nb
    n = np.abs(rel_kq)
    max_exact = nb // 2
    large = max_exact + (np.log(np.maximum(n, 1) / max_exact)
                         / math.log(T5_MAX_DIST / max_exact) * (nb - max_exact)).astype(np.int32)
    large = np.minimum(large, nb - 1)
    return ret + np.where(n < max_exact, n, large)


def _toeplitz_diagonals(table, index_of_rel, n_prev):
    band = (n_prev + 2) * CHUNK
    period = band + PAIR
    delta = np.arange(period)
    delta = np.where(delta < band, delta, delta - period)
    diag = jnp.take(table, index_of_rel(n_prev * CHUNK - delta), axis=0).T
    _, ok = _pair_band_index(n_prev)
    corner = jnp.asarray(np.where(ok, 0.0, NEG_INF * LOG2E), F32)
    return diag.astype(F32) * LOG2E, corner


def _bias_tables(relpos_a, t5_table):
    diag_a, corner_a = _toeplitz_diagonals(
        relpos_a, lambda rel: np.clip(rel, -A_MAX_REL, A_MAX_REL) + A_MAX_REL, A_PREV_CHUNKS)
    diag_b, corner_b = _toeplitz_diagonals(t5_table, lambda rel: _t5_bucket(-rel),
                                           B_PREV_CHUNKS)
    kc_a = np.arange(A_BAND) // CHUNK
    hide_a = np.zeros((PAIRS_PER_TILE + 1, 1, 1, A_BAND), bool)
    for jj in range(PAIRS_PER_TILE):
        hide_a[jj, 0, 0] = 2 * jj - A_PREV_CHUNKS + kc_a < 0
    kc_b = np.arange(B_BAND) // CHUNK
    hide_b = np.zeros((2, 1, 1, B_BAND), bool)
    hide_b[0, 0, 0] = kc_b - B_PREV_CHUNKS < 0
    hidden = lambda hide: jnp.asarray(np.where(hide[:, 0], NEG_INF * LOG2E, 0.0), F32)
    return (diag_a, corner_a, hidden(hide_a)), (diag_b, corner_b, hidden(hide_b))


def kernel(x, t5_table, e_norm_pre, e_norm_post, e_w_in, e_w_out, e_relpos_a, e_sink_b,
           o_norm_pre, o_norm_post, o_pool_w, o_pool_scale,
           mlp_norm_pre, mlp_norm_post, mlp_w_up, mlp_w_down):
    bsz, seq_len, d = x.shape
    depth = mlp_w_up.shape[0]
    a_heads = e_relpos_a.shape[2]
    b_heads = e_sink_b.shape[1]
    a_width, b_width = a_heads * HEAD_DIM, b_heads * HEAD_DIM
    b_kv_width = (b_heads // B_GROUP) * HEAD_DIM
    assert seq_len % ATTN_TILE == 0 and seq_len % TOKEN_TILE == 0
    assert a_heads == b_heads and a_heads % 2 == 0
    row = lambda v: v.reshape(1, -1).astype(F32)

    h = x.reshape(bsz * seq_len, d)
    pool_w2 = o_pool_w.reshape(o_pool_w.shape[0], -1, o_pool_w.shape[-1])
    names, casts = [], []
    for layer in range(depth):
        names += [("up", layer), ("down", layer)]
        casts += [(mlp_w_up, layer), (mlp_w_down, layer)]
        names.append(("mix", layer))
        casts.append((e_w_out, layer // 2) if layer % 2 == 0 else (pool_w2, layer // 2))
    bf = {}
    for layer in range(depth):
        i = layer // 2
        g_mpre, g_mpost = row(mlp_norm_pre[layer]), row(mlp_norm_post[layer])
        if layer % 2 == 0:
            seg = a_width
            q_segs = (0, 3)
            assert b_width == seg and (2 * b_kv_width) % (2 * LANES) == 0
            proj, cast = _inproj(h, row(e_norm_pre[i]), e_w_in[i].astype(BF16), q_segs, seg,
                                 3 * a_width + b_width, [] if bf else casts)
            bf = bf or dict(zip(names, cast))
            attn = _attention(proj, e_sink_b[i].astype(F32) * LOG2E,
                              *_bias_tables(e_relpos_a[i], t5_table),
                              seq_len, a_width, b_width, b_kv_width)
            h = _outproj_mlp(attn, h, bf["mix", layer], row(e_norm_post[i]),
                             g_mpre, bf["up", layer], bf["down", layer], g_mpost)
        else:
            h = _pool_mlp(h, seq_len, row(o_norm_pre[i]),
                          bf["mix", layer].reshape(o_pool_w.shape[1:]),
                          row(o_pool_scale[i]), row(o_norm_post[i]),
                          g_mpre, bf["up", layer], bf["down", layer], g_mpost)
    return h.reshape(bsz, seq_len, d)
```

```python
import functools
import math

import jax
import jax.numpy as jnp
import numpy as np
from jax import lax
from jax.experimental import pallas as pl
from jax.experimental.pallas import tpu as pltpu

CHUNK = 64
HEAD_DIM = 64
A_PREV_CHUNKS = 8
A_MAX_REL = 128
B_GROUP = 4
B_PREV_CHUNKS = 2
T5_BUCKETS = 32
T5_MAX_DIST = 128
POOL_WINDOWS = (2, 4, 8, 16)
RMS_EPS = 1e-6
NEG_INF = -1e30
LOG2E = math.log2(math.e)

LANES = 128
BF16_ROWS = 16
V7X_VMEM_BYTES = 64 * 1024 * 1024

TOKEN_TILE = 1024
INPROJ_TILE = 2048
OUTPROJ_TILE = 1024
SUB_ROWS = 512
OUTPROJ_SUB_ROWS = 512
ATTN_TILE = 1024
A_REACH = A_PREV_CHUNKS * CHUNK
A_REACH_PAIRS = A_REACH // (2 * CHUNK)
PAIR = 2 * CHUNK
PAIRS_PER_TILE = ATTN_TILE // PAIR
A_BAND = (A_PREV_CHUNKS + 2) * CHUNK
B_BAND = (B_PREV_CHUNKS + 2) * CHUNK
ATTN_LOOKAHEAD = 3
POOL_HALO = 16
FF_CHUNK = 2048
VMEM_LIMIT = 56 * 1024 * 1024

BF16 = jnp.bfloat16
F32 = jnp.float32


def _rms(x, g):
    ms = jnp.mean(x * x, axis=-1, keepdims=True)
    return (x * lax.rsqrt(ms + RMS_EPS)) * g


def _dot(a, b):
    return jnp.dot(a, b, preferred_element_type=F32)


def _dot_nt(a, b):
    return lax.dot_general(a, b, (((1,), (1,)), ((), ())), preferred_element_type=F32)


def _resident(shape):
    nd = len(shape)
    return pl.BlockSpec(shape, lambda *_: (0,) * nd, pipeline_mode=pl.Buffered(1))


def _inproj_kernel(x_ref, g_ref, w_ref, *refs, n_cast, seg, q_segs, q_scale, kv_b0):
    o_ref = refs[n_cast]
    for src_ref, dst_ref in zip(refs[:n_cast], refs[n_cast + 1:]):
        dst_ref[...] = src_ref[...].astype(BF16)
    lo_half = lax.broadcasted_iota(jnp.int32, (1, LANES), 1) < HEAD_DIM
    for r in range(x_ref.shape[0] // SUB_ROWS):
        rows = slice(r * SUB_ROWS, (r + 1) * SUB_ROWS)
        xb = _rms(x_ref[rows, :], g_ref[...]).astype(BF16)
        for s in range(kv_b0 // seg):
            y = _dot(xb, w_ref[:, s * seg:(s + 1) * seg])
            if s in q_segs:
                y = y * q_scale
            o_ref[rows, s * seg:(s + 1) * seg] = y.astype(BF16)
        y = _dot(xb, w_ref[:, kv_b0:])
        for j in range(y.shape[1] // LANES):
            two = y[:, j * LANES:(j + 1) * LANES]
            swapped = pltpu.roll(two, HEAD_DIM, axis=1)
            c0 = kv_b0 + 2 * j * LANES
            o_ref[rows, c0:c0 + LANES] = jnp.where(lo_half, two, swapped).astype(BF16)
            o_ref[rows, c0 + LANES:c0 + 2 * LANES] = jnp.where(lo_half, swapped, two).astype(BF16)


def _inproj(x2, g, w_in, q_segs, seg, kv_b0, casts):
    n, d = x2.shape
    out_cols = kv_b0 + 2 * (w_in.shape[1] - kv_b0)
    steps = n // INPROJ_TILE
    assert n % INPROJ_TILE == 0 and kv_b0 % seg == 0
    cast_in, cast_out, cast_shape = [], [], []
    for w, li in casts:
        _, r, c = w.shape
        assert r % (steps * BF16_ROWS) == 0
        cast_in.append(pl.BlockSpec((None, r // steps, c), lambda i, li=li: (li, i, 0)))
        cast_out.append(pl.BlockSpec((r // steps, c), lambda i: (i, 0)))
        cast_shape.append(jax.ShapeDtypeStruct((r, c), BF16))
    kern = functools.partial(_inproj_kernel, n_cast=len(casts), seg=seg, q_segs=q_segs,
                             q_scale=HEAD_DIM ** -0.5 * LOG2E, kv_b0=kv_b0)
    proj, *cast = pl.pallas_call(
        kern,
        grid=(steps,),
        in_specs=[
            pl.BlockSpec((INPROJ_TILE, d), lambda i: (i, 0)),
            _resident((1, d)),
            _resident(w_in.shape),
        ] + cast_in,
        out_specs=[pl.BlockSpec((INPROJ_TILE, out_cols), lambda i: (i, 0))] + cast_out,
        out_shape=[jax.ShapeDtypeStruct((n, out_cols), BF16)] + cast_shape,
        compiler_params=pltpu.CompilerParams(
            dimension_semantics=("arbitrary",), vmem_limit_bytes=VMEM_LIMIT),
        name="inproj",
    )(x2, g, w_in, *[w for w, _ in casts])
    return proj, cast


def _softmax_pv(s, v_ones, sink):
    m = jnp.max(s, axis=-1, keepdims=True)
    if sink is not None:
        m = jnp.maximum(m, sink)
    pb = jnp.exp2((s - m).astype(BF16))
    if isinstance(v_ones, tuple):
        w1 = v_ones[0].shape[0]
        o = _dot(pb[:, :w1], v_ones[0]) + _dot(pb[:, w1:], v_ones[1])
    else:
        o = _dot(pb, v_ones)
    den = o[:, LANES:]
    if sink is not None:
        den = den + jnp.exp2(sink - m)
    out = o[:, :LANES] * (1.0 / den)
    half = out.shape[0] // 2
    lo_half = lax.broadcasted_iota(jnp.int32, (1, LANES), 1) < HEAD_DIM
    return jnp.where(lo_half, out[:half], out[half:])


def _attn_kernel(sink_ref, qa_ref, kap_ref, kac_ref, vap_ref, vac_ref,
                 qb_ref, kbp_ref, kbc_ref, vbp_ref, vbc_ref,
                 diaga_ref, cornera_ref, hida_ref, diagb_ref, cornerb_ref, hidb_ref,
                 o_ref, biasa_ref, biasb_ref, *, tiles_per_seq, pairs_per_kv):
    i = pl.program_id(0)

    @pl.when(i == 0)
    def _():
        for diag_ref, corner_ref, hid_ref, bias_ref in (
                (diaga_ref, cornera_ref, hida_ref, biasa_ref),
                (diagb_ref, cornerb_ref, hidb_ref, biasb_ref)):
            rows, band = corner_ref.shape
            for hd in range(diag_ref.shape[0]):
                diag = jnp.broadcast_to(diag_ref[hd:hd + 1, :], (rows, diag_ref.shape[1]))
                skew = pltpu.roll(diag, 0, axis=1, stride=1, stride_axis=0)
                base = skew[:, :band] + corner_ref[...]
                for v in range(hid_ref.shape[0]):
                    bias_ref[v, hd] = base + hid_ref[v]

    oa_ref = o_ref.at[:, :qa_ref.shape[1]]
    ob_ref = o_ref.at[:, qa_ref.shape[1]:]
    first = (i % tiles_per_seq) == 0
    lo_half = lax.broadcasted_iota(jnp.int32, (1, LANES), 1) < HEAD_DIM
    zero = jnp.zeros((), BF16)
    n_head_pairs = qa_ref.shape[1] // LANES

    def lanes(hp):
        return slice(hp * LANES, (hp + 1) * LANES)

    def stacked_q(q_ref, r0, hp):
        q2 = q_ref[r0:r0 + PAIR, lanes(hp)]
        return jnp.concatenate([jnp.where(lo_half, q2, zero), jnp.where(lo_half, zero, q2)],
                               axis=0)

    def with_ones(v):
        return jnp.concatenate([v, jnp.ones(v.shape, BF16)], axis=1)

    def band_a(prev_ref, cur_ref, jj, hp):
        r0 = jj * PAIR
        if jj < A_REACH_PAIRS:
            return prev_ref[r0:, lanes(hp)], cur_ref[:r0 + PAIR, lanes(hp)]
        return (cur_ref[r0 - A_REACH:r0 + PAIR, lanes(hp)],)

    def logits_a(jj, hp):
        q = stacked_q(qa_ref, jj * PAIR, hp)
        s = jnp.concatenate([_dot_nt(q, k) for k in band_a(kap_ref, kac_ref, jj, hp)], axis=1)
        variant = jnp.where(first, jj, A_REACH_PAIRS) if jj < A_REACH_PAIRS else A_REACH_PAIRS
        return s + biasa_ref[variant, 2 * hp:2 * hp + 2].reshape(s.shape)

    def finish_a(jj, hp, s):
        v = tuple(with_ones(v) for v in band_a(vap_ref, vac_ref, jj, hp))
        return _softmax_pv(s, v if len(v) > 1 else v[0], None)

    def band_b(prev_ref, cur_ref, jj, hp):
        r0 = jj * PAIR
        kv = lanes(hp // pairs_per_kv)
        if jj == 0:
            return jnp.concatenate([prev_ref[:, kv], cur_ref[:PAIR, kv]], axis=0)
        return cur_ref[r0 - PAIR:r0 + PAIR, kv]

    def logits_b(jj, hp):
        s = _dot_nt(stacked_q(qb_ref, jj * PAIR, hp), band_b(kbp_ref, kbc_ref, jj, hp))
        variant = jnp.where(first, 0, 1) if jj == 0 else 1
        return s + biasb_ref[variant, 2 * hp:2 * hp + 2].reshape(s.shape)

    def finish_b(jj, hp, s):
        sink = jnp.concatenate(
            [jnp.full((PAIR, 1), sink_ref[2 * hp + e], F32) for e in range(2)], axis=0)
        return _softmax_pv(s, with_ones(band_b(vbp_ref, vbc_ref, jj, hp)), sink)

    mixers = {"a": (logits_a, finish_a, oa_ref), "b": (logits_b, finish_b, ob_ref)}
    units = [(mx, jj, hp) for hp in range(n_head_pairs) for jj in range(PAIRS_PER_TILE)
             for mx in ("a", "b")]
    pending = {}
    for t in range(len(units) + ATTN_LOOKAHEAD):
        if t < len(units):
            mx, jj, hp = units[t]
            pending[t] = mixers[mx][0](jj, hp)
        if t >= ATTN_LOOKAHEAD:
            mx, jj, hp = units[t - ATTN_LOOKAHEAD]
            out = mixers[mx][1](jj, hp, pending.pop(t - ATTN_LOOKAHEAD))
            mixers[mx][2][jj * PAIR:(jj + 1) * PAIR, lanes(hp)] = out.astype(BF16)


def _attention(proj, sink, bias_a, bias_b, seq_len, a_width, b_width, b_kv_width):
    n = proj.shape[0]
    tiles_per_seq = seq_len // ATTN_TILE
    kv_dup = 2 * b_kv_width
    assert a_width == b_width and kv_dup % LANES == 0 and a_width % kv_dup == 0
    sub = ATTN_TILE // PAIR
    qa, ka, va, qb = 0, 1, 2, 3
    kb = 4 * a_width // kv_dup
    vb = kb + 1

    def cur(width, c):
        return pl.BlockSpec((ATTN_TILE, width), lambda i: (i, c))

    reach_blocks = ATTN_TILE // A_REACH

    def prev(c):
        return pl.BlockSpec((A_REACH, a_width),
                            lambda i: (jnp.maximum(i * reach_blocks - 1, 0), c))

    def prev_pair(c):
        return pl.BlockSpec((PAIR, kv_dup), lambda i: (jnp.maximum(i * sub - 1, 0), c))

    kern = functools.partial(_attn_kernel, tiles_per_seq=tiles_per_seq,
                             pairs_per_kv=(b_width // LANES) // (kv_dup // LANES))
    return pl.pallas_call(
        kern,
        grid=(n // ATTN_TILE,),
        in_specs=[
            pl.BlockSpec(memory_space=pltpu.SMEM),
            cur(a_width, qa), prev(ka), cur(a_width, ka), prev(va), cur(a_width, va),
            cur(b_width, qb), prev_pair(kb), cur(kv_dup, kb), prev_pair(vb), cur(kv_dup, vb),
        ] + [_resident(t.shape) for t in (*bias_a, *bias_b)],
        out_specs=cur(a_width + b_width, 0),
        out_shape=jax.ShapeDtypeStruct((n, a_width + b_width), BF16),
        scratch_shapes=[pltpu.VMEM((hid.shape[0], diag.shape[0]) + corner.shape, F32)
                        for diag, corner, hid in (bias_a, bias_b)],
        compiler_params=pltpu.CompilerParams(
            dimension_semantics=("arbitrary",), vmem_limit_bytes=VMEM_LIMIT),
        name="band_attention",
    )(sink, proj, proj, proj, proj, proj, proj, proj, proj, proj, proj, *bias_a, *bias_b)


def _mlp_rows(n_rows, sub_rows, early_fn, h1_fn, o_ref, gpre_ref, wup_ref, wdown_ref,
              gpost_ref, side_jobs=()):
    n_sub = n_rows // sub_rows
    n_chunks = wup_ref.shape[1] // FF_CHUNK
    n_slots = n_sub * n_chunks * 2
    jobs = list(side_jobs)
    job_slot = {round((j + 0.5) * n_slots / len(jobs)): job for j, job in enumerate(jobs)}
    assert len(job_slot) == len(jobs) and all(0 < k <= n_slots for k in job_slot)
    rows = [slice(s * sub_rows, (s + 1) * sub_rows) for s in range(n_sub)]
    early = {0: early_fn(rows[0])} if early_fn else {}
    slot = 0
    for s in range(n_sub):
        if early_fn and s + 1 < n_sub:
            early[s + 1] = early_fn(rows[s + 1])
        h1 = h1_fn(rows[s], early.pop(s, None))
        hn = _rms(h1, gpre_ref[...]).astype(BF16)
        acc = None
        for c in range(n_chunks):
            ff = slice(c * FF_CHUNK, (c + 1) * FF_CHUNK)
            u = jnp.maximum(_dot(hn, wup_ref[:, ff]), 0.0)
            slot += 1
            if slot in job_slot:
                job_slot[slot]()
            part = _dot((u * u).astype(BF16), wdown_ref[ff, :])
            slot += 1
            if slot in job_slot:
                job_slot[slot]()
            acc = part if acc is None else acc + part
        o_ref[rows[s], :] = h1 + _rms(acc, gpost_ref[...])


def _outproj_mlp_kernel(a_ref, x_ref, wout_ref, gpost_ref,
                        gpre_ref, wup_ref, wdown_ref, gmpost_ref, o_ref):
    def out_proj(rows):
        return _dot(a_ref[rows, :], wout_ref[...])

    def h1_fn(rows, y):
        return x_ref[rows, :] + _rms(y, gpost_ref[...])

    _mlp_rows(x_ref.shape[0], OUTPROJ_SUB_ROWS, out_proj, h1_fn, o_ref,
              gpre_ref, wup_ref, wdown_ref, gmpost_ref)


def _outproj_mlp(attn, x2, w_out, g_post, g_mpre, w_up, w_down, g_mpost):
    n, d = x2.shape
    assert n % OUTPROJ_TILE == 0
    row = lambda w: pl.BlockSpec((OUTPROJ_TILE, w), lambda i: (i, 0))
    return pl.pallas_call(
        _outproj_mlp_kernel,
        grid=(n // OUTPROJ_TILE,),
        in_specs=[row(attn.shape[1]), row(d), _resident(w_out.shape), _resident((1, d)),
                  _resident((1, d)), _resident(w_up.shape), _resident(w_down.shape),
                  _resident((1, d))],
        out_specs=row(d),
        out_shape=jax.ShapeDtypeStruct((n, d), F32),
        compiler_params=pltpu.CompilerParams(
            dimension_semantics=("arbitrary",), vmem_limit_bytes=VMEM_LIMIT),
        name="outproj_mlp",
    )(attn, x2, w_out, g_post, g_mpre, w_up, w_down, g_mpost)


def _pool_mixer_jobs(h_ref, halo_fn, g_ref, pw_ref, pscale_ref, t_in_seq, y_ref):
    tm = h_ref.shape[0]
    pc = pw_ref.shape[1]
    st = {}

    def prepare():
        st["hn"] = _rms(h_ref[...], g_ref[...])
        st["halo"] = halo_fn()
        pos1 = t_in_seq * tm + 1 + lax.broadcasted_iota(jnp.int32, (tm, 1), 0)
        st["pos1"] = pos1
        st["inv_pos1"] = 1.0 / pos1.astype(F32)

    def group(gi, w):
        cols = slice(gi * pc, (gi + 1) * pc)
        hn = st["hn"][:, cols]
        wsum = jnp.concatenate([st["halo"][:, cols], hn], axis=0)
        k = 1
        while k < w:
            wsum = wsum + pltpu.roll(wsum, k, axis=0)
            k *= 2
        inv_cnt = jnp.where(st["pos1"] < w, st["inv_pos1"], 1.0 / w)
        d = (wsum[POOL_HALO:, :] * inv_cnt - hn).astype(BF16)
        y_ref[:, cols] = _dot(d, pw_ref[gi]) * pscale_ref[:, cols]

    return [prepare] + [functools.partial(group, gi, w) for gi, w in enumerate(POOL_WINDOWS)]


def _pool_mlp_kernel(h_ref, hnext_ref, gpre_ref, pw_ref, pscale_ref, gpost_ref,
                     gmpre_ref, wup_ref, wdown_ref, gmpost_ref, o_ref, y_ref, *, tiles_per_seq):
    i = pl.program_id(0)
    tm, dm = h_ref.shape
    @pl.when(i == 0)
    def _():
        for job in _pool_mixer_jobs(h_ref, lambda: jnp.zeros((POOL_HALO, dm), F32),
                                    gpre_ref, pw_ref, pscale_ref, 0, y_ref):
            job()

    y = y_ref[...]

    def h1_fn(rows, _):
        return h_ref[rows, :] + _rms(y[rows, :], gpost_ref[...])

    t_next = (i + 1) % tiles_per_seq

    def next_halo():
        keep = jnp.where(t_next == 0, 0.0, 1.0)
        return _rms(h_ref[tm - POOL_HALO:, :], gpre_ref[...]) * keep

    jobs = _pool_mixer_jobs(hnext_ref, next_halo, gpre_ref, pw_ref, pscale_ref, t_next, y_ref)
    _mlp_rows(tm, SUB_ROWS, None, h1_fn, o_ref, gmpre_ref, wup_ref, wdown_ref, gmpost_ref, jobs)


def _pool_mlp(h2, seq_len, g_pre, pool_w, pool_scale, g_post, g_mpre, w_up, w_down, g_mpost):
    n, d = h2.shape
    tiles_per_seq = seq_len // TOKEN_TILE
    n_tiles = n // TOKEN_TILE
    row = pl.BlockSpec((TOKEN_TILE, d), lambda i: (i, 0))
    nxt = pl.BlockSpec((TOKEN_TILE, d), lambda i: (jnp.minimum(i + 1, n_tiles - 1), 0))
    kern = functools.partial(_pool_mlp_kernel, tiles_per_seq=tiles_per_seq)
    return pl.pallas_call(
        kern,
        grid=(n_tiles,),
        in_specs=[row, nxt, _resident((1, d)), _resident(pool_w.shape), _resident((1, d)),
                  _resident((1, d)), _resident((1, d)), _resident(w_up.shape),
                  _resident(w_down.shape), _resident((1, d))],
        out_specs=row,
        out_shape=jax.ShapeDtypeStruct((n, d), F32),
        scratch_shapes=[pltpu.VMEM((TOKEN_TILE, d), F32)],
        compiler_params=pltpu.CompilerParams(
            dimension_semantics=("arbitrary",), vmem_limit_bytes=VMEM_LIMIT),
        name="pool_mlp",
    )(h2, h2, g_pre, pool_w, pool_scale, g_post, g_mpre, w_up, w_down, g_mpost)


def _pair_band_index(n_prev):
    band = (n_prev + 2) * CHUNK
    r = np.arange(PAIR)[:, None]
    k = np.arange(band)[None, :]
    e, qi = r // CHUNK, r % CHUNK
    kc = k // CHUNK - e
    ok = (kc >= 0) & (kc <= n_prev)
    key_pos = kc * CHUNK + k % CHUNK - n_prev * CHUNK
    return qi - key_pos, ok


def _t5_bucket(rel_kq):
    nb = T5_BUCKETS // 2
    ret = (rel_kq > 0).astype(np.int32) * nb
    n = np.abs(rel_kq)
    max_exact = nb // 2
    large = max_exact + (np.log(np.maximum(n, 1) / max_exact)
                         / math.log(T5_MAX_DIST / max_exact) * (nb - max_exact)).astype(np.int32)
    large = np.minimum(large, nb - 1)
    return ret + np.where(n < max_exact, n, large)


def _toeplitz_diagonals(table, index_of_rel, n_prev):
    band = (n_prev + 2) * CHUNK
    period = band + PAIR
    delta = np.arange(period)
    delta = np.where(delta < band, delta, delta - period)
    diag = jnp.take(table, index_of_rel(n_prev * CHUNK - delta), axis=0).T
    _, ok = _pair_band_index(n_prev)
    corner = jnp.asarray(np.where(ok, 0.0, NEG_INF * LOG2E), F32)
    return diag.astype(F32) * LOG2E, corner


def _bias_tables(relpos_a, t5_table):
    diag_a, corner_a = _toeplitz_diagonals(
        relpos_a, lambda rel: np.clip(rel, -A_MAX_REL, A_MAX_REL) + A_MAX_REL, A_PREV_CHUNKS)
    diag_b, corner_b = _toeplitz_diagonals(t5_table, lambda rel: _t5_bucket(-rel),
                                           B_PREV_CHUNKS)
    kc_a = np.arange(A_BAND) // CHUNK
    hide_a = np.zeros((A_REACH_PAIRS + 1, 1, 1, A_BAND), bool)
    for jj in range(A_REACH_PAIRS):
        hide_a[jj, 0, 0] = 2 * jj - A_PREV_CHUNKS + kc_a < 0
    kc_b = np.arange(B_BAND) // CHUNK
    hide_b = np.zeros((2, 1, 1, B_BAND), bool)
    hide_b[0, 0, 0] = kc_b - B_PREV_CHUNKS < 0
    hidden = lambda hide: jnp.asarray(np.where(hide[:, 0], NEG_INF * LOG2E, 0.0), F32)
    return (diag_a, corner_a, hidden(hide_a)), (diag_b, corner_b, hidden(hide_b))


def kernel(x, t5_table, e_norm_pre, e_norm_post, e_w_in, e_w_out, e_relpos_a, e_sink_b,
           o_norm_pre, o_norm_post, o_pool_w, o_pool_scale,
           mlp_norm_pre, mlp_norm_post, mlp_w_up, mlp_w_down):
    bsz, seq_len, d = x.shape
    depth = mlp_w_up.shape[0]
    a_heads = e_relpos_a.shape[2]
    b_heads = e_sink_b.shape[1]
    a_width, b_width = a_heads * HEAD_DIM, b_heads * HEAD_DIM
    b_kv_width = (b_heads // B_GROUP) * HEAD_DIM
    assert seq_len % ATTN_TILE == 0 and seq_len % TOKEN_TILE == 0
    assert a_heads == b_heads and a_heads % 2 == 0
    row = lambda v: v.reshape(1, -1).astype(F32)

    h = x.reshape(bsz * seq_len, d)
    pool_w2 = o_pool_w.reshape(o_pool_w.shape[0], -1, o_pool_w.shape[-1])
    names, casts = [], []
    for layer in range(depth):
        names += [("up", layer), ("down", layer)]
        casts += [(mlp_w_up, layer), (mlp_w_down, layer)]
        names.append(("mix", layer))
        casts.append((e_w_out, layer // 2) if layer % 2 == 0 else (pool_w2, layer // 2))
    bf = {}
    for layer in range(depth):
        i = layer // 2
        g_mpre, g_mpost = row(mlp_norm_pre[layer]), row(mlp_norm_post[layer])
        if layer % 2 == 0:
            seg = a_width
            q_segs = (0, 3)
            assert b_width == seg and (2 * b_kv_width) % (2 * LANES) == 0
            proj, cast = _inproj(h, row(e_norm_pre[i]), e_w_in[i].astype(BF16), q_segs, seg,
                                 3 * a_width + b_width, [] if bf else casts)
            bf = bf or dict(zip(names, cast))
            attn = _attention(proj, e_sink_b[i].astype(F32) * LOG2E,
                              *_bias_tables(e_relpos_a[i], t5_table),
                              seq_len, a_width, b_width, b_kv_width)
            h = _outproj_mlp(attn, h, bf["mix", layer], row(e_norm_post[i]),
                             g_mpre, bf["up", layer], bf["down", layer], g_mpost)
        else:
            h = _pool_mlp(h, seq_len, row(o_norm_pre[i]),
                          bf["mix", layer].reshape(o_pool_w.shape[1:]),
                          row(o_pool_scale[i]), row(o_norm_post[i]),
                          g_mpre, bf["up", layer], bf["down", layer], g_mpost)
    return h.reshape(bsz, seq_len, d)
```

```python
import functools
import math

import jax
import jax.numpy as jnp
import numpy as np
from jax import lax
from jax.experimental import pallas as pl
from jax.experimental.pallas import tpu as pltpu

CHUNK = 64
HEAD_DIM = 64
A_PREV_CHUNKS = 8
A_MAX_REL = 128
B_GROUP = 4
B_PREV_CHUNKS = 2
T5_BUCKETS = 32
T5_MAX_DIST = 128
POOL_WINDOWS = (2, 4, 8, 16)
RMS_EPS = 1e-6
NEG_INF = -1e30
LOG2E = math.log2(math.e)

LANES = 128
BF16_ROWS = 16
V7X_VMEM_BYTES = 64 * 1024 * 1024

TOKEN_TILE = 1024
INPROJ_TILE = 2048
OUTPROJ_TILE = 1024
SUB_ROWS = 512
OUTPROJ_SUB_ROWS = 512
ATTN_TILE = 512
PAIR = 2 * CHUNK
PAIRS_PER_TILE = ATTN_TILE // PAIR
A_BAND = (A_PREV_CHUNKS + 2) * CHUNK
B_BAND = (B_PREV_CHUNKS + 2) * CHUNK
ATTN_LOOKAHEAD = 3
POOL_HALO = 16
FF_CHUNK = 2048
VMEM_LIMIT = V7X_VMEM_BYTES * 7 // 8

BF16 = jnp.bfloat16
F32 = jnp.float32


def _rms(x, g):
    ms = jnp.mean(x * x, axis=-1, keepdims=True)
    return (x * lax.rsqrt(ms + RMS_EPS)) * g


def _dot(a, b):
    return jnp.dot(a, b, preferred_element_type=F32)


def _dot_nt(a, b):
    return lax.dot_general(a, b, (((1,), (1,)), ((), ())), preferred_element_type=F32)


def _resident(shape):
    nd = len(shape)
    return pl.BlockSpec(shape, lambda *_: (0,) * nd, pipeline_mode=pl.Buffered(1))


def _inproj_kernel(x_ref, g_ref, w_ref, *refs, n_cast, seg, q_segs, q_scale, kv_b0):
    o_ref = refs[n_cast]
    for src_ref, dst_ref in zip(refs[:n_cast], refs[n_cast + 1:]):
        dst_ref[...] = src_ref[...].astype(BF16)
    lo_half = lax.broadcasted_iota(jnp.int32, (1, LANES), 1) < HEAD_DIM
    for r in range(x_ref.shape[0] // SUB_ROWS):
        rows = slice(r * SUB_ROWS, (r + 1) * SUB_ROWS)
        xb = _rms(x_ref[rows, :], g_ref[...]).astype(BF16)
        for s in range(kv_b0 // seg):
            y = _dot(xb, w_ref[:, s * seg:(s + 1) * seg])
            if s in q_segs:
                y = y * q_scale
            o_ref[rows, s * seg:(s + 1) * seg] = y.astype(BF16)
        y = _dot(xb, w_ref[:, kv_b0:])
        for j in range(y.shape[1] // LANES):
            two = y[:, j * LANES:(j + 1) * LANES]
            swapped = pltpu.roll(two, HEAD_DIM, axis=1)
            c0 = kv_b0 + 2 * j * LANES
            o_ref[rows, c0:c0 + LANES] = jnp.where(lo_half, two, swapped).astype(BF16)
            o_ref[rows, c0 + LANES:c0 + 2 * LANES] = jnp.where(lo_half, swapped, two).astype(BF16)


def _inproj(x2, g, w_in, q_segs, seg, kv_b0, casts):
    n, d = x2.shape
    out_cols = kv_b0 + 2 * (w_in.shape[1] - kv_b0)
    steps = n // INPROJ_TILE
    assert n % INPROJ_TILE == 0 and kv_b0 % seg == 0
    cast_in, cast_out, cast_shape = [], [], []
    for w, li in casts:
        _, r, c = w.shape
        assert r % (steps * BF16_ROWS) == 0
        cast_in.append(pl.BlockSpec((None, r // steps, c), lambda i, li=li: (li, i, 0)))
        cast_out.append(pl.BlockSpec((r // steps, c), lambda i: (i, 0)))
        cast_shape.append(jax.ShapeDtypeStruct((r, c), BF16))
    kern = functools.partial(_inproj_kernel, n_cast=len(casts), seg=seg, q_segs=q_segs,
                             q_scale=HEAD_DIM ** -0.5 * LOG2E, kv_b0=kv_b0)
    proj, *cast = pl.pallas_call(
        kern,
        grid=(steps,),
        in_specs=[
            pl.BlockSpec((INPROJ_TILE, d), lambda i: (i, 0)),
            _resident((1, d)),
            _resident(w_in.shape),
        ] + cast_in,
        out_specs=[pl.BlockSpec((INPROJ_TILE, out_cols), lambda i: (i, 0))] + cast_out,
        out_shape=[jax.ShapeDtypeStruct((n, out_cols), BF16)] + cast_shape,
        compiler_params=pltpu.CompilerParams(
            dimension_semantics=("arbitrary",), vmem_limit_bytes=VMEM_LIMIT),
        name="inproj",
    )(x2, g, w_in, *[w for w, _ in casts])
    return proj, cast


def _softmax_pv(s, v_ones, sink):
    m = jnp.max(s, axis=-1, keepdims=True)
    if sink is not None:
        m = jnp.maximum(m, sink)
    pb = jnp.exp2((s - m).astype(BF16))
    if isinstance(v_ones, tuple):
        w1 = v_ones[0].shape[0]
        o = _dot(pb[:, :w1], v_ones[0]) + _dot(pb[:, w1:], v_ones[1])
    else:
        o = _dot(pb, v_ones)
    den = o[:, LANES:]
    if sink is not None:
        den = den + jnp.exp2(sink - m)
    out = o[:, :LANES] * (1.0 / den)
    half = out.shape[0] // 2
    lo_half = lax.broadcasted_iota(jnp.int32, (1, LANES), 1) < HEAD_DIM
    return jnp.where(lo_half, out[:half], out[half:])


def _attn_kernel(sink_ref, qa_ref, kap_ref, kac_ref, vap_ref, vac_ref,
                 qb_ref, kbp_ref, kbc_ref, vbp_ref, vbc_ref,
                 diaga_ref, cornera_ref, hida_ref, diagb_ref, cornerb_ref, hidb_ref,
                 o_ref, biasa_ref, biasb_ref, *, tiles_per_seq, pairs_per_kv):
    i = pl.program_id(0)

    @pl.when(i == 0)
    def _():
        for diag_ref, corner_ref, hid_ref, bias_ref in (
                (diaga_ref, cornera_ref, hida_ref, biasa_ref),
                (diagb_ref, cornerb_ref, hidb_ref, biasb_ref)):
            rows, band = corner_ref.shape
            for hd in range(diag_ref.shape[0]):
                diag = jnp.broadcast_to(diag_ref[hd:hd + 1, :], (rows, diag_ref.shape[1]))
                skew = pltpu.roll(diag, 0, axis=1, stride=1, stride_axis=0)
                base = skew[:, :band] + corner_ref[...]
                for v in range(hid_ref.shape[0]):
                    bias_ref[v, hd] = base + hid_ref[v]

    oa_ref = o_ref.at[:, :qa_ref.shape[1]]
    ob_ref = o_ref.at[:, qa_ref.shape[1]:]
    first = (i % tiles_per_seq) == 0
    lo_half = lax.broadcasted_iota(jnp.int32, (1, LANES), 1) < HEAD_DIM
    zero = jnp.zeros((), BF16)
    n_head_pairs = qa_ref.shape[1] // LANES

    def lanes(hp):
        return slice(hp * LANES, (hp + 1) * LANES)

    def stacked_q(q_ref, r0, hp):
        q2 = q_ref[r0:r0 + PAIR, lanes(hp)]
        return jnp.concatenate([jnp.where(lo_half, q2, zero), jnp.where(lo_half, zero, q2)],
                               axis=0)

    def with_ones(v):
        return jnp.concatenate([v, jnp.ones(v.shape, BF16)], axis=1)

    def logits_a(jj, hp):
        r0 = jj * PAIR
        q = stacked_q(qa_ref, r0, hp)
        s = jnp.concatenate([_dot_nt(q, kap_ref[r0:, lanes(hp)]),
                             _dot_nt(q, kac_ref[:r0 + PAIR, lanes(hp)])], axis=1)
        bias = biasa_ref[jnp.where(first, jj, PAIRS_PER_TILE), 2 * hp:2 * hp + 2]
        return s + bias.reshape(s.shape)

    def finish_a(jj, hp, s):
        r0 = jj * PAIR
        return _softmax_pv(s, (with_ones(vap_ref[r0:, lanes(hp)]),
                               with_ones(vac_ref[:r0 + PAIR, lanes(hp)])), None)

    def band_b(prev_ref, cur_ref, jj, hp):
        r0 = jj * PAIR
        kv = lanes(hp // pairs_per_kv)
        if jj == 0:
            return jnp.concatenate([prev_ref[:, kv], cur_ref[:PAIR, kv]], axis=0)
        return cur_ref[r0 - PAIR:r0 + PAIR, kv]

    def logits_b(jj, hp):
        s = _dot_nt(stacked_q(qb_ref, jj * PAIR, hp), band_b(kbp_ref, kbc_ref, jj, hp))
        variant = jnp.where(first, 0, 1) if jj == 0 else 1
        return s + biasb_ref[variant, 2 * hp:2 * hp + 2].reshape(s.shape)

    def finish_b(jj, hp, s):
        sink = jnp.concatenate(
            [jnp.full((PAIR, 1), sink_ref[2 * hp + e], F32) for e in range(2)], axis=0)
        return _softmax_pv(s, with_ones(band_b(vbp_ref, vbc_ref, jj, hp)), sink)

    mixers = {"a": (logits_a, finish_a, oa_ref), "b": (logits_b, finish_b, ob_ref)}
    units = [(mx, jj, hp) for hp in range(n_head_pairs) for jj in range(PAIRS_PER_TILE)
             for mx in ("a", "b")]
    pending = {}
    for t in range(len(units) + ATTN_LOOKAHEAD):
        if t < len(units):
            mx, jj, hp = units[t]
            pending[t] = mixers[mx][0](jj, hp)
        if t >= ATTN_LOOKAHEAD:
            mx, jj, hp = units[t - ATTN_LOOKAHEAD]
            out = mixers[mx][1](jj, hp, pending.pop(t - ATTN_LOOKAHEAD))
            mixers[mx][2][jj * PAIR:(jj + 1) * PAIR, lanes(hp)] = out.astype(BF16)


def _attention(proj, sink, bias_a, bias_b, seq_len, a_width, b_width, b_kv_width):
    n = proj.shape[0]
    tiles_per_seq = seq_len // ATTN_TILE
    kv_dup = 2 * b_kv_width
    assert a_width == b_width and kv_dup % LANES == 0 and a_width % kv_dup == 0
    sub = ATTN_TILE // PAIR
    qa, ka, va, qb = 0, 1, 2, 3
    kb = 4 * a_width // kv_dup
    vb = kb + 1

    def cur(width, c):
        return pl.BlockSpec((ATTN_TILE, width), lambda i: (i, c))

    def prev(c):
        return pl.BlockSpec((ATTN_TILE, a_width), lambda i: (jnp.maximum(i - 1, 0), c))

    def prev_pair(c):
        return pl.BlockSpec((PAIR, kv_dup), lambda i: (jnp.maximum(i * sub - 1, 0), c))

    kern = functools.partial(_attn_kernel, tiles_per_seq=tiles_per_seq,
                             pairs_per_kv=(b_width // LANES) // (kv_dup // LANES))
    return pl.pallas_call(
        kern,
        grid=(n // ATTN_TILE,),
        in_specs=[
            pl.BlockSpec(memory_space=pltpu.SMEM),
            cur(a_width, qa), prev(ka), cur(a_width, ka), prev(va), cur(a_width, va),
            cur(b_width, qb), prev_pair(kb), cur(kv_dup, kb), prev_pair(vb), cur(kv_dup, vb),
        ] + [_resident(t.shape) for t in (*bias_a, *bias_b)],
        out_specs=cur(a_width + b_width, 0),
        out_shape=jax.ShapeDtypeStruct((n, a_width + b_width), BF16),
        scratch_shapes=[pltpu.VMEM((hid.shape[0], diag.shape[0]) + corner.shape, F32)
                        for diag, corner, hid in (bias_a, bias_b)],
        compiler_params=pltpu.CompilerParams(
            dimension_semantics=("arbitrary",), vmem_limit_bytes=VMEM_LIMIT),
        name="band_attention",
    )(sink, proj, proj, proj, proj, proj, proj, proj, proj, proj, proj, *bias_a, *bias_b)


def _mlp_rows(n_rows, sub_rows, early_fn, h1_fn, o_ref, gpre_ref, wup_ref, wdown_ref,
              gpost_ref, side_jobs=()):
    n_sub = n_rows // sub_rows
    n_chunks = wup_ref.shape[1] // FF_CHUNK
    n_slots = n_sub * n_chunks * 2
    jobs = list(side_jobs)
    job_slot = {round((j + 0.5) * n_slots / len(jobs)): job for j, job in enumerate(jobs)}
    assert len(job_slot) == len(jobs) and all(0 < k <= n_slots for k in job_slot)
    rows = [slice(s * sub_rows, (s + 1) * sub_rows) for s in range(n_sub)]
    early = {0: early_fn(rows[0])} if early_fn else {}
    slot = 0
    for s in range(n_sub):
        if early_fn and s + 1 < n_sub:
            early[s + 1] = early_fn(rows[s + 1])
        h1 = h1_fn(rows[s], early.pop(s, None))
        hn = _rms(h1, gpre_ref[...]).astype(BF16)
        acc = None
        for c in range(n_chunks):
            ff = slice(c * FF_CHUNK, (c + 1) * FF_CHUNK)
            u = jnp.maximum(_dot(hn, wup_ref[:, ff]), 0.0)
            slot += 1
            if slot in job_slot:
                job_slot[slot]()
            part = _dot((u * u).astype(BF16), wdown_ref[ff, :])
            slot += 1
            if slot in job_slot:
                job_slot[slot]()
            acc = part if acc is None else acc + part
        o_ref[rows[s], :] = h1 + _rms(acc, gpost_ref[...])


def _outproj_mlp_kernel(a_ref, x_ref, wout_ref, gpost_ref,
                        gpre_ref, wup_ref, wdown_ref, gmpost_ref, o_ref):
    def out_proj(rows):
        return _dot(a_ref[rows, :], wout_ref[...])

    def h1_fn(rows, y):
        return x_ref[rows, :] + _rms(y, gpost_ref[...])

    _mlp_rows(x_ref.shape[0], OUTPROJ_SUB_ROWS, out_proj, h1_fn, o_ref,
              gpre_ref, wup_ref, wdown_ref, gmpost_ref)


def _outproj_mlp(attn, x2, w_out, g_post, g_mpre, w_up, w_down, g_mpost):
    n, d = x2.shape
    assert n % OUTPROJ_TILE == 0
    row = lambda w: pl.BlockSpec((OUTPROJ_TILE, w), lambda i: (i, 0))
    return pl.pallas_call(
        _outproj_mlp_kernel,
        grid=(n // OUTPROJ_TILE,),
        in_specs=[row(attn.shape[1]), row(d), _resident(w_out.shape), _resident((1, d)),
                  _resident((1, d)), _resident(w_up.shape), _resident(w_down.shape),
                  _resident((1, d))],
        out_specs=row(d),
        out_shape=jax.ShapeDtypeStruct((n, d), F32),
        compiler_params=pltpu.CompilerParams(
            dimension_semantics=("arbitrary",), vmem_limit_bytes=VMEM_LIMIT),
        name="outproj_mlp",
    )(attn, x2, w_out, g_post, g_mpre, w_up, w_down, g_mpost)


def _pool_mixer_jobs(h_ref, halo_fn, g_ref, pw_ref, pscale_ref, t_in_seq, y_ref):
    tm = h_ref.shape[0]
    pc = pw_ref.shape[1]
    st = {}

    def prepare():
        st["hn"] = _rms(h_ref[...], g_ref[...])
        st["halo"] = halo_fn()
        pos1 = t_in_seq * tm + 1 + lax.broadcasted_iota(jnp.int32, (tm, 1), 0)
        st["pos1"] = pos1
        st["inv_pos1"] = 1.0 / pos1.astype(F32)

    def group(gi, w):
        cols = slice(gi * pc, (gi + 1) * pc)
        hn = st["hn"][:, cols]
        wsum = jnp.concatenate([st["halo"][:, cols], hn], axis=0)
        k = 1
        while k < w:
            wsum = wsum + pltpu.roll(wsum, k, axis=0)
            k *= 2
        inv_cnt = jnp.where(st["pos1"] < w, st["inv_pos1"], 1.0 / w)
        d = (wsum[POOL_HALO:, :] * inv_cnt - hn).astype(BF16)
        y_ref[:, cols] = _dot(d, pw_ref[gi]) * pscale_ref[:, cols]

    return [prepare] + [functools.partial(group, gi, w) for gi, w in enumerate(POOL_WINDOWS)]


def _pool_mlp_kernel(h_ref, hnext_ref, gpre_ref, pw_ref, pscale_ref, gpost_ref,
                     gmpre_ref, wup_ref, wdown_ref, gmpost_ref, o_ref, y_ref, *, tiles_per_seq):
    i = pl.program_id(0)
    tm, dm = h_ref.shape
    @pl.when(i == 0)
    def _():
        for job in _pool_mixer_jobs(h_ref, lambda: jnp.zeros((POOL_HALO, dm), F32),
                                    gpre_ref, pw_ref, pscale_ref, 0, y_ref):
            job()

    y = y_ref[...]

    def h1_fn(rows, _):
        return h_ref[rows, :] + _rms(y[rows, :], gpost_ref[...])

    t_next = (i + 1) % tiles_per_seq

    def next_halo():
        keep = jnp.where(t_next == 0, 0.0, 1.0)
        return _rms(h_ref[tm - POOL_HALO:, :], gpre_ref[...]) * keep

    jobs = _pool_mixer_jobs(hnext_ref, next_halo, gpre_ref, pw_ref, pscale_ref, t_next, y_ref)
    _mlp_rows(tm, SUB_ROWS, None, h1_fn, o_ref, gmpre_ref, wup_ref, wdown_ref, gmpost_ref, jobs)


def _pool_mlp(h2, seq_len, g_pre, pool_w, pool_scale, g_post, g_mpre, w_up, w_down, g_mpost):
    n, d = h2.shape
    tiles_per_seq = seq_len // TOKEN_TILE
    n_tiles = n // TOKEN_TILE
    row = pl.BlockSpec((TOKEN_TILE, d), lambda i: (i, 0))
    nxt = pl.BlockSpec((TOKEN_TILE, d), lambda i: (jnp.minimum(i + 1, n_tiles - 1), 0))
    kern = functools.partial(_pool_mlp_kernel, tiles_per_seq=tiles_per_seq)
    return pl.pallas_call(
        kern,
        grid=(n_tiles,),
        in_specs=[row, nxt, _resident((1, d)), _resident(pool_w.shape), _resident((1, d)),
                  _resident((1, d)), _resident((1, d)), _resident(w_up.shape),
                  _resident(w_down.shape), _resident((1, d))],
        out_specs=row,
        out_shape=jax.ShapeDtypeStruct((n, d), F32),
        scratch_shapes=[pltpu.VMEM((TOKEN_TILE, d), F32)],
        compiler_params=pltpu.CompilerParams(
            dimension_semantics=("arbitrary",), vmem_limit_bytes=VMEM_LIMIT),
        name="pool_mlp",
    )(h2, h2, g_pre, pool_w, pool_scale, g_post, g_mpre, w_up, w_down, g_mpost)


def _pair_band_index(n_prev):
    band = (n_prev + 2) * CHUNK
    r = np.arange(PAIR)[:, None]
    k = np.arange(band)[None, :]
    e, qi = r // CHUNK, r % CHUNK
    kc = k // CHUNK - e
    ok = (kc >= 0) & (kc <= n_prev)
    key_pos = kc * CHUNK + k % CHUNK - n_prev * CHUNK
    return qi - key_pos, ok


def _t5_bucket(rel_kq):
    nb = T5_BUCKETS // 2
    ret = (rel_kq > 0).astype(np.int32) * nb
    n = np.abs(rel_kq)
    max_exact = nb // 2
    large = max_exact + (np.log(np.maximum(n, 1) / max_exact)
                         / math.log(T5_MAX_DIST / max_exact) * (nb - max_exact)).astype(np.int32)
    large = np.minimum(large, nb - 1)
    return ret + np.where(n < max_exact, n, large)


def _toeplitz_diagonals(table, index_of_rel, n_prev):
    band = (n_prev + 2) * CHUNK
    period = band + PAIR
    delta = np.arange(period)
    delta = np.where(delta < band, delta, delta - period)
    diag = jnp.take(table, index_of_rel(n_prev * CHUNK - delta), axis=0).T
    _, ok = _pair_band_index(n_prev)
    corner = jnp.asarray(np.where(ok, 0.0, NEG_INF * LOG2E), F32)
    return diag.astype(F32) * LOG2E, corner


def _bias_tables(relpos_a, t5_table):
    diag_a, corner_a = _toeplitz_diagonals(
        relpos_a, lambda rel: np.clip(rel, -A_MAX_REL, A_MAX_REL) + A_MAX_REL, A_PREV_CHUNKS)
    diag_b, corner_b = _toeplitz_diagonals(t5_table, lambda rel: _t5_bucket(-rel),
                                           B_PREV_CHUNKS)
    kc_a = np.arange(A_BAND) // CHUNK
    hide_a = np.zeros((PAIRS_PER_TILE + 1, 1, 1, A_BAND), bool)
    for jj in range(PAIRS_PER_TILE):
        hide_a[jj, 0, 0] = 2 * jj - A_PREV_CHUNKS + kc_a < 0
    kc_b = np.arange(B_BAND) // CHUNK
    hide_b = np.zeros((2, 1, 1, B_BAND), bool)
    hide_b[0, 0, 0] = kc_b - B_PREV_CHUNKS < 0
    hidden = lambda hide: jnp.asarray(np.where(hide[:, 0], NEG_INF * LOG2E, 0.0), F32)
    return (diag_a, corner_a, hidden(hide_a)), (diag_b, corner_b, hidden(hide_b))


def kernel(x, t5_table, e_norm_pre, e_norm_post, e_w_in, e_w_out, e_relpos_a, e_sink_b,
           o_norm_pre, o_norm_post, o_pool_w, o_pool_scale,
           mlp_norm_pre, mlp_norm_post, mlp_w_up, mlp_w_down):
    bsz, seq_len, d = x.shape
    depth = mlp_w_up.shape[0]
    a_heads = e_relpos_a.shape[2]
    b_heads = e_sink_b.shape[1]
    a_width, b_width = a_heads * HEAD_DIM, b_heads * HEAD_DIM
    b_kv_width = (b_heads // B_GROUP) * HEAD_DIM
    assert seq_len % ATTN_TILE == 0 and seq_len % TOKEN_TILE == 0
    assert a_heads == b_heads and a_heads % 2 == 0
    row = lambda v: v.reshape(1, -1).astype(F32)

    h = x.reshape(bsz * seq_len, d)
    pool_w2 = o_pool_w.reshape(o_pool_w.shape[0], -1, o_pool_w.shape[-1])
    names, casts = [], []
    for layer in range(depth):
        names += [("up", layer), ("down", layer)]
        casts += [(mlp_w_up, layer), (mlp_w_down, layer)]
        names.append(("mix", layer))
        casts.append((e_w_out, layer // 2) if layer % 2 == 0 else (pool_w2, layer // 2))
    bf = {}
    for layer in range(depth):
        i = layer // 2
        g_mpre, g_mpost = row(mlp_norm_pre[layer]), row(mlp_norm_post[layer])
        if layer % 2 == 0:
            seg = a_width
            q_segs = (0, 3)
            assert b_width == seg and (2 * b_kv_width) % (2 * LANES) == 0
            proj, cast = _inproj(h, row(e_norm_pre[i]), e_w_in[i].astype(BF16), q_segs, seg,
                                 3 * a_width + b_width, [] if bf else casts)
            bf = bf or dict(zip(names, cast))
            attn = _attention(proj, e_sink_b[i].astype(F32) * LOG2E,
                              *_bias_tables(e_relpos_a[i], t5_table),
                              seq_len, a_width, b_width, b_kv_width)
            h = _outproj_mlp(attn, h, bf["mix", layer], row(e_norm_post[i]),
                             g_mpre, bf["up", layer], bf["down", layer], g_mpost)
        else:
            h = _pool_mlp(h, seq_len, row(o_norm_pre[i]),
                          bf["mix", layer].reshape(o_pool_w.shape[1:]),
                          row(o_pool_scale[i]), row(o_norm_post[i]),
                          g_mpre, bf["up", layer], bf["down", layer], g_mpost)
    return h.reshape(bsz, seq_len, d)
```
